```python
import jax, jax.numpy as jnp
from jax import lax
import numpy as np

D_MODEL = 4096
BATCH = 8
SEQ = 2048
DEPTH = 2

N_MIXERS = 2
N_A_LAYERS = (DEPTH + 1) // 2
N_B_LAYERS = DEPTH // 2
PLE_DIM = 256
RMS_EPS = 1e-6

ATT_HEAD_DIM = 128
ATT_HEADS = D_MODEL // ATT_HEAD_DIM
ATT_KV_HEADS = ATT_HEADS // 4
ATT_GROUP = ATT_HEADS // ATT_KV_HEADS
ROT_DIM = ATT_HEAD_DIM // 4
ROPE_THETA = 500000.0
IDX_HEADS = 64
IDX_HEAD_DIM = 128
IDX_TOPK_MAX = 256
DSA_QBLOCK = 64
A_SIZES = (ATT_HEADS * ATT_HEAD_DIM, ATT_KV_HEADS * ATT_HEAD_DIM, ATT_KV_HEADS * ATT_HEAD_DIM,
           IDX_HEADS * IDX_HEAD_DIM, IDX_HEAD_DIM, IDX_HEADS)
A_COLS = sum(A_SIZES)

RET_QK_DIM = 256
RET_HEADS = D_MODEL // RET_QK_DIM
RET_V_DIM = 2 * RET_QK_DIM
RET_THETA = 10000.0
RET_CHUNK = 128
B_SIZES = (RET_HEADS * RET_QK_DIM, RET_HEADS * RET_QK_DIM, RET_HEADS * RET_V_DIM, RET_HEADS * RET_V_DIM)
B_COLS = sum(B_SIZES)

N_GROUPS = 8
EXPERTS_PER_GROUP = 8
EXPERT_TOPK = 2
EXPERT_FF = D_MODEL // 16

kernel_name = "hybrid_dsa_retention_hmoe_ple"


def rms_norm(x, gain):
    xf = x.astype(jnp.float32)
    y = xf * lax.rsqrt(jnp.mean(xf * xf, axis=-1, keepdims=True) + RMS_EPS)
    return (y * gain.astype(jnp.float32)).astype(x.dtype)


def split_cols(a, sizes):
    out, start = [], 0
    for n in sizes:
        out.append(a[..., start:start + n])
        start += n
    return out


def rotary_angles(positions, dim, theta):
    inv = theta ** (-jnp.arange(0, dim, 2, dtype=jnp.float32) / dim)
    ang = positions.astype(jnp.float32)[..., None] * inv
    return jnp.cos(ang)[:, :, None, :], jnp.sin(ang)[:, :, None, :]


def apply_rotary(x, cos, sin):
    half = cos.shape[-1]
    xr = x[..., :2 * half].astype(jnp.float32)
    x1, x2 = xr[..., :half], xr[..., half:]
    rot = jnp.concatenate([x1 * cos - x2 * sin, x1 * sin + x2 * cos], axis=-1).astype(x.dtype)
    return jnp.concatenate([rot, x[..., 2 * half:]], axis=-1)


def dsa_attention(h, positions, w_in, w_out, q_norm, k_norm, idx_k_norm):
    B, S, _ = h.shape
    q, k, v, qi, ki, wi = split_cols(h @ w_in, A_SIZES)
    cos, sin = rotary_angles(positions, ROT_DIM, ROPE_THETA)
    q = apply_rotary(rms_norm(q.reshape(B, S, ATT_HEADS, ATT_HEAD_DIM), q_norm), cos, sin)
    k = apply_rotary(rms_norm(k.reshape(B, S, ATT_KV_HEADS, ATT_HEAD_DIM), k_norm), cos, sin)
    v = v.reshape(B, S, ATT_KV_HEADS, ATT_HEAD_DIM)
    qi = apply_rotary(qi.reshape(B, S, IDX_HEADS, IDX_HEAD_DIM), cos, sin)
    ki = apply_rotary(rms_norm(ki, idx_k_norm)[:, :, None, :], cos, sin)[:, :, 0, :]
    wi = wi * (IDX_HEADS ** -0.5 * IDX_HEAD_DIM ** -0.5)
    top_k = min(IDX_TOPK_MAX, S // 4)
    nb = S // DSA_QBLOCK
    q = q.reshape(B, S, ATT_KV_HEADS, ATT_GROUP, ATT_HEAD_DIM)

    def to_blocks(a):
        return jnp.moveaxis(a.reshape(B, nb, DSA_QBLOCK, *a.shape[2:]), 1, 0)

    key_pos = jnp.arange(S)

    def block(args):
        qb, qib, wib, t0 = args
        q_pos = t0 + jnp.arange(DSA_QBLOCK)
        s = jnp.einsum('bqhd,bsd->bqhs', qib.astype(jnp.float32), ki.astype(jnp.float32))
        score = jnp.einsum('bqh,bqhs->bqs', wib.astype(jnp.float32), jax.nn.relu(s))
        causal = key_pos[None, :] <= q_pos[:, None]
        score = jnp.where(causal[None], score, -jnp.inf)
        _, idx = lax.top_k(score, top_k)
        valid = idx <= q_pos[None, :, None]
        k_sel = jax.vmap(lambda kk, ii: kk[ii])(k, idx)
        v_sel = jax.vmap(lambda vv, ii: vv[ii])(v, idx)
        logits = jnp.einsum('bqngd,bqjnd->bqngj', qb, k_sel).astype(jnp.float32) * (ATT_HEAD_DIM ** -0.5)
        logits = jnp.where(valid[:, :, None, None, :], logits, -jnp.inf)
        probs = jax.nn.softmax(logits, axis=-1).astype(v.dtype)
        return jnp.einsum('bqngj,bqjnd->bqngd', probs, v_sel)

    t0s = jnp.arange(nb, dtype=jnp.int32) * DSA_QBLOCK
    o = lax.map(block, (to_blocks(q), to_blocks(qi), to_blocks(wi), t0s))
    o = jnp.moveaxis(o, 0, 1).reshape(B, S, ATT_HEADS * ATT_HEAD_DIM)
    return (o @ w_out).astype(h.dtype)


def retention(h, positions, w_in, w_out, out_norm):
    B, S, _ = h.shape
    q, k, v, g = split_cols(h @ w_in, B_SIZES)
    cos, sin = rotary_angles(positions, RET_QK_DIM, RET_THETA)
    q = apply_rotary(q.reshape(B, S, RET_HEADS, RET_QK_DIM), cos, sin)
    k = apply_rotary(k.reshape(B, S, RET_HEADS, RET_QK_DIM), cos, sin) * (RET_QK_DIM ** -0.5)
    v = v.reshape(B, S, RET_HEADS, RET_V_DIM)
    C = RET_CHUNK
    nc = S // C
    log_gamma = jnp.log1p(-jnp.exp2(-5.0 - jnp.arange(RET_HEADS, dtype=jnp.float32)))
    i = jnp.arange(C, dtype=jnp.float32)
    dist = i[:, None] - i[None, :]
    intra = jnp.where(dist[None] >= 0, jnp.exp(log_gamma[:, None, None] * jnp.maximum(dist, 0.0)[None]), 0.0)
    q_decay = jnp.exp(log_gamma[None, :] * (i[:, None] + 1.0))
    k_decay = jnp.exp(log_gamma[None, :] * (C - 1.0 - i)[:, None])
    chunk_decay = jnp.exp(log_gamma * C)

    def chunks(a):
        return jnp.moveaxis(a.reshape(B, nc, C, *a.shape[2:]), 1, 0)

    def step(state, inp):
        qc, kc, vc = (a.astype(jnp.float32) for a in inp)
        att = jnp.einsum('bihd,bjhd->bhij', qc, kc) * intra[None]
        inner = jnp.einsum('bhij,bjhe->bihe', att, vc)
        cross = jnp.einsum('bihd,bhde->bihe', qc * q_decay[None, :, :, None], state)
        new_state = state * chunk_decay[None, :, None, None] + jnp.einsum(
            'bjhd,bjhe->bhde', kc * k_decay[None, :, :, None], vc)
        return new_state, inner + cross

    state0 = jnp.zeros((B, RET_HEADS, RET_QK_DIM, RET_V_DIM), jnp.float32)
    _, o = lax.scan(step, state0, (chunks(q), chunks(k), chunks(v)))
    o = jnp.moveaxis(o, 0, 1).reshape(B, S, RET_HEADS, RET_V_DIM)
    o = rms_norm(o, out_norm).astype(h.dtype).reshape(B, S, RET_HEADS * RET_V_DIM)
    return ((jax.nn.silu(g) * o) @ w_out).astype(h.dtype)


def hier_moe(h, w_group, b_group, w_expert, b_expert, w_gate, w_up, w_down):
    B, S, D = h.shape
    T = B * S
    t = h.reshape(T, D)
    g_logits = (t @ w_group).astype(jnp.float32) + b_group.astype(jnp.float32)
    g_prob = jax.nn.softmax(g_logits, axis=-1)
    g_sel = jnp.argmax(g_logits, axis=-1)
    g_w = jnp.take_along_axis(g_prob, g_sel[:, None], axis=-1)
    e_logits = ((t @ w_expert).astype(jnp.float32) + b_expert.astype(jnp.float32)).reshape(T, N_GROUPS, EXPERTS_PER_GROUP)
    e_logits = jnp.take_along_axis(e_logits, g_sel[:, None, None], axis=1)[:, 0]
    top_w, top_i = lax.top_k(jax.nn.softmax(e_logits, axis=-1), EXPERT_TOPK)
    top_w = top_w / jnp.sum(top_w, axis=-1, keepdims=True)
    gate = jnp.sum(jax.nn.one_hot(top_i, EXPERTS_PER_GROUP, dtype=jnp.float32) * top_w[..., None], axis=1) * g_w
    y = jnp.zeros((T, D), jnp.float32)
    for gi in range(N_GROUPS):
        gate_g = jnp.where((g_sel == gi)[:, None], gate, 0.0).astype(t.dtype)
        a = jnp.einsum('td,edf->tef', t, w_gate[gi])
        u = jnp.einsum('td,edf->tef', t, w_up[gi])
        hid = jax.nn.silu(a) * u * gate_g[..., None]
        y = y + jnp.einsum('tef,efd->td', hid, w_down[gi]).astype(jnp.float32)
    return y.astype(h.dtype).reshape(B, S, D)


def setup_inputs(seed: int = 0) -> dict:
    key = jax.random.key(seed)
    ks = jax.random.split(key, 24)
    f32 = jnp.float32

    def w(k, shape, fan_in):
        return jax.random.normal(k, shape, f32) * (fan_in ** -0.5)

    def gain(k, shape):
        return 1.0 + 0.02 * jax.random.normal(k, shape, f32)

    x = jax.random.normal(ks[0], (BATCH, SEQ, D_MODEL), f32)
    p = jax.random.normal(ks[1], (DEPTH, BATCH, SEQ, PLE_DIM), f32)
    offsets = jax.random.randint(ks[2], (BATCH, 1), 0, 4096, dtype=jnp.int32)
    positions = offsets + jnp.arange(SEQ, dtype=jnp.int32)[None, :]
    G, E, F = N_GROUPS, EXPERTS_PER_GROUP, EXPERT_FF
    return {
        "x": x,
        "p": p,
        "positions": positions,
        "attn_norm": gain(ks[3], (DEPTH, D_MODEL)),
        "ffn_norm": gain(ks[4], (DEPTH, D_MODEL)),
        "ple_norm": gain(ks[5], (DEPTH, D_MODEL)),
        "a_w_in": w(ks[6], (N_A_LAYERS, D_MODEL, A_COLS), D_MODEL),
        "a_w_out": w(ks[7], (N_A_LAYERS, ATT_HEADS * ATT_HEAD_DIM, D_MODEL), ATT_HEADS * ATT_HEAD_DIM),
        "a_q_norm": gain(ks[8], (N_A_LAYERS, ATT_HEAD_DIM)),
        "a_k_norm": gain(ks[9], (N_A_LAYERS, ATT_HEAD_DIM)),
        "a_idx_k_norm": gain(ks[10], (N_A_LAYERS, IDX_HEAD_DIM)),
        "b_w_in": w(ks[11], (N_B_LAYERS, D_MODEL, B_COLS), D_MODEL),
        "b_w_out": w(ks[12], (N_B_LAYERS, RET_HEADS * RET_V_DIM, D_MODEL), RET_HEADS * RET_V_DIM),
        "b_out_norm": gain(ks[13], (N_B_LAYERS, RET_HEADS, RET_V_DIM)),
        "moe_w_group": w(ks[14], (DEPTH, D_MODEL, G), D_MODEL),
        "moe_b_group": 0.01 * jax.random.normal(ks[15], (DEPTH, G), f32),
        "moe_w_expert": w(ks[16], (DEPTH, D_MODEL, G * E), D_MODEL),
        "moe_b_expert": 0.01 * jax.random.normal(ks[17], (DEPTH, G * E), f32),
        "moe_w_gate": w(ks[18], (DEPTH, G, E, D_MODEL, F), D_MODEL),
        "moe_w_up": w(ks[19], (DEPTH, G, E, D_MODEL, F), D_MODEL),
        "moe_w_down": w(ks[20], (DEPTH, G, E, F, D_MODEL), F),
        "ple_proj": w(ks[21], (DEPTH, PLE_DIM, D_MODEL), PLE_DIM),
        "ple_gate": w(ks[22], (DEPTH, D_MODEL, D_MODEL), D_MODEL),
    }


def reference(x, p, positions, attn_norm, ffn_norm, ple_norm, a_w_in, a_w_out, a_q_norm, a_k_norm,
              a_idx_k_norm, b_w_in, b_w_out, b_out_norm, moe_w_group, moe_b_group, moe_w_expert,
              moe_b_expert, moe_w_gate, moe_w_up, moe_w_down, ple_proj, ple_gate):
    for i in range(DEPTH):
        hn = rms_norm(x, attn_norm[i])
        j = i // N_MIXERS
        if i % N_MIXERS == 0:
            mix = dsa_attention(hn, positions, a_w_in[j], a_w_out[j], a_q_norm[j], a_k_norm[j], a_idx_k_norm[j])
        else:
            mix = retention(hn, positions, b_w_in[j], b_w_out[j], b_out_norm[j])
        x = x + mix
        x = x + hier_moe(rms_norm(x, ffn_norm[i]), moe_w_group[i], moe_b_group[i], moe_w_expert[i],
                         moe_b_expert[i], moe_w_gate[i], moe_w_up[i], moe_w_down[i])
        ple_g = jax.nn.sigmoid(rms_norm(x, ple_norm[i]) @ ple_gate[i])
        x = x + ple_g * (p[i] @ ple_proj[i]).astype(x.dtype)
    return x
```

```python
import functools

import jax
import jax.numpy as jnp
from jax import lax
from jax.experimental import pallas as pl
from jax.experimental.pallas import tpu as pltpu

F32 = jnp.float32
BF16 = jnp.bfloat16
I32 = jnp.int32

RMS_EPS = 1e-6
HEAD_DIM = 128
IDX_HEADS = 64
IDX_TOPK_MAX = 256
ROPE_THETA = 500000.0
RET_QK_DIM = 256
RET_THETA = 10000.0
RET_CHUNK = 128
N_GROUPS = 8
EXPERTS_PER_GROUP = 8
LANES = 128
VMEM_LIMIT_BYTES = 56 * 1024 * 1024
NEG_BIG = -1e30
INT_MIN = -2147483648


def _cparams(sem):
    return pltpu.CompilerParams(dimension_semantics=sem, vmem_limit_bytes=VMEM_LIMIT_BYTES)


def _pick(n, cands):
    for c in cands:
        if n % c == 0:
            return c
    raise ValueError(f"no tile for {n} in {cands}")


def _rmsnorm_kernel(x_ref, g_ref, o_ref):
    x = x_ref[...]
    ms = jnp.mean(x * x, axis=-1, keepdims=True)
    o_ref[...] = (x * lax.rsqrt(ms + RMS_EPS) * g_ref[...]).astype(o_ref.dtype)


def rmsnorm_cast(x2d, gain):
    T, D = x2d.shape
    tm = _pick(T, (512, 256, 128, 8))
    return pl.pallas_call(
        _rmsnorm_kernel,
        grid=(T // tm,),
        in_specs=[pl.BlockSpec((tm, D), lambda i: (i, 0)), pl.BlockSpec((1, D), lambda i: (0, 0))],
        out_specs=pl.BlockSpec((tm, D), lambda i: (i, 0)),
        out_shape=jax.ShapeDtypeStruct((T, D), BF16),
        compiler_params=_cparams(("parallel",)),
        name="rmsnorm_cast",
    )(x2d, gain.reshape(1, D))


def _mm_kernel(*refs, nk, head_major, has_res, n_valid):
    a_ref, w_ref = refs[0], refs[1]
    r_ref = refs[2] if has_res else None
    o_ref = refs[2 + has_res]
    acc_ref = refs[3 + has_res] if nk > 1 else None
    if head_major:
        a = jnp.concatenate([a_ref[h] for h in range(a_ref.shape[0])], axis=1)
    else:
        a = a_ref[...]
    w = w_ref[...]
    if n_valid % w.shape[1]:
        col = pl.program_id(1) * w.shape[1] + lax.broadcasted_iota(I32, (1, w.shape[1]), 1)
        w = jnp.where(col < n_valid, w, 0.0)
    part = jnp.dot(a, w.astype(BF16), preferred_element_type=F32)

    def finish(acc):
        if has_res:
            acc = acc + r_ref[...]
        o_ref[...] = acc.astype(o_ref.dtype)

    if nk == 1:
        finish(part)
    else:
        k = pl.program_id(2)

        @pl.when(k == 0)
        def _():
            acc_ref[...] = part

        @pl.when(k > 0)
        def _():
            acc_ref[...] += part

        @pl.when(k == nk - 1)
        def _():
            finish(acc_ref[...])


def matmul(a, w, layer, *, n_out=None, res=None, out_dtype=F32, tm=1024, tn=512, tk=None, head_major=False,
           name="matmul"):
    if head_major:
        nh, M, hd = a.shape
        K = nh * hd
    else:
        M, K = a.shape
    N = w.shape[2]
    n_out = N if n_out is None else n_out
    tm = min(tm, M)
    tk = K if (tk is None or head_major) else min(tk, K)
    assert M % tm == 0 and K % tk == 0 and n_out % tn == 0
    nk = K // tk
    a_mode = dict(pipeline_mode=pl.Buffered(1)) if nk == 1 else {}
    if head_major:
        a_spec = pl.BlockSpec((nh, tm, hd), lambda i, j, k: (0, i, 0), **a_mode)
    else:
        a_spec = pl.BlockSpec((tm, tk), lambda i, j, k: (i, k), **a_mode)
    in_specs = [a_spec, pl.BlockSpec((None, tk, tn), lambda i, j, k: (layer, k, j))]
    args = [a, w]
    if res is not None:
        in_specs.append(pl.BlockSpec((tm, tn), lambda i, j, k: (i, j)))
        args.append(res)
    return pl.pallas_call(
        functools.partial(_mm_kernel, nk=nk, head_major=head_major, has_res=res is not None, n_valid=N),
        grid=(M // tm, n_out // tn, nk),
        in_specs=in_specs,
        out_specs=pl.BlockSpec((tm, tn), lambda i, j, k: (i, j)),
        out_shape=jax.ShapeDtypeStruct((M, n_out), out_dtype),
        scratch_shapes=[pltpu.VMEM((tm, tn), F32)] if nk > 1 else [],
        compiler_params=_cparams(("parallel", "parallel", "arbitrary")),
        name=name,
    )(*args)


def _rot_head(x, c, s1, s2):
    half = HEAD_DIM // 8
    return x * c + pltpu.roll(x, HEAD_DIM - half, 1) * s1 + pltpu.roll(x, half, 1) * s2


def _heads_kernel(x_ref, c_ref, s1_ref, s2_ref, g_ref, o_ref, *, norm, rot, scale):
    c, s1, s2 = c_ref[...], s1_ref[...], s2_ref[...]
    for h in range(o_ref.shape[0]):
        x = x_ref[:, h * HEAD_DIM:(h + 1) * HEAD_DIM]
        if norm:
            ms = jnp.mean(x * x, axis=-1, keepdims=True)
            x = x * lax.rsqrt(ms + RMS_EPS) * g_ref[...]
        if rot:
            x = _rot_head(x, c, s1, s2)
        if scale != 1.0:
            x = x * scale
        o_ref[h] = x.astype(o_ref.dtype)


def heads_prep(proj, col_off, n_heads, tabs, gain, *, norm, rot, scale=1.0, name):
    T = proj.shape[0]
    width = n_heads * HEAD_DIM
    bw = next(b for b in (1024, 512, 256, 128) if width % b == 0 and col_off % b == 0)
    hb = bw // HEAD_DIM
    tm = _pick(T, (256, 128, 8))
    c, s1, s2 = tabs
    row = lambda i, j: (i, 0)
    return pl.pallas_call(
        functools.partial(_heads_kernel, norm=norm, rot=rot, scale=scale),
        grid=(T // tm, width // bw),
        in_specs=[pl.BlockSpec((tm, bw), lambda i, j: (i, col_off // bw + j)),
                  pl.BlockSpec((tm, HEAD_DIM), row), pl.BlockSpec((tm, HEAD_DIM), row),
                  pl.BlockSpec((tm, HEAD_DIM), row), pl.BlockSpec((1, HEAD_DIM), lambda i, j: (0, 0))],
        out_specs=pl.BlockSpec((hb, tm, HEAD_DIM), lambda i, j: (j, i, 0)),
        out_shape=jax.ShapeDtypeStruct((n_heads, T, HEAD_DIM), BF16),
        compiler_params=_cparams(("parallel", "parallel")),
        name=name,
    )(proj, c, s1, s2, gain.reshape(1, HEAD_DIM))


def _kiwi_kernel(x_ref, c_ref, s1_ref, s2_ref, g_ref, ki_ref, wi_ref, *, wscale):
    x = x_ref[:, :HEAD_DIM]
    ms = jnp.mean(x * x, axis=-1, keepdims=True)
    x = x * lax.rsqrt(ms + RMS_EPS) * g_ref[...]
    ki_ref[...] = _rot_head(x, c_ref[...], s1_ref[...], s2_ref[...]).astype(ki_ref.dtype)
    wt = jnp.transpose(x_ref[:, HEAD_DIM:2 * HEAD_DIM])
    wi_ref[...] = wt[:IDX_HEADS, :] * wscale


def kiwi_prep(proj, col_off, tabs, gain, wscale):
    T = proj.shape[0]
    bw = 2 * HEAD_DIM
    assert col_off % bw == 0
    tm = _pick(T, (256, 128))
    c, s1, s2 = tabs
    row = lambda i: (i, 0)
    return pl.pallas_call(
        functools.partial(_kiwi_kernel, wscale=wscale),
        grid=(T // tm,),
        in_specs=[pl.BlockSpec((tm, bw), lambda i: (i, col_off // bw)),
                  pl.BlockSpec((tm, HEAD_DIM), row), pl.BlockSpec((tm, HEAD_DIM), row),
                  pl.BlockSpec((tm, HEAD_DIM), row), pl.BlockSpec((1, HEAD_DIM), lambda i: (0, 0))],
        out_specs=[pl.BlockSpec((tm, HEAD_DIM), row), pl.BlockSpec((IDX_HEADS, tm), lambda i: (0, i))],
        out_shape=[jax.ShapeDtypeStruct((T, HEAD_DIM), BF16), jax.ShapeDtypeStruct((IDX_HEADS, T), F32)],
        compiler_params=_cparams(("parallel",)),
        name="kiwi_prep",
    )(proj, c, s1, s2, gain.reshape(1, HEAD_DIM))


_NT = (((1,), (1,)), ((), ()))


def _dsa_kernel(qi_ref, ki_ref, wi_ref, q_ref, k_ref, v_ref, o_ref, sc_ref, u_ref, bias_ref, *, top_k, sub,
                steps_per_kv):
    S, tq = sc_ref.shape
    i = pl.program_id(1)

    sc_ref[...] = jnp.zeros_like(sc_ref)

    def idx_body(h, carry):
        s = lax.dot_general(ki_ref[...], qi_ref[h], _NT, preferred_element_type=F32)
        sc_ref[...] += jnp.maximum(s, 0.0) * wi_ref[pl.ds(h, 1), :]
        return carry

    lax.fori_loop(0, IDX_HEADS, idx_body, 0)

    key_pos = lax.broadcasted_iota(I32, (S, tq), 0)
    q_pos = i * tq + lax.broadcasted_iota(I32, (S, tq), 1)
    causal = key_pos <= q_pos
    bits = pltpu.bitcast(sc_ref[...], I32)
    u = bits ^ ((bits >> 31) & 0x7FFFFFFF)
    u_ref[...] = jnp.where(causal, u, INT_MIN)

    def count_ge(cand):
        return jnp.sum(jnp.where(u_ref[...] >= cand, 1.0, 0.0), axis=0, keepdims=True)

    zero = jnp.zeros((1, tq), I32)
    v0 = jnp.where(count_ge(zero) >= top_k, zero, jnp.full((1, tq), INT_MIN, I32))

    def bis_body(it, v):
        cand = v + lax.shift_left(jnp.int32(1), 30 - it)
        return jnp.where(count_ge(cand) >= top_k, cand, v)

    thr = lax.fori_loop(0, 31, bis_body, v0)
    keep = jnp.logical_and(u_ref[...] >= thr, causal)
    bias_ref[...] = jnp.transpose(jnp.where(keep, 0.0, NEG_BIG))

    def att_body(st, carry):
        n = st // steps_per_kv
        q = q_ref[pl.ds(st * sub, sub)].reshape(sub * tq, HEAD_DIM)
        lg = lax.dot_general(q, k_ref[n], _NT, preferred_element_type=F32)
        ps, ls = [], []
        for g in range(sub):
            x = lg[g * tq:(g + 1) * tq] + bias_ref[...]
            p = jnp.exp(x - jnp.max(x, axis=-1, keepdims=True))
            ls.append(jnp.sum(p, axis=-1, keepdims=True))
            ps.append(p.astype(BF16))
        o = jnp.dot(jnp.concatenate(ps, axis=0), v_ref[n], preferred_element_type=F32)
        for g in range(sub):
            o_ref[st * sub + g] = (o[g * tq:(g + 1) * tq] / ls[g]).astype(o_ref.dtype)
        return carry

    lax.fori_loop(0, q_ref.shape[0] // sub, att_body, 0)


def dsa_attention_core(qi, ki, wi, q, k, v, B, S, top_k):
    n_heads, T, _ = q.shape
    n_kv = k.shape[0]
    tq = _pick(S, (256, 128))
    nq = S // tq
    group = n_heads // n_kv
    sub = 2 if group % 2 == 0 else 1
    tok = lambda b, i: (0, b * nq + i, 0)
    return pl.pallas_call(
        functools.partial(_dsa_kernel, top_k=top_k, sub=sub, steps_per_kv=group // sub),
        grid=(B, nq),
        in_specs=[pl.BlockSpec((IDX_HEADS, tq, HEAD_DIM), tok),
                  pl.BlockSpec((S, HEAD_DIM), lambda b, i: (b, 0)),
                  pl.BlockSpec((IDX_HEADS, tq), lambda b, i: (0, b * nq + i)),
                  pl.BlockSpec((n_heads, tq, HEAD_DIM), tok),
                  pl.BlockSpec((n_kv, S, HEAD_DIM), lambda b, i: (0, b, 0)),
                  pl.BlockSpec((n_kv, S, HEAD_DIM), lambda b, i: (0, b, 0))],
        out_specs=pl.BlockSpec((n_heads, tq, HEAD_DIM), tok),
        out_shape=jax.ShapeDtypeStruct((n_heads, T, HEAD_DIM), BF16),
        scratch_shapes=[pltpu.VMEM((S, tq), F32), pltpu.VMEM((S, tq), I32), pltpu.VMEM((tq, S), F32)],
        compiler_params=_cparams(("parallel", "arbitrary")),
        name="dsa_core",
    )(qi, ki, wi, q, k, v)


def _route_kernel(x_ref, g_ref, w_ref, b_ref, o_ref):
    x = x_ref[...]
    ms = jnp.mean(x * x, axis=-1, keepdims=True)
    t = x * lax.rsqrt(ms + RMS_EPS) * g_ref[...]
    logits = jnp.dot(t, w_ref[...], precision=lax.Precision.HIGHEST, preferred_element_type=F32) + b_ref[...]
    lane = lax.broadcasted_iota(I32, logits.shape, 1)
    lane_f = lane.astype(F32)
    big = float(LANES)
    gl = jnp.where(lane < N_GROUPS, logits, -jnp.inf)
    gmax = jnp.max(gl, axis=-1, keepdims=True)
    g_sel = jnp.min(jnp.where(gl == gmax, lane_f, big), axis=-1, keepdims=True)
    g_w = 1.0 / jnp.sum(jnp.exp(gl - gmax), axis=-1, keepdims=True)
    lo = N_GROUPS + g_sel * EXPERTS_PER_GROUP
    in_grp = jnp.logical_and(lane_f >= lo, lane_f < lo + EXPERTS_PER_GROUP)
    el = jnp.where(in_grp, logits, -jnp.inf)
    m1 = jnp.max(el, axis=-1, keepdims=True)
    i1 = jnp.min(jnp.where(el == m1, lane_f, big), axis=-1, keepdims=True)
    el2 = jnp.where(lane_f == i1, -jnp.inf, el)
    m2 = jnp.max(el2, axis=-1, keepdims=True)
    i2 = jnp.min(jnp.where(el2 == m2, lane_f, big), axis=-1, keepdims=True)
    z = jnp.sum(jnp.exp(el - m1), axis=-1, keepdims=True)
    p1 = 1.0 / z
    p2 = jnp.exp(m2 - m1) / z
    w1 = p1 / (p1 + p2) * g_w
    w2 = p2 / (p1 + p2) * g_w
    out = jnp.where(lane == 0, i1 - N_GROUPS, jnp.where(lane == 1, i2 - N_GROUPS,
          jnp.where(lane == 2, w1, jnp.where(lane == 3, w2, 0.0))))
    o_ref[...] = out


def moe_route(x2d, gain, w_group, b_group, w_expert, b_expert):
    T, D = x2d.shape
    n_log = w_group.shape[1] + w_expert.shape[1]
    w = jnp.concatenate([w_group, w_expert, jnp.zeros((D, LANES - n_log), F32)], axis=1)
    b = jnp.concatenate([b_group, b_expert, jnp.zeros((LANES - n_log,), F32)]).reshape(1, LANES)
    tm = _pick(T, (256, 128, 8))
    return pl.pallas_call(
        _route_kernel,
        grid=(T // tm,),
        in_specs=[pl.BlockSpec((tm, D), lambda i: (i, 0)), pl.BlockSpec((1, D), lambda i: (0, 0)),
                  pl.BlockSpec((D, LANES), lambda i: (0, 0)), pl.BlockSpec((1, LANES), lambda i: (0, 0))],
        out_specs=pl.BlockSpec((tm, LANES), lambda i: (i, 0)),
        out_shape=jax.ShapeDtypeStruct((T, LANES), F32),
        compiler_params=_cparams(("parallel",)),
        name="moe_route",
    )(x2d, gain.reshape(1, D), w, b)


def _row_copy(src_hbm, dst_ref, sem, src_row, dst_row):
    return pltpu.make_async_copy(src_hbm.at[pl.ds(src_row, 1), :], dst_ref.at[pl.ds(dst_row, 1), :], sem)


def _gather_rows(idx_ref, src_hbm, dst_ref, sem, base, n_rows):
    def start(r, c):
        _row_copy(src_hbm, dst_ref, sem, idx_ref[base + r], r).start()
        return c

    def wait(r, c):
        _row_copy(src_hbm, dst_ref, sem, 0, r).wait()
        return c

    lax.fori_loop(0, n_rows, start, 0)
    lax.fori_loop(0, n_rows, wait, 0)


def _gather_kernel(idx_ref, src_hbm, o_ref, sem):
    R = o_ref.shape[0]
    _gather_rows(idx_ref, src_hbm, o_ref, sem, pl.program_id(0) * R, R)


def gather_rows(src, idx, rows_per_step):
    n = idx.shape[0]
    D = src.shape[1]
    R = rows_per_step
    assert n % R == 0
    return pl.pallas_call(
        _gather_kernel,
        grid_spec=pltpu.PrefetchScalarGridSpec(
            num_scalar_prefetch=1, grid=(n // R,),
            in_specs=[pl.BlockSpec(memory_space=pl.ANY)],
            out_specs=pl.BlockSpec((R, D), lambda i, idx: (i, 0)),
            scratch_shapes=[pltpu.SemaphoreType.DMA(())]),
        out_shape=jax.ShapeDtypeStruct((n, D), src.dtype),
        compiler_params=_cparams(("arbitrary",)),
        name="gather_rows",
    )(idx, src)


def _ffn_kernel(te_ref, nu_ref, xs_ref, gate_ref, g_ref, wg_ref, wu_ref, wd_ref, o_ref):
    @pl.when(pl.program_id(0) < nu_ref[0])
    def _():
        x = xs_ref[...]
        ms = jnp.mean(x * x, axis=-1, keepdims=True)
        t = (x * lax.rsqrt(ms + RMS_EPS) * g_ref[...]).astype(BF16)
        a = jnp.dot(t, wg_ref[0].astype(BF16), preferred_element_type=F32)
        u = jnp.dot(t, wu_ref[0].astype(BF16), preferred_element_type=F32)
        hid = a * jax.nn.sigmoid(a) * u * gate_ref[...]
        o_ref[...] = jnp.dot(hid.astype(BF16), wd_ref[0].astype(BF16), preferred_element_type=F32)

    @pl.when(pl.program_id(0) >= nu_ref[0])
    def _():
        o_ref[...] = jnp.zeros_like(o_ref)


def moe_ffn(xs, gate_s, gain, w_gate, w_up, w_down, tile_expert, n_used, tm):
    P, D = xs.shape
    F = w_gate.shape[-1]
    nt = P // tm
    wmap = lambda j, te, nu: (te[j], 0, 0)
    return pl.pallas_call(
        _ffn_kernel,
        grid_spec=pltpu.PrefetchScalarGridSpec(
            num_scalar_prefetch=2, grid=(nt,),
            in_specs=[pl.BlockSpec((tm, D), lambda j, te, nu: (j, 0)),
                      pl.BlockSpec((tm, 1), lambda j, te, nu: (j, 0)),
                      pl.BlockSpec((1, D), lambda j, te, nu: (0, 0)),
                      pl.BlockSpec((1, D, F), wmap), pl.BlockSpec((1, D, F), wmap),
                      pl.BlockSpec((1, F, D), wmap)],
            out_specs=pl.BlockSpec((tm, D), lambda j, te, nu: (j, 0))),
        out_shape=jax.ShapeDtypeStruct((P, D), F32),
        compiler_params=_cparams(("arbitrary",)),
        name="moe_ffn",
    )(tile_expert, n_used, xs, gate_s, gain.reshape(1, D), w_gate, w_up, w_down)


def _combine_kernel(p1_ref, p2_ref, ys_hbm, x_ref, o_ref, buf1, buf2, sem):
    R = x_ref.shape[0]
    base = pl.program_id(0) * R
    _gather_rows(p1_ref, ys_hbm, buf1, sem.at[0], base, R)
    _gather_rows(p2_ref, ys_hbm, buf2, sem.at[1], base, R)
    o_ref[...] = x_ref[...] + (buf1[...] + buf2[...])


def moe_combine(x2d, ys, pos1, pos2, rows_per_step):
    T, D = x2d.shape
    R = rows_per_step
    return pl.pallas_call(
        _combine_kernel,
        grid_spec=pltpu.PrefetchScalarGridSpec(
            num_scalar_prefetch=2, grid=(T // R,),
            in_specs=[pl.BlockSpec(memory_space=pl.ANY),
                      pl.BlockSpec((R, D), lambda i, a, b: (i, 0))],
            out_specs=pl.BlockSpec((R, D), lambda i, a, b: (i, 0)),
            scratch_shapes=[pltpu.VMEM((R, D), F32), pltpu.VMEM((R, D), F32), pltpu.SemaphoreType.DMA((2,))]),
        out_shape=jax.ShapeDtypeStruct((T, D), F32),
        compiler_params=_cparams(("arbitrary",)),
        name="moe_combine",
    )(pos1, pos2, ys, x2d)


def hier_moe_residual(x2d, gain, w_group, b_group, w_expert, b_expert, w_gate, w_up, w_down, layer):
    T, D = x2d.shape
    depth, G, E = w_gate.shape[0], w_gate.shape[1], w_gate.shape[2]
    n_exp = G * E
    F = w_gate.shape[-1]
    route = moe_route(x2d, gain, w_group, b_group, w_expert, b_expert)
    eid = route[:, 0:2].astype(I32).reshape(2 * T)
    wts = route[:, 2:4].reshape(2 * T)
    tm = 256 if 2 * T >= 4096 else 128
    nt = (2 * T) // tm + n_exp
    order = jnp.argsort(eid, stable=True).astype(I32)
    counts = jnp.sum(eid[:, None] == jnp.arange(n_exp, dtype=I32)[None, :], axis=0, dtype=I32)
    tiles = (counts + tm - 1) // tm
    tile_end = jnp.cumsum(tiles)
    tile_start = tile_end - tiles
    pair_start = jnp.cumsum(counts) - counts
    n_used = tile_end[-1:]
    j = jnp.arange(nt, dtype=I32)
    tile_expert = jnp.minimum(jnp.searchsorted(tile_end, j, side="right").astype(I32), n_exp - 1)
    last_expert = tile_expert[jnp.maximum(n_used[0] - 1, 0)]
    tile_expert = jnp.where(j < n_used[0], tile_expert, last_expert)
    slot = jnp.arange(nt * tm, dtype=I32)
    s_exp = jnp.repeat(tile_expert, tm)
    rank = slot - jnp.repeat(tile_start[tile_expert], tm) * tm
    valid = jnp.logical_and(rank < counts[s_exp], jnp.repeat(j < n_used[0], tm))
    src_pair = order[jnp.clip(pair_start[s_exp] + rank, 0, 2 * T - 1)]
    src_tok = jnp.where(valid, src_pair // 2, 0)
    gate_s = jnp.where(valid, wts[src_pair], 0.0).reshape(nt * tm, 1)
    sorted_rank = jnp.arange(2 * T, dtype=I32)
    e_sorted = eid[order]
    pos_sorted = tile_start[e_sorted] * tm + (sorted_rank - pair_start[e_sorted])
    pos = jnp.zeros((2 * T,), I32).at[order].set(pos_sorted).reshape(T, 2)

    xs = gather_rows(x2d, src_tok, tm)
    ys = moe_ffn(xs, gate_s, gain, w_gate.reshape(depth * n_exp, D, F), w_up.reshape(depth * n_exp, D, F),
                 w_down.reshape(depth * n_exp, F, D), tile_expert + layer * n_exp, n_used, tm)
    return moe_combine(x2d, ys, pos[:, 0], pos[:, 1], min(256, T))


def _ple_kernel(a_ref, w_ref, p_ref, pw_ref, x_ref, o_ref):
    z = jnp.dot(a_ref[...], w_ref[...].astype(BF16), preferred_element_type=F32)
    e = jnp.dot(p_ref[...].astype(BF16), pw_ref[...].astype(BF16), preferred_element_type=F32)
    o_ref[...] = x_ref[...] + jax.nn.sigmoid(z) * e


def ple_residual(x2d, gain, w_gate, p3d, w_proj, layer):
    T, D = x2d.shape
    P = p3d.shape[2]
    a = rmsnorm_cast(x2d, gain)
    tm, tn = min(1024, T), 512
    return pl.pallas_call(
        _ple_kernel,
        grid=(T // tm, D // tn),
        in_specs=[pl.BlockSpec((tm, D), lambda i, j: (i, 0), pipeline_mode=pl.Buffered(1)),
                  pl.BlockSpec((None, D, tn), lambda i, j: (layer, 0, j)),
                  pl.BlockSpec((None, tm, P), lambda i, j: (layer, i, 0)),
                  pl.BlockSpec((None, P, tn), lambda i, j: (layer, 0, j)),
                  pl.BlockSpec((tm, tn), lambda i, j: (i, j))],
        out_specs=pl.BlockSpec((tm, tn), lambda i, j: (i, j)),
        out_shape=jax.ShapeDtypeStruct((T, D), F32),
        compiler_params=_cparams(("parallel", "parallel")),
        name="ple",
    )(a, w_gate, p3d, w_proj, x2d)


def _retention_kernel(q_ref, k_ref, v_ref, g_ref, cos_ref, sin_ref, intra_ref, qd_ref, kd_ref, cd_ref, gn_ref,
                      o_ref, state_ref):
    S = q_ref.shape[0]
    C = intra_ref.shape[1]
    half = q_ref.shape[1] // 2
    state_ref[...] = jnp.zeros_like(state_ref)
    intra = intra_ref[0]
    qd, kd, cd, gn = qd_ref[0], kd_ref[0], cd_ref[0], gn_ref[0]

    def rot(x, cos, sin):
        x1, x2 = x[:, :half], x[:, half:]
        return jnp.concatenate([x1 * cos - x2 * sin, x1 * sin + x2 * cos], axis=1)

    def chunk(c, carry):
        rows = pl.ds(pl.multiple_of(c * C, C), C)
        cos, sin = cos_ref[rows, :], sin_ref[rows, :]
        qc = rot(q_ref[rows, :], cos, sin)
        kc = rot(k_ref[rows, :], cos, sin) * (RET_QK_DIM ** -0.5)
        vc = v_ref[rows, :].astype(BF16)
        att = lax.dot_general(qc.astype(BF16), kc.astype(BF16), _NT, preferred_element_type=F32) * intra
        inner = jnp.dot(att.astype(BF16), vc, preferred_element_type=F32)
        state = state_ref[...]
        cross = jnp.dot((qc * qd).astype(BF16), state.astype(BF16), preferred_element_type=F32)
        kt = jnp.transpose(kc * kd).astype(BF16)
        state_ref[...] = state * cd + jnp.dot(kt, vc, preferred_element_type=F32)
        o = inner + cross
        o = o * lax.rsqrt(jnp.mean(o * o, axis=-1, keepdims=True) + RMS_EPS) * gn
        g = g_ref[rows, :]
        o_ref[rows, :] = (g * jax.nn.sigmoid(g) * o).astype(o_ref.dtype)
        return carry

    lax.fori_loop(0, S // C, chunk, 0)


def retention_core(proj, cos, sin, out_norm, B, S, n_heads):
    T = proj.shape[0]
    dk = RET_QK_DIM
    dv = 2 * dk
    C = RET_CHUNK
    log_gamma = jnp.log1p(-jnp.exp2(-5.0 - jnp.arange(n_heads, dtype=F32)))
    i = jnp.arange(C, dtype=F32)
    dist = i[:, None] - i[None, :]
    intra = jnp.where(dist[None] >= 0, jnp.exp(log_gamma[:, None, None] * jnp.maximum(dist, 0.0)[None]), 0.0)
    q_decay = jnp.broadcast_to(jnp.exp(log_gamma[:, None] * (i[None, :] + 1.0))[:, :, None], (n_heads, C, dk))
    k_decay = jnp.broadcast_to(jnp.exp(log_gamma[:, None] * (C - 1.0 - i)[None, :])[:, :, None], (n_heads, C, dk))
    chunk_decay = jnp.broadcast_to(jnp.exp(log_gamma * C)[:, None, None], (n_heads, 1, dv))
    nq = n_heads
    return pl.pallas_call(
        _retention_kernel,
        grid=(B, n_heads),
        in_specs=[pl.BlockSpec((S, dk), lambda b, h: (b, h)),
                  pl.BlockSpec((S, dk), lambda b, h: (b, nq + h)),
                  pl.BlockSpec((S, dv), lambda b, h: (b, nq + h)),
                  pl.BlockSpec((S, dv), lambda b, h: (b, 2 * nq + h)),
                  pl.BlockSpec((S, dk // 2), lambda b, h: (b, 0)),
                  pl.BlockSpec((S, dk // 2), lambda b, h: (b, 0)),
                  pl.BlockSpec((1, C, C), lambda b, h: (h, 0, 0)),
                  pl.BlockSpec((1, C, dk), lambda b, h: (h, 0, 0)),
                  pl.BlockSpec((1, C, dk), lambda b, h: (h, 0, 0)),
                  pl.BlockSpec((1, 1, dv), lambda b, h: (h, 0, 0)),
                  pl.BlockSpec((1, 1, dv), lambda b, h: (h, 0, 0))],
        out_specs=pl.BlockSpec((S, dv), lambda b, h: (b, h)),
        out_shape=jax.ShapeDtypeStruct((T, n_heads * dv), BF16),
        scratch_shapes=[pltpu.VMEM((dk, dv), F32)],
        compiler_params=_cparams(("parallel", "arbitrary")),
        name="retention_core",
    )(proj, proj, proj, proj, cos, sin, intra, q_decay, k_decay, chunk_decay,
      out_norm.reshape(n_heads, 1, dv))


def _rot_tables(positions, dim, theta):
    inv = theta ** (-jnp.arange(0, dim, 2, dtype=F32) / dim)
    ang = positions.astype(F32).reshape(-1, 1) * inv[None, :]
    return jnp.cos(ang), jnp.sin(ang)


def dsa_layer(x2d, positions, gain, w_in, w_out, q_norm, k_norm, idx_k_norm, layer, B, S):
    T, D = x2d.shape
    n_heads = D // HEAD_DIM
    n_kv = n_heads // 4
    rot_dim = HEAD_DIM // 4
    cos, sin = _rot_tables(positions, rot_dim, ROPE_THETA)
    ones = jnp.ones((T, HEAD_DIM - rot_dim), F32)
    zeros_r = jnp.zeros((T, HEAD_DIM - rot_dim // 2), F32)
    tabs = (jnp.concatenate([cos, cos, ones], axis=1),
            jnp.concatenate([-sin, zeros_r], axis=1),
            jnp.concatenate([jnp.zeros_like(sin), sin, jnp.zeros((T, HEAD_DIM - rot_dim), F32)], axis=1))
    hn = rmsnorm_cast(x2d, gain)
    tn = 512
    n_cols = w_in.shape[2]
    proj = matmul(hn, w_in, layer, n_out=-(-n_cols // tn) * tn, tm=2048, tn=tn, name="dsa_in_proj")
    off_k = n_heads * HEAD_DIM
    off_v = off_k + n_kv * HEAD_DIM
    off_qi = off_v + n_kv * HEAD_DIM
    off_ki = off_qi + IDX_HEADS * HEAD_DIM
    q = heads_prep(proj, 0, n_heads, tabs, q_norm, norm=True, rot=True, scale=HEAD_DIM ** -0.5, name="prep_q")
    k = heads_prep(proj, off_k, n_kv, tabs, k_norm, norm=True, rot=True, name="prep_k")
    v = heads_prep(proj, off_v, n_kv, tabs, k_norm, norm=False, rot=False, name="prep_v")
    qi = heads_prep(proj, off_qi, IDX_HEADS, tabs, k_norm, norm=False, rot=True, name="prep_qi")
    ki, wi = kiwi_prep(proj, off_ki, tabs, idx_k_norm, IDX_HEADS ** -0.5 * HEAD_DIM ** -0.5)
    top_k = min(IDX_TOPK_MAX, S // 4)
    o = dsa_attention_core(qi, ki, wi, q, k, v, B, S, top_k)
    return matmul(o, w_out, layer, res=x2d, head_major=True, tm=2048, tn=512, name="dsa_out_proj")


def retention_layer(x2d, positions, gain, w_in, w_out, out_norm, layer, B, S):
    T, D = x2d.shape
    n_heads = D // RET_QK_DIM
    cos, sin = _rot_tables(positions, RET_QK_DIM, RET_THETA)
    hn = rmsnorm_cast(x2d, gain)
    proj = matmul(hn, w_in, layer, tm=2048, tn=512, name="ret_in_proj")
    og = retention_core(proj, cos, sin, out_norm, B, S, n_heads)
    return matmul(og, w_out, layer, res=x2d, tm=1024, tn=256, name="ret_out_proj")


def kernel(x, p, positions, attn_norm, ffn_norm, ple_norm, a_w_in, a_w_out, a_q_norm, a_k_norm, a_idx_k_norm,
           b_w_in, b_w_out, b_out_norm, moe_w_group, moe_b_group, moe_w_expert, moe_b_expert, moe_w_gate,
           moe_w_up, moe_w_down, ple_proj, ple_gate):
    B, S, D = x.shape
    depth = p.shape[0]
    T = B * S
    x2d = x.reshape(T, D)
    p3d = p.reshape(depth, T, p.shape[-1])
    for i in range(depth):
        j = i // 2
        if i % 2 == 0:
            x2d = dsa_layer(x2d, positions, attn_norm[i], a_w_in, a_w_out, a_q_norm[j], a_k_norm[j],
                            a_idx_k_norm[j], j, B, S)
        else:
            x2d = retention_layer(x2d, positions, attn_norm[i], b_w_in, b_w_out, b_out_norm[j], j, B, S)
        x2d = hier_moe_residual(x2d, ffn_norm[i], moe_w_group[i], moe_b_group[i], moe_w_expert[i],
                                moe_b_expert[i], moe_w_gate, moe_w_up, moe_w_down, i)
        x2d = ple_residual(x2d, ple_norm[i], ple_gate, p3d, ple_proj, i)
    return x2d.reshape(B, S, D)
```

```python
import functools

import jax
import jax.numpy as jnp
from jax import lax
from jax.experimental import pallas as pl
from jax.experimental.pallas import tpu as pltpu

F32 = jnp.float32
BF16 = jnp.bfloat16
I32 = jnp.int32

RMS_EPS = 1e-6
HEAD_DIM = 128
IDX_HEADS = 64
IDX_TOPK_MAX = 256
ROPE_THETA = 500000.0
RET_QK_DIM = 256
RET_THETA = 10000.0
RET_CHUNK = 128
N_GROUPS = 8
EXPERTS_PER_GROUP = 8
LANES = 128
VMEM_LIMIT_BYTES = 56 * 1024 * 1024
NEG_BIG = -1e30
INT_MIN = -2147483648


def _cparams(sem):
    return pltpu.CompilerParams(dimension_semantics=sem, vmem_limit_bytes=VMEM_LIMIT_BYTES)


def _pick(n, cands):
    for c in cands:
        if n % c == 0:
            return c
    raise ValueError(f"no tile for {n} in {cands}")


def _rmsnorm_kernel(x_ref, g_ref, o_ref):
    x = x_ref[...]
    ms = jnp.mean(x * x, axis=-1, keepdims=True)
    o_ref[...] = (x * lax.rsqrt(ms + RMS_EPS) * g_ref[...]).astype(o_ref.dtype)


def rmsnorm_cast(x2d, gain):
    T, D = x2d.shape
    tm = _pick(T, (512, 256, 128, 8))
    return pl.pallas_call(
        _rmsnorm_kernel,
        grid=(T // tm,),
        in_specs=[pl.BlockSpec((tm, D), lambda i: (i, 0)), pl.BlockSpec((1, D), lambda i: (0, 0))],
        out_specs=pl.BlockSpec((tm, D), lambda i: (i, 0)),
        out_shape=jax.ShapeDtypeStruct((T, D), BF16),
        compiler_params=_cparams(("parallel",)),
        name="rmsnorm_cast",
    )(x2d, gain.reshape(1, D))


def _mm_kernel(*refs, nk, head_major, has_res, n_valid):
    a_ref, w_ref = refs[0], refs[1]
    r_ref = refs[2] if has_res else None
    o_ref = refs[2 + has_res]
    acc_ref = refs[3 + has_res] if nk > 1 else None
    if head_major:
        a = jnp.concatenate([a_ref[h] for h in range(a_ref.shape[0])], axis=1)
    else:
        a = a_ref[...]
    w = w_ref[...]
    if n_valid % w.shape[1]:
        col = pl.program_id(1) * w.shape[1] + lax.broadcasted_iota(I32, (1, w.shape[1]), 1)
        w = jnp.where(col < n_valid, w, 0.0)
    part = jnp.dot(a, w.astype(BF16), preferred_element_type=F32)

    def finish(acc):
        if has_res:
            acc = acc + r_ref[...]
        o_ref[...] = acc.astype(o_ref.dtype)

    if nk == 1:
        finish(part)
    else:
        k = pl.program_id(2)

        @pl.when(k == 0)
        def _():
            acc_ref[...] = part

        @pl.when(k > 0)
        def _():
            acc_ref[...] += part

        @pl.when(k == nk - 1)
        def _():
            finish(acc_ref[...])


def matmul(a, w, layer, *, n_out=None, res=None, out_dtype=F32, tm=1024, tn=512, tk=None, head_major=False,
           name="matmul"):
    if head_major:
        nh, M, hd = a.shape
        K = nh * hd
    else:
        M, K = a.shape
    N = w.shape[2]
    n_out = N if n_out is None else n_out
    tm = min(tm, M)
    tk = K if (tk is None or head_major) else min(tk, K)
    assert M % tm == 0 and K % tk == 0 and n_out % tn == 0
    nk = K // tk
    a_mode = dict(pipeline_mode=pl.Buffered(1)) if nk == 1 else {}
    if head_major:
        a_spec = pl.BlockSpec((nh, tm, hd), lambda i, j, k: (0, i, 0), **a_mode)
    else:
        a_spec = pl.BlockSpec((tm, tk), lambda i, j, k: (i, k), **a_mode)
    in_specs = [a_spec, pl.BlockSpec((None, tk, tn), lambda i, j, k: (layer, k, j))]
    args = [a, w]
    if res is not None:
        in_specs.append(pl.BlockSpec((tm, tn), lambda i, j, k: (i, j)))
        args.append(res)
    return pl.pallas_call(
        functools.partial(_mm_kernel, nk=nk, head_major=head_major, has_res=res is not None, n_valid=N),
        grid=(M // tm, n_out // tn, nk),
        in_specs=in_specs,
        out_specs=pl.BlockSpec((tm, tn), lambda i, j, k: (i, j)),
        out_shape=jax.ShapeDtypeStruct((M, n_out), out_dtype),
        scratch_shapes=[pltpu.VMEM((tm, tn), F32)] if nk > 1 else [],
        compiler_params=_cparams(("parallel", "parallel", "arbitrary")),
        name=name,
    )(*args)


def _rot_head(x, c, s1, s2):
    half = HEAD_DIM // 8
    return x * c + pltpu.roll(x, HEAD_DIM - half, 1) * s1 + pltpu.roll(x, half, 1) * s2


def _heads_kernel(x_ref, c_ref, s1_ref, s2_ref, g_ref, o_ref, *, norm, rot, scale):
    c, s1, s2 = c_ref[...], s1_ref[...], s2_ref[...]
    for h in range(o_ref.shape[0]):
        x = x_ref[:, h * HEAD_DIM:(h + 1) * HEAD_DIM]
        if norm:
            ms = jnp.mean(x * x, axis=-1, keepdims=True)
            x = x * lax.rsqrt(ms + RMS_EPS) * g_ref[...]
        if rot:
            x = _rot_head(x, c, s1, s2)
        if scale != 1.0:
            x = x * scale
        o_ref[h] = x.astype(o_ref.dtype)


def heads_prep(proj, col_off, n_heads, tabs, gain, *, norm, rot, scale=1.0, name):
    T = proj.shape[0]
    width = n_heads * HEAD_DIM
    bw = next(b for b in (1024, 512, 256, 128) if width % b == 0 and col_off % b == 0)
    hb = bw // HEAD_DIM
    tm = _pick(T, (256, 128, 8))
    c, s1, s2 = tabs
    row = lambda i, j: (i, 0)
    return pl.pallas_call(
        functools.partial(_heads_kernel, norm=norm, rot=rot, scale=scale),
        grid=(T // tm, width // bw),
        in_specs=[pl.BlockSpec((tm, bw), lambda i, j: (i, col_off // bw + j)),
                  pl.BlockSpec((tm, HEAD_DIM), row), pl.BlockSpec((tm, HEAD_DIM), row),
                  pl.BlockSpec((tm, HEAD_DIM), row), pl.BlockSpec((1, HEAD_DIM), lambda i, j: (0, 0))],
        out_specs=pl.BlockSpec((hb, tm, HEAD_DIM), lambda i, j: (j, i, 0)),
        out_shape=jax.ShapeDtypeStruct((n_heads, T, HEAD_DIM), BF16),
        compiler_params=_cparams(("parallel", "parallel")),
        name=name,
    )(proj, c, s1, s2, gain.reshape(1, HEAD_DIM))


def _kiwi_kernel(x_ref, c_ref, s1_ref, s2_ref, g_ref, ki_ref, wi_ref, *, wscale):
    x = x_ref[:, :HEAD_DIM]
    ms = jnp.mean(x * x, axis=-1, keepdims=True)
    x = x * lax.rsqrt(ms + RMS_EPS) * g_ref[...]
    ki_ref[...] = _rot_head(x, c_ref[...], s1_ref[...], s2_ref[...]).astype(ki_ref.dtype)
    wt = jnp.transpose(x_ref[:, HEAD_DIM:2 * HEAD_DIM])
    wi_ref[...] = wt[:IDX_HEADS, :] * wscale


def kiwi_prep(proj, col_off, tabs, gain, wscale):
    T = proj.shape[0]
    bw = 2 * HEAD_DIM
    assert col_off % bw == 0
    tm = _pick(T, (256, 128))
    c, s1, s2 = tabs
    row = lambda i: (i, 0)
    return pl.pallas_call(
        functools.partial(_kiwi_kernel, wscale=wscale),
        grid=(T // tm,),
        in_specs=[pl.BlockSpec((tm, bw), lambda i: (i, col_off // bw)),
                  pl.BlockSpec((tm, HEAD_DIM), row), pl.BlockSpec((tm, HEAD_DIM), row),
                  pl.BlockSpec((tm, HEAD_DIM), row), pl.BlockSpec((1, HEAD_DIM), lambda i: (0, 0))],
        out_specs=[pl.BlockSpec((tm, HEAD_DIM), row), pl.BlockSpec((IDX_HEADS, tm), lambda i: (0, i))],
        out_shape=[jax.ShapeDtypeStruct((T, HEAD_DIM), BF16), jax.ShapeDtypeStruct((IDX_HEADS, T), F32)],
        compiler_params=_cparams(("parallel",)),
        name="kiwi_prep",
    )(proj, c, s1, s2, gain.reshape(1, HEAD_DIM))


_NT = (((1,), (1,)), ((), ()))


IDX_UNROLL = 4
LOG2_E = 1.4426950408889634


def _dsa_tile(qi_ref, ki_ref, wi_ref, q_ref, k_ref, v_ref, o_ref, sc_ref, u_ref, bias_ref,
              *, top_k, sub, steps_per_kv, nk, q0):
    tq = sc_ref.shape[1]
    keys = pl.ds(0, nk)

    def idx_term(h):
        s = lax.dot_general(ki_ref[keys, :], qi_ref[h], _NT, preferred_element_type=F32)
        return jnp.maximum(s, 0.0) * wi_ref[pl.ds(h, 1), :]

    def idx_terms(h0):
        acc = idx_term(h0)
        for d in range(1, IDX_UNROLL):
            acc = acc + idx_term(h0 + d)
        return acc

    sc_ref[keys, :] = idx_terms(0)

    def idx_body(g, carry):
        sc_ref[keys, :] += idx_terms(g * IDX_UNROLL)
        return carry

    lax.fori_loop(1, IDX_HEADS // IDX_UNROLL, idx_body, 0)

    key_pos = lax.broadcasted_iota(I32, (nk, tq), 0)
    q_pos = q0 + lax.broadcasted_iota(I32, (nk, tq), 1)
    causal = key_pos <= q_pos
    bits = pltpu.bitcast(sc_ref[keys, :], I32)
    u = bits ^ ((bits >> 31) & 0x7FFFFFFF)
    u_ref[keys, :] = jnp.where(causal, u, INT_MIN)

    def count_ge(cand):
        return jnp.sum(jnp.where(u_ref[keys, :] >= cand, 1.0, 0.0), axis=0, keepdims=True)

    zero = jnp.zeros((1, tq), I32)
    v0 = jnp.where(count_ge(zero) >= top_k, zero, jnp.full((1, tq), INT_MIN, I32))

    def bis_body(it, v):
        cand = v + lax.shift_left(jnp.int32(1), 30 - it)
        return jnp.where(count_ge(cand) >= top_k, cand, v)

    thr = lax.fori_loop(0, 31, bis_body, v0)
    keep = jnp.logical_and(u_ref[keys, :] >= thr, causal)
    bias_ref[:, keys] = jnp.transpose(jnp.where(keep, 0.0, NEG_BIG))

    def att_body(st, carry):
        n = st // steps_per_kv
        q = q_ref[pl.ds(st * sub, sub)].reshape(sub * tq, HEAD_DIM)
        lg = lax.dot_general(q, k_ref[n, keys, :], _NT, preferred_element_type=F32)
        ps, ls = [], []
        for g in range(sub):
            x = lg[g * tq:(g + 1) * tq] + bias_ref[:, keys]
            p = jnp.exp2(x - jnp.max(x, axis=-1, keepdims=True))
            ls.append(jnp.sum(p, axis=-1, keepdims=True))
            ps.append(p.astype(BF16))
        o = jnp.dot(jnp.concatenate(ps, axis=0), v_ref[n, keys, :], preferred_element_type=F32)
        for g in range(sub):
            o_ref[st * sub + g] = (o[g * tq:(g + 1) * tq] / ls[g]).astype(o_ref.dtype)
        return carry

    lax.fori_loop(0, q_ref.shape[0] // sub, att_body, 0)


def _dsa_kernel(*refs, **static):
    S, tq = refs[7].shape
    i = pl.program_id(1)
    for c in range(S // tq):
        @pl.when(i == c)
        def _(c=c):
            _dsa_tile(*refs, nk=(c + 1) * tq, q0=c * tq, **static)


def dsa_attention_core(qi, ki, wi, q, k, v, B, S, top_k):
    n_heads, T, _ = q.shape
    n_kv = k.shape[0]
    tq = _pick(S, (256, 128))
    nq = S // tq
    group = n_heads // n_kv
    sub = 2 if group % 2 == 0 else 1
    tok = lambda b, i: (0, b * nq + i, 0)
    once = dict(pipeline_mode=pl.Buffered(1))
    return pl.pallas_call(
        functools.partial(_dsa_kernel, top_k=top_k, sub=sub, steps_per_kv=group // sub),
        grid=(B, nq),
        in_specs=[pl.BlockSpec((IDX_HEADS, tq, HEAD_DIM), tok),
                  pl.BlockSpec((S, HEAD_DIM), lambda b, i: (b, 0), **once),
                  pl.BlockSpec((IDX_HEADS, tq), lambda b, i: (0, b * nq + i)),
                  pl.BlockSpec((n_heads, tq, HEAD_DIM), tok),
                  pl.BlockSpec((n_kv, S, HEAD_DIM), lambda b, i: (0, b, 0), **once),
                  pl.BlockSpec((n_kv, S, HEAD_DIM), lambda b, i: (0, b, 0), **once)],
        out_specs=pl.BlockSpec((n_heads, tq, HEAD_DIM), tok),
        out_shape=jax.ShapeDtypeStruct((n_heads, T, HEAD_DIM), BF16),
        scratch_shapes=[pltpu.VMEM((S, tq), F32), pltpu.VMEM((S, tq), I32), pltpu.VMEM((tq, S), F32)],
        compiler_params=_cparams(("parallel", "arbitrary")),
        name="dsa_core",
    )(qi, ki, wi, q, k, v)


def _route_kernel(x_ref, g_ref, w_ref, b_ref, o_ref):
    x = x_ref[...]
    ms = jnp.mean(x * x, axis=-1, keepdims=True)
    t = x * lax.rsqrt(ms + RMS_EPS) * g_ref[...]
    logits = jnp.dot(t, w_ref[...], precision=lax.Precision.HIGHEST, preferred_element_type=F32) + b_ref[...]
    lane = lax.broadcasted_iota(I32, logits.shape, 1)
    lane_f = lane.astype(F32)
    big = float(LANES)
    gl = jnp.where(lane < N_GROUPS, logits, -jnp.inf)
    gmax = jnp.max(gl, axis=-1, keepdims=True)
    g_sel = jnp.min(jnp.where(gl == gmax, lane_f, big), axis=-1, keepdims=True)
    g_w = 1.0 / jnp.sum(jnp.exp(gl - gmax), axis=-1, keepdims=True)
    lo = N_GROUPS + g_sel * EXPERTS_PER_GROUP
    in_grp = jnp.logical_and(lane_f >= lo, lane_f < lo + EXPERTS_PER_GROUP)
    el = jnp.where(in_grp, logits, -jnp.inf)
    m1 = jnp.max(el, axis=-1, keepdims=True)
    i1 = jnp.min(jnp.where(el == m1, lane_f, big), axis=-1, keepdims=True)
    el2 = jnp.where(lane_f == i1, -jnp.inf, el)
    m2 = jnp.max(el2, axis=-1, keepdims=True)
    i2 = jnp.min(jnp.where(el2 == m2, lane_f, big), axis=-1, keepdims=True)
    z = jnp.sum(jnp.exp(el - m1), axis=-1, keepdims=True)
    p1 = 1.0 / z
    p2 = jnp.exp(m2 - m1) / z
    w1 = p1 / (p1 + p2) * g_w
    w2 = p2 / (p1 + p2) * g_w
    out = jnp.where(lane == 0, i1 - N_GROUPS, jnp.where(lane == 1, i2 - N_GROUPS,
          jnp.where(lane == 2, w1, jnp.where(lane == 3, w2, 0.0))))
    o_ref[...] = out


def moe_route(x2d, gain, w_group, b_group, w_expert, b_expert):
    T, D = x2d.shape
    n_log = w_group.shape[1] + w_expert.shape[1]
    w = jnp.concatenate([w_group, w_expert, jnp.zeros((D, LANES - n_log), F32)], axis=1)
    b = jnp.concatenate([b_group, b_expert, jnp.zeros((LANES - n_log,), F32)]).reshape(1, LANES)
    tm = _pick(T, (256, 128, 8))
    return pl.pallas_call(
        _route_kernel,
        grid=(T // tm,),
        in_specs=[pl.BlockSpec((tm, D), lambda i: (i, 0)), pl.BlockSpec((1, D), lambda i: (0, 0)),
                  pl.BlockSpec((D, LANES), lambda i: (0, 0)), pl.BlockSpec((1, LANES), lambda i: (0, 0))],
        out_specs=pl.BlockSpec((tm, LANES), lambda i: (i, 0)),
        out_shape=jax.ShapeDtypeStruct((T, LANES), F32),
        compiler_params=_cparams(("parallel",)),
        name="moe_route",
    )(x2d, gain.reshape(1, D), w, b)


def _row_copy(src_hbm, dst_ref, sem, src_row, dst_row):
    return pltpu.make_async_copy(src_hbm.at[pl.ds(src_row, 1), :], dst_ref.at[pl.ds(dst_row, 1), :], sem)


DMA_PRIORITIES = 2


def _gather_rows(idx_ref, src_hbm, dst_ref, sem, base, n_rows):
    def start(g, c):
        for pr in range(DMA_PRIORITIES):
            r = g * DMA_PRIORITIES + pr
            _row_copy(src_hbm, dst_ref, sem, idx_ref[base + r], r).start(priority=pr)
        return c

    def wait(r, c):
        _row_copy(src_hbm, dst_ref, sem, 0, r).wait()
        return c

    lax.fori_loop(0, n_rows // DMA_PRIORITIES, start, 0)
    lax.fori_loop(0, n_rows, wait, 0)


def _gather_kernel(idx_ref, nu_ref, src_hbm, o_ref, sem):
    R = o_ref.shape[0]
    used = pl.program_id(0) < nu_ref[0]

    @pl.when(used)
    def _():
        _gather_rows(idx_ref, src_hbm, o_ref, sem, pl.program_id(0) * R, R)

    @pl.when(jnp.logical_not(used))
    def _():
        o_ref[...] = jnp.zeros_like(o_ref)


def gather_rows(src, idx, n_used, rows_per_step):
    n = idx.shape[0]
    D = src.shape[1]
    R = rows_per_step
    assert n % R == 0 and R % DMA_PRIORITIES == 0
    return pl.pallas_call(
        _gather_kernel,
        grid_spec=pltpu.PrefetchScalarGridSpec(
            num_scalar_prefetch=2, grid=(n // R,),
            in_specs=[pl.BlockSpec(memory_space=pl.ANY)],
            out_specs=pl.BlockSpec((R, D), lambda i, idx, nu: (i, 0)),
            scratch_shapes=[pltpu.SemaphoreType.DMA(())]),
        out_shape=jax.ShapeDtypeStruct((n, D), src.dtype),
        compiler_params=_cparams(("arbitrary",)),
        name="gather_rows",
    )(idx, n_used, src)


def _ffn_kernel(te_ref, nu_ref, xs_ref, gate_ref, g_ref, wg_ref, wu_ref, wd_ref, o_ref):
    @pl.when(pl.program_id(0) < nu_ref[0])
    def _():
        x = xs_ref[...]
        ms = jnp.mean(x * x, axis=-1, keepdims=True)
        t = (x * lax.rsqrt(ms + RMS_EPS) * g_ref[...]).astype(BF16)
        a = jnp.dot(t, wg_ref[0].astype(BF16), preferred_element_type=F32)
        u = jnp.dot(t, wu_ref[0].astype(BF16), preferred_element_type=F32)
        hid = a * jax.nn.sigmoid(a) * u * gate_ref[...]
        o_ref[...] = jnp.dot(hid.astype(BF16), wd_ref[0].astype(BF16), preferred_element_type=F32)

    @pl.when(pl.program_id(0) >= nu_ref[0])
    def _():
        o_ref[...] = jnp.zeros_like(o_ref)


def moe_ffn(xs, gate_s, gain, w_gate, w_up, w_down, tile_expert, n_used, tm):
    P, D = xs.shape
    F = w_gate.shape[-1]
    nt = P // tm
    wmap = lambda j, te, nu: (te[j], 0, 0)
    return pl.pallas_call(
        _ffn_kernel,
        grid_spec=pltpu.PrefetchScalarGridSpec(
            num_scalar_prefetch=2, grid=(nt,),
            in_specs=[pl.BlockSpec((tm, D), lambda j, te, nu: (j, 0)),
                      pl.BlockSpec((tm, 1), lambda j, te, nu: (j, 0)),
                      pl.BlockSpec((1, D), lambda j, te, nu: (0, 0)),
                      pl.BlockSpec((1, D, F), wmap), pl.BlockSpec((1, D, F), wmap),
                      pl.BlockSpec((1, F, D), wmap)],
            out_specs=pl.BlockSpec((tm, D), lambda j, te, nu: (j, 0))),
        out_shape=jax.ShapeDtypeStruct((P, D), F32),
        compiler_params=_cparams(("arbitrary",)),
        name="moe_ffn",
    )(tile_expert, n_used, xs, gate_s, gain.reshape(1, D), w_gate, w_up, w_down)


def _combine_kernel(p1_ref, p2_ref, ys_hbm, x_ref, o_ref, buf1, buf2, sem):
    R = x_ref.shape[0]
    base = pl.program_id(0) * R
    _gather_rows(p1_ref, ys_hbm, buf1, sem.at[0], base, R)
    _gather_rows(p2_ref, ys_hbm, buf2, sem.at[1], base, R)
    o_ref[...] = x_ref[...] + (buf1[...] + buf2[...])


def moe_combine(x2d, ys, pos1, pos2, rows_per_step):
    T, D = x2d.shape
    R = rows_per_step
    return pl.pallas_call(
        _combine_kernel,
        grid_spec=pltpu.PrefetchScalarGridSpec(
            num_scalar_prefetch=2, grid=(T // R,),
            in_specs=[pl.BlockSpec(memory_space=pl.ANY),
                      pl.BlockSpec((R, D), lambda i, a, b: (i, 0))],
            out_specs=pl.BlockSpec((R, D), lambda i, a, b: (i, 0)),
            scratch_shapes=[pltpu.VMEM((R, D), F32), pltpu.VMEM((R, D), F32), pltpu.SemaphoreType.DMA((2,))]),
        out_shape=jax.ShapeDtypeStruct((T, D), F32),
        compiler_params=_cparams(("arbitrary",)),
        name="moe_combine",
    )(pos1, pos2, ys, x2d)


def hier_moe_residual(x2d, gain, w_group, b_group, w_expert, b_expert, w_gate, w_up, w_down, layer):
    T, D = x2d.shape
    depth, G, E = w_gate.shape[0], w_gate.shape[1], w_gate.shape[2]
    n_exp = G * E
    F = w_gate.shape[-1]
    route = moe_route(x2d, gain, w_group, b_group, w_expert, b_expert)
    eid = route[:, 0:2].astype(I32).reshape(2 * T)
    wts = route[:, 2:4].reshape(2 * T)
    tm = 256 if 2 * T >= 4096 else 128
    nt = (2 * T) // tm + n_exp
    onehot = (eid[None, :] == jnp.arange(n_exp, dtype=I32)[:, None]).astype(I32)
    csum = jnp.cumsum(onehot, axis=1)
    counts = csum[:, -1]
    rank = jnp.sum(onehot * csum, axis=0) - 1
    tiles = (counts + tm - 1) // tm
    tile_end = jnp.cumsum(tiles)
    tile_start = tile_end - tiles
    n_used = tile_end[-1:]
    j = jnp.arange(nt, dtype=I32)
    tile_expert = jnp.minimum(jnp.sum(tile_end[None, :] <= j[:, None], axis=1, dtype=I32), n_exp - 1)
    last_expert = jnp.sum(tile_end < n_used[0], dtype=I32)
    tile_expert = jnp.where(j < n_used[0], tile_expert, last_expert)
    pos = (tile_start[eid] * tm + rank).astype(I32)
    inv = jnp.full((nt * tm,), -1, I32).at[pos].set(jnp.arange(2 * T, dtype=I32))
    valid = inv >= 0
    src_pair = jnp.maximum(inv, 0)
    src_tok = jnp.where(valid, src_pair // 2, 0)
    gate_s = jnp.where(valid, wts[src_pair], 0.0).reshape(nt * tm, 1)
    pos = pos.reshape(T, 2)

    xs = gather_rows(x2d, src_tok, n_used, tm)
    ys = moe_ffn(xs, gate_s, gain, w_gate.reshape(depth * n_exp, D, F), w_up.reshape(depth * n_exp, D, F),
                 w_down.reshape(depth * n_exp, F, D), tile_expert + layer * n_exp, n_used, tm)
    return moe_combine(x2d, ys, pos[:, 0], pos[:, 1], min(256, T))


def _ple_kernel(a_ref, w_ref, p_ref, pw_ref, x_ref, o_ref):
    z = jnp.dot(a_ref[...], w_ref[...].astype(BF16), preferred_element_type=F32)
    e = jnp.dot(p_ref[...].astype(BF16), pw_ref[...].astype(BF16), preferred_element_type=F32)
    o_ref[...] = x_ref[...] + jax.nn.sigmoid(z) * e


def ple_residual(x2d, gain, w_gate, p3d, w_proj, layer):
    T, D = x2d.shape
    P = p3d.shape[2]
    a = rmsnorm_cast(x2d, gain)
    tm, tn = min(1024, T), 512
    return pl.pallas_call(
        _ple_kernel,
        grid=(T // tm, D // tn),
        in_specs=[pl.BlockSpec((tm, D), lambda i, j: (i, 0), pipeline_mode=pl.Buffered(1)),
                  pl.BlockSpec((None, D, tn), lambda i, j: (layer, 0, j)),
                  pl.BlockSpec((None, tm, P), lambda i, j: (layer, i, 0)),
                  pl.BlockSpec((None, P, tn), lambda i, j: (layer, 0, j)),
                  pl.BlockSpec((tm, tn), lambda i, j: (i, j))],
        out_specs=pl.BlockSpec((tm, tn), lambda i, j: (i, j)),
        out_shape=jax.ShapeDtypeStruct((T, D), F32),
        compiler_params=_cparams(("parallel", "parallel")),
        name="ple",
    )(a, w_gate, p3d, w_proj, x2d)


def _retention_kernel(q_ref, k_ref, v_ref, g_ref, cos_ref, sin_ref, intra_ref, qd_ref, kd_ref, cd_ref, gn_ref,
                      o_ref, state_ref):
    S = q_ref.shape[0]
    C = intra_ref.shape[1]
    half = q_ref.shape[1] // 2
    state_ref[...] = jnp.zeros_like(state_ref)
    intra = intra_ref[0]
    qd, kd, cd, gn = qd_ref[0], kd_ref[0], cd_ref[0], gn_ref[0]

    def rot(x, cos, sin):
        x1, x2 = x[:, :half], x[:, half:]
        return jnp.concatenate([x1 * cos - x2 * sin, x1 * sin + x2 * cos], axis=1)

    def chunk(c, carry):
        rows = pl.ds(pl.multiple_of(c * C, C), C)
        cos, sin = cos_ref[rows, :], sin_ref[rows, :]
        qc = rot(q_ref[rows, :], cos, sin)
        kc = rot(k_ref[rows, :], cos, sin) * (RET_QK_DIM ** -0.5)
        vc = v_ref[rows, :].astype(BF16)
        att = lax.dot_general(qc.astype(BF16), kc.astype(BF16), _NT, preferred_element_type=F32) * intra
        inner = jnp.dot(att.astype(BF16), vc, preferred_element_type=F32)
        state = state_ref[...]
        cross = jnp.dot((qc * qd).astype(BF16), state.astype(BF16), preferred_element_type=F32)
        kt = jnp.transpose(kc * kd).astype(BF16)
        state_ref[...] = state * cd + jnp.dot(kt, vc, preferred_element_type=F32)
        o = inner + cross
        o = o * lax.rsqrt(jnp.mean(o * o, axis=-1, keepdims=True) + RMS_EPS) * gn
        g = g_ref[rows, :]
        o_ref[rows, :] = (g * jax.nn.sigmoid(g) * o).astype(o_ref.dtype)
        return carry

    lax.fori_loop(0, S // C, chunk, 0)


def retention_core(proj, cos, sin, out_norm, B, S, n_heads):
    T = proj.shape[0]
    dk = RET_QK_DIM
    dv = 2 * dk
    C = RET_CHUNK
    log_gamma = jnp.log1p(-jnp.exp2(-5.0 - jnp.arange(n_heads, dtype=F32)))
    i = jnp.arange(C, dtype=F32)
    dist = i[:, None] - i[None, :]
    intra = jnp.where(dist[None] >= 0, jnp.exp(log_gamma[:, None, None] * jnp.maximum(dist, 0.0)[None]), 0.0)
    q_decay = jnp.broadcast_to(jnp.exp(log_gamma[:, None] * (i[None, :] + 1.0))[:, :, None], (n_heads, C, dk))
    k_decay = jnp.broadcast_to(jnp.exp(log_gamma[:, None] * (C - 1.0 - i)[None, :])[:, :, None], (n_heads, C, dk))
    chunk_decay = jnp.broadcast_to(jnp.exp(log_gamma * C)[:, None, None], (n_heads, 1, dv))
    nq = n_heads
    return pl.pallas_call(
        _retention_kernel,
        grid=(B, n_heads),
        in_specs=[pl.BlockSpec((S, dk), lambda b, h: (b, h)),
                  pl.BlockSpec((S, dk), lambda b, h: (b, nq + h)),
                  pl.BlockSpec((S, dv), lambda b, h: (b, nq + h)),
                  pl.BlockSpec((S, dv), lambda b, h: (b, 2 * nq + h)),
                  pl.BlockSpec((S, dk // 2), lambda b, h: (b, 0)),
                  pl.BlockSpec((S, dk // 2), lambda b, h: (b, 0)),
                  pl.BlockSpec((1, C, C), lambda b, h: (h, 0, 0)),
                  pl.BlockSpec((1, C, dk), lambda b, h: (h, 0, 0)),
                  pl.BlockSpec((1, C, dk), lambda b, h: (h, 0, 0)),
                  pl.BlockSpec((1, 1, dv), lambda b, h: (h, 0, 0)),
                  pl.BlockSpec((1, 1, dv), lambda b, h: (h, 0, 0))],
        out_specs=pl.BlockSpec((S, dv), lambda b, h: (b, h)),
        out_shape=jax.ShapeDtypeStruct((T, n_heads * dv), BF16),
        scratch_shapes=[pltpu.VMEM((dk, dv), F32)],
        compiler_params=_cparams(("parallel", "arbitrary")),
        name="retention_core",
    )(proj, proj, proj, proj, cos, sin, intra, q_decay, k_decay, chunk_decay,
      out_norm.reshape(n_heads, 1, dv))


def _rot_tables(positions, dim, theta):
    inv = theta ** (-jnp.arange(0, dim, 2, dtype=F32) / dim)
    ang = positions.astype(F32).reshape(-1, 1) * inv[None, :]
    return jnp.cos(ang), jnp.sin(ang)


def dsa_layer(x2d, positions, gain, w_in, w_out, q_norm, k_norm, idx_k_norm, layer, B, S):
    T, D = x2d.shape
    n_heads = D // HEAD_DIM
    n_kv = n_heads // 4
    rot_dim = HEAD_DIM // 4
    cos, sin = _rot_tables(positions, rot_dim, ROPE_THETA)
    ones = jnp.ones((T, HEAD_DIM - rot_dim), F32)
    zeros_r = jnp.zeros((T, HEAD_DIM - rot_dim // 2), F32)
    tabs = (jnp.concatenate([cos, cos, ones], axis=1),
            jnp.concatenate([-sin, zeros_r], axis=1),
            jnp.concatenate([jnp.zeros_like(sin), sin, jnp.zeros((T, HEAD_DIM - rot_dim), F32)], axis=1))
    hn = rmsnorm_cast(x2d, gain)
    tn = 512
    n_cols = w_in.shape[2]
    proj = matmul(hn, w_in, layer, n_out=-(-n_cols // tn) * tn, tm=2048, tn=tn, name="dsa_in_proj")
    off_k = n_heads * HEAD_DIM
    off_v = off_k + n_kv * HEAD_DIM
    off_qi = off_v + n_kv * HEAD_DIM
    off_ki = off_qi + IDX_HEADS * HEAD_DIM
    q = heads_prep(proj, 0, n_heads, tabs, q_norm, norm=True, rot=True, scale=HEAD_DIM ** -0.5 * LOG2_E,
                   name="prep_q")
    k = heads_prep(proj, off_k, n_kv, tabs, k_norm, norm=True, rot=True, name="prep_k")
    v = heads_prep(proj, off_v, n_kv, tabs, k_norm, norm=False, rot=False, name="prep_v")
    qi = heads_prep(proj, off_qi, IDX_HEADS, tabs, k_norm, norm=False, rot=True, name="prep_qi")
    ki, wi = kiwi_prep(proj, off_ki, tabs, idx_k_norm, IDX_HEADS ** -0.5 * HEAD_DIM ** -0.5)
    top_k = min(IDX_TOPK_MAX, S // 4)
    o = dsa_attention_core(qi, ki, wi, q, k, v, B, S, top_k)
    return matmul(o, w_out, layer, res=x2d, head_major=True, tm=2048, tn=512, name="dsa_out_proj")


def retention_layer(x2d, positions, gain, w_in, w_out, out_norm, layer, B, S):
    T, D = x2d.shape
    n_heads = D // RET_QK_DIM
    cos, sin = _rot_tables(positions, RET_QK_DIM, RET_THETA)
    hn = rmsnorm_cast(x2d, gain)
    proj = matmul(hn, w_in, layer, tm=2048, tn=512, name="ret_in_proj")
    og = retention_core(proj, cos, sin, out_norm, B, S, n_heads)
    return matmul(og, w_out, layer, res=x2d, tm=1024, tn=256, name="ret_out_proj")


def kernel(x, p, positions, attn_norm, ffn_norm, ple_norm, a_w_in, a_w_out, a_q_norm, a_k_norm, a_idx_k_norm,
           b_w_in, b_w_out, b_out_norm, moe_w_group, moe_b_group, moe_w_expert, moe_b_expert, moe_w_gate,
           moe_w_up, moe_w_down, ple_proj, ple_gate):
    B, S, D = x.shape
    depth = p.shape[0]
    T = B * S
    x2d = x.reshape(T, D)
    p3d = p.reshape(depth, T, p.shape[-1])
    for i in range(depth):
        j = i // 2
        if i % 2 == 0:
            x2d = dsa_layer(x2d, positions, attn_norm[i], a_w_in, a_w_out, a_q_norm[j], a_k_norm[j],
                            a_idx_k_norm[j], j, B, S)
        else:
            x2d = retention_layer(x2d, positions, attn_norm[i], b_w_in, b_w_out, b_out_norm[j], j, B, S)
        x2d = hier_moe_residual(x2d, ffn_norm[i], moe_w_group[i], moe_b_group[i], moe_w_expert[i],
                                moe_b_expert[i], moe_w_gate, moe_w_up, moe_w_down, i)
        x2d = ple_residual(x2d, ple_norm[i], ple_gate, p3d, ple_proj, i)
    return x2d.reshape(B, S, D)
```

```python
import functools

import jax
import jax.numpy as jnp
from jax import lax
from jax.experimental import pallas as pl
from jax.experimental.pallas import tpu as pltpu

F32 = jnp.float32
BF16 = jnp.bfloat16
I32 = jnp.int32

RMS_EPS = 1e-6
HEAD_DIM = 128
IDX_HEADS = 64
IDX_TOPK_MAX = 256
ROPE_THETA = 500000.0
RET_QK_DIM = 256
RET_THETA = 10000.0
RET_CHUNK = 128
N_GROUPS = 8
EXPERTS_PER_GROUP = 8
LANES = 128
VMEM_LIMIT_BYTES = 56 * 1024 * 1024
NEG_BIG = -1e30
INT_MIN = -2147483648


def _cparams(sem):
    return pltpu.CompilerParams(dimension_semantics=sem, vmem_limit_bytes=VMEM_LIMIT_BYTES)


def _pick(n, cands):
    for c in cands:
        if n % c == 0:
            return c
    raise ValueError(f"no tile for {n} in {cands}")


def _rmsnorm_kernel(x_ref, g_ref, o_ref):
    x = x_ref[...]
    ms = jnp.mean(x * x, axis=-1, keepdims=True)
    o_ref[...] = (x * lax.rsqrt(ms + RMS_EPS) * g_ref[...]).astype(o_ref.dtype)


def rmsnorm_cast(x2d, gain):
    T, D = x2d.shape
    tm = _pick(T, (512, 256, 128, 8))
    return pl.pallas_call(
        _rmsnorm_kernel,
        grid=(T // tm,),
        in_specs=[pl.BlockSpec((tm, D), lambda i: (i, 0)), pl.BlockSpec((1, D), lambda i: (0, 0))],
        out_specs=pl.BlockSpec((tm, D), lambda i: (i, 0)),
        out_shape=jax.ShapeDtypeStruct((T, D), BF16),
        compiler_params=_cparams(("parallel",)),
        name="rmsnorm_cast",
    )(x2d, gain.reshape(1, D))


def _mm_kernel(*refs, nk, head_major, has_res, n_valid):
    a_ref, w_ref = refs[0], refs[1]
    r_ref = refs[2] if has_res else None
    o_ref = refs[2 + has_res]
    acc_ref = refs[3 + has_res] if nk > 1 else None
    if head_major:
        a = jnp.concatenate([a_ref[h] for h in range(a_ref.shape[0])], axis=1)
    else:
        a = a_ref[...]
    w = w_ref[...]
    if n_valid % w.shape[1]:
        col = pl.program_id(1) * w.shape[1] + lax.broadcasted_iota(I32, (1, w.shape[1]), 1)
        w = jnp.where(col < n_valid, w, 0.0)
    part = jnp.dot(a, w.astype(BF16), preferred_element_type=F32)

    def finish(acc):
        if has_res:
            acc = acc + r_ref[...]
        o_ref[...] = acc.astype(o_ref.dtype)

    if nk == 1:
        finish(part)
    else:
        k = pl.program_id(2)

        @pl.when(k == 0)
        def _():
            acc_ref[...] = part

        @pl.when(k > 0)
        def _():
            acc_ref[...] += part

        @pl.when(k == nk - 1)
        def _():
            finish(acc_ref[...])


def matmul(a, w, layer, *, n_out=None, res=None, out_dtype=F32, tm=1024, tn=512, tk=None, head_major=False,
           name="matmul"):
    if head_major:
        nh, M, hd = a.shape
        K = nh * hd
    else:
        M, K = a.shape
    N = w.shape[2]
    n_out = N if n_out is None else n_out
    tm = min(tm, M)
    tk = K if (tk is None or head_major) else min(tk, K)
    assert M % tm == 0 and K % tk == 0 and n_out % tn == 0
    nk = K // tk
    a_mode = dict(pipeline_mode=pl.Buffered(1)) if nk == 1 else {}
    if head_major:
        a_spec = pl.BlockSpec((nh, tm, hd), lambda i, j, k: (0, i, 0), **a_mode)
    else:
        a_spec = pl.BlockSpec((tm, tk), lambda i, j, k: (i, k), **a_mode)
    in_specs = [a_spec, pl.BlockSpec((None, tk, tn), lambda i, j, k: (layer, k, j))]
    args = [a, w]
    if res is not None:
        in_specs.append(pl.BlockSpec((tm, tn), lambda i, j, k: (i, j)))
        args.append(res)
    return pl.pallas_call(
        functools.partial(_mm_kernel, nk=nk, head_major=head_major, has_res=res is not None, n_valid=N),
        grid=(M // tm, n_out // tn, nk),
        in_specs=in_specs,
        out_specs=pl.BlockSpec((tm, tn), lambda i, j, k: (i, j)),
        out_shape=jax.ShapeDtypeStruct((M, n_out), out_dtype),
        scratch_shapes=[pltpu.VMEM((tm, tn), F32)] if nk > 1 else [],
        compiler_params=_cparams(("parallel", "parallel", "arbitrary")),
        name=name,
    )(*args)


def _rot_head(x, c, s1, s2):
    half = HEAD_DIM // 8
    return x * c + pltpu.roll(x, HEAD_DIM - half, 1) * s1 + pltpu.roll(x, half, 1) * s2


def _flag(pred, shape):
    return jnp.broadcast_to(pred.astype(I32), shape) == 1


def _dsa_proj_kernel(a_ref, w_ref, c_ref, s1_ref, s2_ref, gains_ref, heads_ref, wi_ref, *, n_valid, bounds,
                     wi_lane, wscale, row_chunk):
    j = pl.program_id(1)
    tm = a_ref.shape[0]
    tn = w_ref.shape[1]
    b_q, b_k, b_v, b_qi = bounds
    col = j * tn + lax.broadcasted_iota(I32, (1, tn), 1)
    w = jnp.where(col < n_valid, w_ref[...], 0.0).astype(BF16)
    is_q = j < b_q
    is_k = jnp.logical_and(j >= b_q, j < b_k)
    is_v = jnp.logical_and(j >= b_k, j < b_v)
    is_ki = j == b_qi
    lane = (1, HEAD_DIM)
    gain = jnp.where(_flag(is_q, lane), gains_ref[0:1, :],
                     jnp.where(_flag(is_k, lane), gains_ref[1:2, :],
                               jnp.where(_flag(is_ki, lane), gains_ref[2:3, :], 1.0)))
    use_norm = _flag(jnp.logical_or(jnp.logical_or(is_q, is_k), is_ki), (row_chunk, 1))
    no_rot = _flag(is_v, (row_chunk, HEAD_DIM))
    for rc in range(tm // row_chunk):
        rows = pl.ds(rc * row_chunk, row_chunk)
        acc = jnp.dot(a_ref[rows, :], w, preferred_element_type=F32)
        c = jnp.where(no_rot, 1.0, c_ref[rows, :])
        s1 = jnp.where(no_rot, 0.0, s1_ref[rows, :])
        s2 = jnp.where(no_rot, 0.0, s2_ref[rows, :])
        for hh in range(tn // HEAD_DIM):
            x = acc[:, hh * HEAD_DIM:(hh + 1) * HEAD_DIM]
            ms = jnp.mean(x * x, axis=-1, keepdims=True)
            x = x * jnp.where(use_norm, lax.rsqrt(ms + RMS_EPS), 1.0) * gain
            heads_ref[hh, rows, :] = _rot_head(x, c, s1, s2).astype(heads_ref.dtype)
        wi_ref[rows, :] = acc[:, wi_lane:wi_lane + HEAD_DIM] * wscale


def dsa_in_proj(hn, w_in, layer, tabs, q_gain, k_gain, ki_gain, n_heads, n_kv):
    T, K = hn.shape
    n_cols = w_in.shape[2]
    seg_ends = [n_heads * HEAD_DIM]
    for width in (n_kv * HEAD_DIM, n_kv * HEAD_DIM, IDX_HEADS * HEAD_DIM):
        seg_ends.append(seg_ends[-1] + width)
    tn = next(t for t in (512, 256, 128) if all(e % t == 0 for e in seg_ends))
    hb = tn // HEAD_DIM
    bounds = tuple(e // tn for e in seg_ends)
    b_v, b_qi = bounds[2], bounds[3]
    n_out = -(-n_cols // tn) * tn
    wi_lane = (seg_ends[-1] + HEAD_DIM) % tn
    assert n_cols == seg_ends[-1] + HEAD_DIM + IDX_HEADS and (n_out - tn) <= seg_ends[-1] + HEAD_DIM
    tm = min(2048, T)
    row_chunk = min(512, tm)
    c, s1, s2 = tabs
    gains = jnp.zeros((8, HEAD_DIM), F32).at[0].set(q_gain * (HEAD_DIM ** -0.5 * LOG2_E)).at[1].set(k_gain)
    gains = gains.at[2].set(ki_gain)
    once = dict(pipeline_mode=pl.Buffered(1))
    row = lambda i, j: (i, 0)

    def head_block(i, j):
        pos = jnp.where(j < b_v, j + (b_qi - b_v), jnp.where(j < b_qi, j - b_v, j))
        return (pos, i, 0)

    return pl.pallas_call(
        functools.partial(_dsa_proj_kernel, n_valid=n_cols, bounds=bounds, wi_lane=wi_lane,
                          wscale=IDX_HEADS ** -0.5 * HEAD_DIM ** -0.5, row_chunk=row_chunk),
        grid=(T // tm, n_out // tn),
        in_specs=[pl.BlockSpec((tm, K), row, **once),
                  pl.BlockSpec((None, K, tn), lambda i, j: (layer, 0, j)),
                  pl.BlockSpec((tm, HEAD_DIM), row, **once), pl.BlockSpec((tm, HEAD_DIM), row, **once),
                  pl.BlockSpec((tm, HEAD_DIM), row, **once), pl.BlockSpec((8, HEAD_DIM), lambda i, j: (0, 0))],
        out_specs=[pl.BlockSpec((hb, tm, HEAD_DIM), head_block), pl.BlockSpec((tm, HEAD_DIM), row)],
        out_shape=[jax.ShapeDtypeStruct((n_out // HEAD_DIM, T, HEAD_DIM), BF16),
                   jax.ShapeDtypeStruct((T, HEAD_DIM), F32)],
        compiler_params=_cparams(("parallel", "arbitrary")),
        name="dsa_in_proj",
    )(hn, w_in, c, s1, s2, gains)


_NT = (((1,), (1,)), ((), ()))


IDX_UNROLL = 4
LOG2_E = 1.4426950408889634


def _dsa_tile(qi_ref, ki_ref, wi_ref, q_ref, k_ref, v_ref, o_ref, sc_ref, u_ref, bias_ref,
              *, top_k, sub, steps_per_kv, nk, q0):
    tq = sc_ref.shape[1]
    keys = pl.ds(0, nk)

    def idx_term(h):
        s = lax.dot_general(ki_ref[keys, :], qi_ref[h], _NT, preferred_element_type=F32)
        return jnp.maximum(s, 0.0) * wi_ref[pl.ds(h, 1), :]

    def idx_terms(h0):
        acc = idx_term(h0)
        for d in range(1, IDX_UNROLL):
            acc = acc + idx_term(h0 + d)
        return acc

    sc_ref[keys, :] = idx_terms(0)

    def idx_body(g, carry):
        sc_ref[keys, :] += idx_terms(g * IDX_UNROLL)
        return carry

    lax.fori_loop(1, IDX_HEADS // IDX_UNROLL, idx_body, 0)

    key_pos = lax.broadcasted_iota(I32, (nk, tq), 0)
    q_pos = q0 + lax.broadcasted_iota(I32, (nk, tq), 1)
    causal = key_pos <= q_pos
    bits = pltpu.bitcast(sc_ref[keys, :], I32)
    u = bits ^ ((bits >> 31) & 0x7FFFFFFF)
    u_ref[keys, :] = jnp.where(causal, u, INT_MIN)

    def count_ge(cand):
        return jnp.sum(jnp.where(u_ref[keys, :] >= cand, 1.0, 0.0), axis=0, keepdims=True)

    zero = jnp.zeros((1, tq), I32)
    v0 = jnp.where(count_ge(zero) >= top_k, zero, jnp.full((1, tq), INT_MIN, I32))

    def bis_body(it, v):
        cand = v + lax.shift_left(jnp.int32(1), 30 - it)
        return jnp.where(count_ge(cand) >= top_k, cand, v)

    thr = lax.fori_loop(0, 31, bis_body, v0)
    keep = jnp.logical_and(u_ref[keys, :] >= thr, causal)
    bias_ref[:, keys] = jnp.transpose(jnp.where(keep, 0.0, NEG_BIG))

    def att_body(st, carry):
        n = st // steps_per_kv
        q = q_ref[pl.ds(st * sub, sub)].reshape(sub * tq, HEAD_DIM)
        lg = lax.dot_general(q, k_ref[n, keys, :], _NT, preferred_element_type=F32)
        ps, ls = [], []
        for g in range(sub):
            x = lg[g * tq:(g + 1) * tq] + bias_ref[:, keys]
            p = jnp.exp2(x - jnp.max(x, axis=-1, keepdims=True))
            ls.append(jnp.sum(p, axis=-1, keepdims=True))
            ps.append(p.astype(BF16))
        o = jnp.dot(jnp.concatenate(ps, axis=0), v_ref[n, keys, :], preferred_element_type=F32)
        for g in range(sub):
            o_ref[st * sub + g] = (o[g * tq:(g + 1) * tq] / ls[g]).astype(o_ref.dtype)
        return carry

    lax.fori_loop(0, q_ref.shape[0] // sub, att_body, 0)


def _dsa_kernel(*refs, **static):
    S, tq = refs[7].shape
    i = pl.program_id(1)
    for c in range(S // tq):
        @pl.when(i == c)
        def _(c=c):
            _dsa_tile(*refs, nk=(c + 1) * tq, q0=c * tq, **static)


def dsa_attention_core(heads, wi, n_heads, n_kv, B, S, top_k):
    T = heads.shape[1]
    tq = _pick(S, (256, 128))
    nq = S // tq
    group = n_heads // n_kv
    sub = 2 if group % 2 == 0 else 1
    off_q, off_k = IDX_HEADS, IDX_HEADS + n_heads
    off_v, off_ki = off_k + n_kv, off_k + 2 * n_kv
    assert off_q % n_heads == 0 and off_k % n_kv == 0 and off_v % n_kv == 0
    once = dict(pipeline_mode=pl.Buffered(1))
    return pl.pallas_call(
        functools.partial(_dsa_kernel, top_k=top_k, sub=sub, steps_per_kv=group // sub),
        grid=(B, nq),
        in_specs=[pl.BlockSpec((IDX_HEADS, tq, HEAD_DIM), lambda b, i: (0, b * nq + i, 0)),
                  pl.BlockSpec((None, S, HEAD_DIM), lambda b, i: (off_ki, b, 0), **once),
                  pl.BlockSpec((IDX_HEADS, tq), lambda b, i: (0, b * nq + i)),
                  pl.BlockSpec((n_heads, tq, HEAD_DIM), lambda b, i: (off_q // n_heads, b * nq + i, 0)),
                  pl.BlockSpec((n_kv, S, HEAD_DIM), lambda b, i: (off_k // n_kv, b, 0), **once),
                  pl.BlockSpec((n_kv, S, HEAD_DIM), lambda b, i: (off_v // n_kv, b, 0), **once)],
        out_specs=pl.BlockSpec((n_heads, tq, HEAD_DIM), lambda b, i: (0, b * nq + i, 0)),
        out_shape=jax.ShapeDtypeStruct((n_heads, T, HEAD_DIM), BF16),
        scratch_shapes=[pltpu.VMEM((S, tq), F32), pltpu.VMEM((S, tq), I32), pltpu.VMEM((tq, S), F32)],
        compiler_params=_cparams(("parallel", "arbitrary")),
        name="dsa_core",
    )(heads, heads, wi, heads, heads, heads)


def _route_kernel(x_ref, g_ref, w_ref, b_ref, o_ref):
    x = x_ref[...]
    ms = jnp.mean(x * x, axis=-1, keepdims=True)
    t = x * lax.rsqrt(ms + RMS_EPS) * g_ref[...]
    logits = jnp.dot(t, w_ref[...], precision=lax.Precision.HIGHEST, preferred_element_type=F32) + b_ref[...]
    lane = lax.broadcasted_iota(I32, logits.shape, 1)
    lane_f = lane.astype(F32)
    big = float(LANES)
    gl = jnp.where(lane < N_GROUPS, logits, -jnp.inf)
    gmax = jnp.max(gl, axis=-1, keepdims=True)
    g_sel = jnp.min(jnp.where(gl == gmax, lane_f, big), axis=-1, keepdims=True)
    g_w = 1.0 / jnp.sum(jnp.exp(gl - gmax), axis=-1, keepdims=True)
    lo = N_GROUPS + g_sel * EXPERTS_PER_GROUP
    in_grp = jnp.logical_and(lane_f >= lo, lane_f < lo + EXPERTS_PER_GROUP)
    el = jnp.where(in_grp, logits, -jnp.inf)
    m1 = jnp.max(el, axis=-1, keepdims=True)
    i1 = jnp.min(jnp.where(el == m1, lane_f, big), axis=-1, keepdims=True)
    el2 = jnp.where(lane_f == i1, -jnp.inf, el)
    m2 = jnp.max(el2, axis=-1, keepdims=True)
    i2 = jnp.min(jnp.where(el2 == m2, lane_f, big), axis=-1, keepdims=True)
    z = jnp.sum(jnp.exp(el - m1), axis=-1, keepdims=True)
    p1 = 1.0 / z
    p2 = jnp.exp(m2 - m1) / z
    w1 = p1 / (p1 + p2) * g_w
    w2 = p2 / (p1 + p2) * g_w
    out = jnp.where(lane == 0, i1 - N_GROUPS, jnp.where(lane == 1, i2 - N_GROUPS,
          jnp.where(lane == 2, w1, jnp.where(lane == 3, w2, 0.0))))
    o_ref[...] = out


def moe_route(x2d, gain, w_group, b_group, w_expert, b_expert):
    T, D = x2d.shape
    n_log = w_group.shape[1] + w_expert.shape[1]
    w = jnp.concatenate([w_group, w_expert, jnp.zeros((D, LANES - n_log), F32)], axis=1)
    b = jnp.concatenate([b_group, b_expert, jnp.zeros((LANES - n_log,), F32)]).reshape(1, LANES)
    tm = _pick(T, (256, 128, 8))
    return pl.pallas_call(
        _route_kernel,
        grid=(T // tm,),
        in_specs=[pl.BlockSpec((tm, D), lambda i: (i, 0)), pl.BlockSpec((1, D), lambda i: (0, 0)),
                  pl.BlockSpec((D, LANES), lambda i: (0, 0)), pl.BlockSpec((1, LANES), lambda i: (0, 0))],
        out_specs=pl.BlockSpec((tm, LANES), lambda i: (i, 0)),
        out_shape=jax.ShapeDtypeStruct((T, LANES), F32),
        compiler_params=_cparams(("parallel",)),
        name="moe_route",
    )(x2d, gain.reshape(1, D), w, b)


def _row_copy(src_hbm, dst_ref, sem, src_row, dst_row):
    return pltpu.make_async_copy(src_hbm.at[pl.ds(src_row, 1), :], dst_ref.at[pl.ds(dst_row, 1), :], sem)


DMA_PRIORITIES = 2


def _start_row_gather(idx_ref, src_hbm, dst_ref, sem, base, n_rows):
    def start(g, c):
        for pr in range(DMA_PRIORITIES):
            r = g * DMA_PRIORITIES + pr
            _row_copy(src_hbm, dst_ref, sem, idx_ref[base + r], r).start(priority=pr)
        return c

    lax.fori_loop(0, n_rows // DMA_PRIORITIES, start, 0)


def _wait_row_gather(src_hbm, dst_ref, sem, n_rows):
    def wait(r, c):
        _row_copy(src_hbm, dst_ref, sem, 0, r).wait()
        return c

    lax.fori_loop(0, n_rows, wait, 0)


def _ffn_kernel(src_ref, te_ref, nu_ref, x_hbm, gate_ref, g_ref, wg_ref, wu_ref, wd_ref, o_ref, xbuf, sem):
    j = pl.program_id(0)
    tm = o_ref.shape[0]
    n_used = nu_ref[0]

    def fetch(tile, slot):
        _start_row_gather(src_ref, x_hbm, xbuf.at[slot], sem.at[slot], tile * tm, tm)

    @pl.when(jnp.logical_and(j == 0, n_used > 0))
    def _():
        fetch(0, 0)

    @pl.when(j + 1 < n_used)
    def _():
        fetch(j + 1, lax.rem(j + 1, 2))

    @pl.when(j < n_used)
    def _():
        slot = lax.rem(j, 2)
        _wait_row_gather(x_hbm, xbuf.at[slot], sem.at[slot], tm)
        x = xbuf[slot]
        ms = jnp.mean(x * x, axis=-1, keepdims=True)
        t = (x * lax.rsqrt(ms + RMS_EPS) * g_ref[...]).astype(BF16)
        a = jnp.dot(t, wg_ref[0].astype(BF16), preferred_element_type=F32)
        u = jnp.dot(t, wu_ref[0].astype(BF16), preferred_element_type=F32)
        hid = a * jax.nn.sigmoid(a) * u * gate_ref[...]
        o_ref[...] = jnp.dot(hid.astype(BF16), wd_ref[0].astype(BF16), preferred_element_type=F32)

    @pl.when(j >= n_used)
    def _():
        o_ref[...] = jnp.zeros_like(o_ref)


def moe_ffn(x2d, src_tok, gate_s, gain, w_gate, w_up, w_down, tile_expert, n_used, tm):
    D = x2d.shape[1]
    P = src_tok.shape[0]
    F = w_gate.shape[-1]
    assert P % tm == 0 and tm % DMA_PRIORITIES == 0
    wmap = lambda j, src, te, nu: (te[j], 0, 0)
    return pl.pallas_call(
        _ffn_kernel,
        grid_spec=pltpu.PrefetchScalarGridSpec(
            num_scalar_prefetch=3, grid=(P // tm,),
            in_specs=[pl.BlockSpec(memory_space=pl.ANY),
                      pl.BlockSpec((tm, 1), lambda j, src, te, nu: (j, 0)),
                      pl.BlockSpec((1, D), lambda j, src, te, nu: (0, 0)),
                      pl.BlockSpec((1, D, F), wmap), pl.BlockSpec((1, D, F), wmap),
                      pl.BlockSpec((1, F, D), wmap)],
            out_specs=pl.BlockSpec((tm, D), lambda j, src, te, nu: (j, 0)),
            scratch_shapes=[pltpu.VMEM((2, tm, D), F32), pltpu.SemaphoreType.DMA((2,))]),
        out_shape=jax.ShapeDtypeStruct((P, D), F32),
        compiler_params=_cparams(("arbitrary",)),
        name="moe_ffn",
    )(src_tok, tile_expert, n_used, x2d, gate_s, gain.reshape(1, D), w_gate, w_up, w_down)


def _combine_kernel(p1_ref, p2_ref, ys_hbm, x_ref, o_ref, buf1, buf2, sem):
    R = x_ref.shape[0]
    base = pl.program_id(0) * R
    _start_row_gather(p1_ref, ys_hbm, buf1, sem.at[0], base, R)
    _start_row_gather(p2_ref, ys_hbm, buf2, sem.at[1], base, R)
    _wait_row_gather(ys_hbm, buf1, sem.at[0], R)
    _wait_row_gather(ys_hbm, buf2, sem.at[1], R)
    o_ref[...] = x_ref[...] + (buf1[...] + buf2[...])


def moe_combine(x2d, ys, pos1, pos2, rows_per_step):
    T, D = x2d.shape
    R = rows_per_step
    return pl.pallas_call(
        _combine_kernel,
        grid_spec=pltpu.PrefetchScalarGridSpec(
            num_scalar_prefetch=2, grid=(T // R,),
            in_specs=[pl.BlockSpec(memory_space=pl.ANY),
                      pl.BlockSpec((R, D), lambda i, a, b: (i, 0))],
            out_specs=pl.BlockSpec((R, D), lambda i, a, b: (i, 0)),
            scratch_shapes=[pltpu.VMEM((R, D), F32), pltpu.VMEM((R, D), F32), pltpu.SemaphoreType.DMA((2,))]),
        out_shape=jax.ShapeDtypeStruct((T, D), F32),
        compiler_params=_cparams(("arbitrary",)),
        name="moe_combine",
    )(pos1, pos2, ys, x2d)


def hier_moe_residual(x2d, gain, w_group, b_group, w_expert, b_expert, w_gate, w_up, w_down, layer):
    T, D = x2d.shape
    depth, G, E = w_gate.shape[0], w_gate.shape[1], w_gate.shape[2]
    n_exp = G * E
    F = w_gate.shape[-1]
    route = moe_route(x2d, gain, w_group, b_group, w_expert, b_expert)
    eid = route[:, 0:2].astype(I32).reshape(2 * T)
    wts = route[:, 2:4].reshape(2 * T)
    tm = 256 if 2 * T >= 4096 else 128
    nt = (2 * T) // tm + n_exp
    onehot = (eid[None, :] == jnp.arange(n_exp, dtype=I32)[:, None]).astype(I32)
    csum = jnp.cumsum(onehot, axis=1)
    counts = csum[:, -1]
    rank = jnp.sum(onehot * csum, axis=0) - 1
    tiles = (counts + tm - 1) // tm
    tile_end = jnp.cumsum(tiles)
    tile_start = tile_end - tiles
    n_used = tile_end[-1:]
    j = jnp.arange(nt, dtype=I32)
    tile_expert = jnp.minimum(jnp.sum(tile_end[None, :] <= j[:, None], axis=1, dtype=I32), n_exp - 1)
    last_expert = jnp.sum(tile_end < n_used[0], dtype=I32)
    tile_expert = jnp.where(j < n_used[0], tile_expert, last_expert)
    pos = (tile_start[eid] * tm + rank).astype(I32)
    inv = jnp.full((nt * tm,), -1, I32).at[pos].set(jnp.arange(2 * T, dtype=I32))
    valid = inv >= 0
    src_pair = jnp.maximum(inv, 0)
    src_tok = jnp.where(valid, src_pair // 2, 0)
    gate_s = jnp.where(valid, wts[src_pair], 0.0).reshape(nt * tm, 1)
    pos = pos.reshape(T, 2)

    ys = moe_ffn(x2d, src_tok, gate_s, gain, w_gate.reshape(depth * n_exp, D, F),
                 w_up.reshape(depth * n_exp, D, F), w_down.reshape(depth * n_exp, F, D),
                 tile_expert + layer * n_exp, n_used, tm)
    return moe_combine(x2d, ys, pos[:, 0], pos[:, 1], min(256, T))


def _ple_kernel(a_ref, w_ref, p_ref, pw_ref, x_ref, o_ref):
    z = jnp.dot(a_ref[...], w_ref[...].astype(BF16), preferred_element_type=F32)
    e = jnp.dot(p_ref[...].astype(BF16), pw_ref[...].astype(BF16), preferred_element_type=F32)
    o_ref[...] = x_ref[...] + jax.nn.sigmoid(z) * e


def ple_residual(x2d, gain, w_gate, p3d, w_proj, layer):
    T, D = x2d.shape
    P = p3d.shape[2]
    a = rmsnorm_cast(x2d, gain)
    tm, tn = min(1024, T), 512
    return pl.pallas_call(
        _ple_kernel,
        grid=(T // tm, D // tn),
        in_specs=[pl.BlockSpec((tm, D), lambda i, j: (i, 0), pipeline_mode=pl.Buffered(1)),
                  pl.BlockSpec((None, D, tn), lambda i, j: (layer, 0, j)),
                  pl.BlockSpec((None, tm, P), lambda i, j: (layer, i, 0)),
                  pl.BlockSpec((None, P, tn), lambda i, j: (layer, 0, j)),
                  pl.BlockSpec((tm, tn), lambda i, j: (i, j))],
        out_specs=pl.BlockSpec((tm, tn), lambda i, j: (i, j)),
        out_shape=jax.ShapeDtypeStruct((T, D), F32),
        compiler_params=_cparams(("parallel", "parallel")),
        name="ple",
    )(a, w_gate, p3d, w_proj, x2d)


def _retention_kernel(q_ref, k_ref, v_ref, g_ref, cos_ref, sin_ref, intra_ref, qd_ref, kd_ref, cd_ref, gn_ref,
                      o_ref, state_ref):
    S = q_ref.shape[0]
    C = intra_ref.shape[1]
    half = q_ref.shape[1] // 2
    state_ref[...] = jnp.zeros_like(state_ref)
    intra = intra_ref[0]
    qd, kd, cd, gn = qd_ref[0], kd_ref[0], cd_ref[0], gn_ref[0]

    def rot(x, cos, sin):
        x1, x2 = x[:, :half], x[:, half:]
        return jnp.concatenate([x1 * cos - x2 * sin, x1 * sin + x2 * cos], axis=1)

    def chunk(c, carry):
        rows = pl.ds(pl.multiple_of(c * C, C), C)
        cos, sin = cos_ref[rows, :], sin_ref[rows, :]
        qc = rot(q_ref[rows, :], cos, sin)
        kc = rot(k_ref[rows, :], cos, sin) * (RET_QK_DIM ** -0.5)
        vc = v_ref[rows, :].astype(BF16)
        att = lax.dot_general(qc.astype(BF16), kc.astype(BF16), _NT, preferred_element_type=F32) * intra
        inner = jnp.dot(att.astype(BF16), vc, preferred_element_type=F32)
        state = state_ref[...]
        cross = jnp.dot((qc * qd).astype(BF16), state.astype(BF16), preferred_element_type=F32)
        kt = jnp.transpose(kc * kd).astype(BF16)
        state_ref[...] = state * cd + jnp.dot(kt, vc, preferred_element_type=F32)
        o = inner + cross
        o = o * lax.rsqrt(jnp.mean(o * o, axis=-1, keepdims=True) + RMS_EPS) * gn
        g = g_ref[rows, :]
        o_ref[rows, :] = (g * jax.nn.sigmoid(g) * o).astype(o_ref.dtype)
        return carry

    lax.fori_loop(0, S // C, chunk, 0)


def retention_core(proj, cos, sin, out_norm, B, S, n_heads):
    T = proj.shape[0]
    dk = RET_QK_DIM
    dv = 2 * dk
    C = RET_CHUNK
    log_gamma = jnp.log1p(-jnp.exp2(-5.0 - jnp.arange(n_heads, dtype=F32)))
    i = jnp.arange(C, dtype=F32)
    dist = i[:, None] - i[None, :]
    intra = jnp.where(dist[None] >= 0, jnp.exp(log_gamma[:, None, None] * jnp.maximum(dist, 0.0)[None]), 0.0)
    q_decay = jnp.broadcast_to(jnp.exp(log_gamma[:, None] * (i[None, :] + 1.0))[:, :, None], (n_heads, C, dk))
    k_decay = jnp.broadcast_to(jnp.exp(log_gamma[:, None] * (C - 1.0 - i)[None, :])[:, :, None], (n_heads, C, dk))
    chunk_decay = jnp.broadcast_to(jnp.exp(log_gamma * C)[:, None, None], (n_heads, 1, dv))
    nq = n_heads
    return pl.pallas_call(
        _retention_kernel,
        grid=(B, n_heads),
        in_specs=[pl.BlockSpec((S, dk), lambda b, h: (b, h)),
                  pl.BlockSpec((S, dk), lambda b, h: (b, nq + h)),
                  pl.BlockSpec((S, dv), lambda b, h: (b, nq + h)),
                  pl.BlockSpec((S, dv), lambda b, h: (b, 2 * nq + h)),
                  pl.BlockSpec((S, dk // 2), lambda b, h: (b, 0)),
                  pl.BlockSpec((S, dk // 2), lambda b, h: (b, 0)),
                  pl.BlockSpec((1, C, C), lambda b, h: (h, 0, 0)),
                  pl.BlockSpec((1, C, dk), lambda b, h: (h, 0, 0)),
                  pl.BlockSpec((1, C, dk), lambda b, h: (h, 0, 0)),
                  pl.BlockSpec((1, 1, dv), lambda b, h: (h, 0, 0)),
                  pl.BlockSpec((1, 1, dv), lambda b, h: (h, 0, 0))],
        out_specs=pl.BlockSpec((S, dv), lambda b, h: (b, h)),
        out_shape=jax.ShapeDtypeStruct((T, n_heads * dv), BF16),
        scratch_shapes=[pltpu.VMEM((dk, dv), F32)],
        compiler_params=_cparams(("parallel", "arbitrary")),
        name="retention_core",
    )(proj, proj, proj, proj, cos, sin, intra, q_decay, k_decay, chunk_decay,
      out_norm.reshape(n_heads, 1, dv))


def _rot_tables(positions, dim, theta):
    inv = theta ** (-jnp.arange(0, dim, 2, dtype=F32) / dim)
    ang = positions.astype(F32).reshape(-1, 1) * inv[None, :]
    return jnp.cos(ang), jnp.sin(ang)


def dsa_layer(x2d, positions, gain, w_in, w_out, q_norm, k_norm, idx_k_norm, layer, B, S):
    T, D = x2d.shape
    n_heads = D // HEAD_DIM
    n_kv = n_heads // 4
    rot_dim = HEAD_DIM // 4
    cos, sin = _rot_tables(positions, rot_dim, ROPE_THETA)
    ones = jnp.ones((T, HEAD_DIM - rot_dim), F32)
    zeros_r = jnp.zeros((T, HEAD_DIM - rot_dim // 2), F32)
    tabs = (jnp.concatenate([cos, cos, ones], axis=1),
            jnp.concatenate([-sin, zeros_r], axis=1),
            jnp.concatenate([jnp.zeros_like(sin), sin, jnp.zeros((T, HEAD_DIM - rot_dim), F32)], axis=1))
    hn = rmsnorm_cast(x2d, gain)
    heads, wi = dsa_in_proj(hn, w_in, layer, tabs, q_norm, k_norm, idx_k_norm, n_heads, n_kv)
    top_k = min(IDX_TOPK_MAX, S // 4)
    o = dsa_attention_core(heads, jnp.transpose(wi[:, :IDX_HEADS]), n_heads, n_kv, B, S, top_k)
    return matmul(o, w_out, layer, res=x2d, head_major=True, tm=2048, tn=512, name="dsa_out_proj")


def retention_layer(x2d, positions, gain, w_in, w_out, out_norm, layer, B, S):
    T, D = x2d.shape
    n_heads = D // RET_QK_DIM
    cos, sin = _rot_tables(positions, RET_QK_DIM, RET_THETA)
    hn = rmsnorm_cast(x2d, gain)
    proj = matmul(hn, w_in, layer, tm=2048, tn=512, name="ret_in_proj")
    og = retention_core(proj, cos, sin, out_norm, B, S, n_heads)
    return matmul(og, w_out, layer, res=x2d, tm=1024, tn=256, name="ret_out_proj")


def kernel(x, p, positions, attn_norm, ffn_norm, ple_norm, a_w_in, a_w_out, a_q_norm, a_k_norm, a_idx_k_norm,
           b_w_in, b_w_out, b_out_norm, moe_w_group, moe_b_group, moe_w_expert, moe_b_expert, moe_w_gate,
           moe_w_up, moe_w_down, ple_proj, ple_gate):
    B, S, D = x.shape
    depth = p.shape[0]
    T = B * S
    x2d = x.reshape(T, D)
    p3d = p.reshape(depth, T, p.shape[-1])
    for i in range(depth):
        j = i // 2
        if i % 2 == 0:
            x2d = dsa_layer(x2d, positions, attn_norm[i], a_w_in, a_w_out, a_q_norm[j], a_k_norm[j],
                            a_idx_k_norm[j], j, B, S)
        else:
            x2d = retention_layer(x2d, positions, attn_norm[i], b_w_in, b_w_out, b_out_norm[j], j, B, S)
        x2d = hier_moe_residual(x2d, ffn_norm[i], moe_w_group[i], moe_b_group[i], moe_w_expert[i],
                                moe_b_expert[i], moe_w_gate, moe_w_up, moe_w_down, i)
        x2d = ple_residual(x2d, ple_norm[i], ple_gate, p3d, ple_proj, i)
    return x2d.reshape(B, S, D)
```

```python
import functools

import jax
import jax.numpy as jnp
from jax import lax
from jax.experimental import pallas as pl
from jax.experimental.pallas import tpu as pltpu

F32 = jnp.float32
BF16 = jnp.bfloat16
I32 = jnp.int32

RMS_EPS = 1e-6
HEAD_DIM = 128
IDX_HEADS = 64
IDX_TOPK_MAX = 256
ROPE_THETA = 500000.0
RET_QK_DIM = 256
RET_THETA = 10000.0
RET_CHUNK = 128
N_GROUPS = 8
EXPERTS_PER_GROUP = 8
LANES = 128
VMEM_LIMIT_BYTES = 56 * 1024 * 1024
NEG_BIG = -1e30
INT_MIN = -2147483648


def _cparams(sem):
    return pltpu.CompilerParams(dimension_semantics=sem, vmem_limit_bytes=VMEM_LIMIT_BYTES)


def _pick(n, cands):
    for c in cands:
        if n % c == 0:
            return c
    raise ValueError(f"no tile for {n} in {cands}")


def _rmsnorm_kernel(x_ref, g_ref, o_ref):
    x = x_ref[...]
    ms = jnp.mean(x * x, axis=-1, keepdims=True)
    o_ref[...] = (x * lax.rsqrt(ms + RMS_EPS) * g_ref[...]).astype(o_ref.dtype)


def rmsnorm_cast(x2d, gain):
    T, D = x2d.shape
    tm = _pick(T, (512, 256, 128, 8))
    return pl.pallas_call(
        _rmsnorm_kernel,
        grid=(T // tm,),
        in_specs=[pl.BlockSpec((tm, D), lambda i: (i, 0)), pl.BlockSpec((1, D), lambda i: (0, 0))],
        out_specs=pl.BlockSpec((tm, D), lambda i: (i, 0)),
        out_shape=jax.ShapeDtypeStruct((T, D), BF16),
        compiler_params=_cparams(("parallel",)),
        name="rmsnorm_cast",
    )(x2d, gain.reshape(1, D))


def _mm_kernel(*refs, nk, head_major, has_res, n_valid):
    a_ref, w_ref = refs[0], refs[1]
    r_ref = refs[2] if has_res else None
    o_ref = refs[2 + has_res]
    acc_ref = refs[3 + has_res] if nk > 1 else None
    if head_major:
        a = jnp.concatenate([a_ref[h] for h in range(a_ref.shape[0])], axis=1)
    else:
        a = a_ref[...]
    w = w_ref[...]
    if n_valid % w.shape[1]:
        col = pl.program_id(1) * w.shape[1] + lax.broadcasted_iota(I32, (1, w.shape[1]), 1)
        w = jnp.where(col < n_valid, w, 0.0)
    part = jnp.dot(a, w.astype(BF16), preferred_element_type=F32)

    def finish(acc):
        if has_res:
            acc = acc + r_ref[...]
        o_ref[...] = acc.astype(o_ref.dtype)

    if nk == 1:
        finish(part)
    else:
        k = pl.program_id(2)

        @pl.when(k == 0)
        def _():
            acc_ref[...] = part

        @pl.when(k > 0)
        def _():
            acc_ref[...] += part

        @pl.when(k == nk - 1)
        def _():
            finish(acc_ref[...])


def matmul(a, w, layer, *, n_out=None, res=None, out_dtype=F32, tm=1024, tn=512, tk=None, head_major=False,
           name="matmul"):
    if head_major:
        nh, M, hd = a.shape
        K = nh * hd
    else:
        M, K = a.shape
    N = w.shape[2]
    n_out = N if n_out is None else n_out
    tm = min(tm, M)
    tk = K if (tk is None or head_major) else min(tk, K)
    assert M % tm == 0 and K % tk == 0 and n_out % tn == 0
    nk = K // tk
    a_mode = dict(pipeline_mode=pl.Buffered(1)) if nk == 1 else {}
    if head_major:
        a_spec = pl.BlockSpec((nh, tm, hd), lambda i, j, k: (0, i, 0), **a_mode)
    else:
        a_spec = pl.BlockSpec((tm, tk), lambda i, j, k: (i, k), **a_mode)
    in_specs = [a_spec, pl.BlockSpec((None, tk, tn), lambda i, j, k: (layer, k, j))]
    args = [a, w]
    if res is not None:
        in_specs.append(pl.BlockSpec((tm, tn), lambda i, j, k: (i, j)))
        args.append(res)
    return pl.pallas_call(
        functools.partial(_mm_kernel, nk=nk, head_major=head_major, has_res=res is not None, n_valid=N),
        grid=(M // tm, n_out // tn, nk),
        in_specs=in_specs,
        out_specs=pl.BlockSpec((tm, tn), lambda i, j, k: (i, j)),
        out_shape=jax.ShapeDtypeStruct((M, n_out), out_dtype),
        scratch_shapes=[pltpu.VMEM((tm, tn), F32)] if nk > 1 else [],
        compiler_params=_cparams(("parallel", "parallel", "arbitrary")),
        name=name,
    )(*args)


def _rot_head(x, c, s1, s2):
    half = HEAD_DIM // 8
    return x * c + pltpu.roll(x, HEAD_DIM - half, 1) * s1 + pltpu.roll(x, half, 1) * s2


def _flag(pred, shape):
    return jnp.broadcast_to(pred.astype(I32), shape) == 1


def _dsa_proj_kernel(a_ref, w_ref, c_ref, s1_ref, s2_ref, gains_ref, heads_ref, wi_ref, *, n_valid, bounds,
                     wi_lane, wscale, row_chunk):
    j = pl.program_id(1)
    tm = a_ref.shape[0]
    tn = w_ref.shape[1]
    b_q, b_k, b_v, b_qi = bounds
    col = j * tn + lax.broadcasted_iota(I32, (1, tn), 1)
    w = jnp.where(col < n_valid, w_ref[...], 0.0).astype(BF16)
    is_q = j < b_q
    is_k = jnp.logical_and(j >= b_q, j < b_k)
    is_v = jnp.logical_and(j >= b_k, j < b_v)
    is_ki = j == b_qi
    lane = (1, HEAD_DIM)
    gain = jnp.where(_flag(is_q, lane), gains_ref[0:1, :],
                     jnp.where(_flag(is_k, lane), gains_ref[1:2, :],
                               jnp.where(_flag(is_ki, lane), gains_ref[2:3, :], 1.0)))
    use_norm = _flag(jnp.logical_or(jnp.logical_or(is_q, is_k), is_ki), (row_chunk, 1))
    no_rot = _flag(is_v, (row_chunk, HEAD_DIM))
    for rc in range(tm // row_chunk):
        rows = pl.ds(rc * row_chunk, row_chunk)
        acc = jnp.dot(a_ref[rows, :], w, preferred_element_type=F32)
        c = jnp.where(no_rot, 1.0, c_ref[rows, :])
        s1 = jnp.where(no_rot, 0.0, s1_ref[rows, :])
        s2 = jnp.where(no_rot, 0.0, s2_ref[rows, :])
        for hh in range(tn // HEAD_DIM):
            x = acc[:, hh * HEAD_DIM:(hh + 1) * HEAD_DIM]
            ms = jnp.mean(x * x, axis=-1, keepdims=True)
            x = x * jnp.where(use_norm, lax.rsqrt(ms + RMS_EPS), 1.0) * gain
            heads_ref[hh, rows, :] = _rot_head(x, c, s1, s2).astype(heads_ref.dtype)
        wi_ref[rows, :] = acc[:, wi_lane:wi_lane + HEAD_DIM] * wscale


def dsa_in_proj(hn, w_in, layer, tabs, q_gain, k_gain, ki_gain, n_heads, n_kv):
    T, K = hn.shape
    n_cols = w_in.shape[2]
    seg_ends = [n_heads * HEAD_DIM]
    for width in (n_kv * HEAD_DIM, n_kv * HEAD_DIM, IDX_HEADS * HEAD_DIM):
        seg_ends.append(seg_ends[-1] + width)
    tn = next(t for t in (512, 256, 128) if all(e % t == 0 for e in seg_ends))
    hb = tn // HEAD_DIM
    bounds = tuple(e // tn for e in seg_ends)
    b_v, b_qi = bounds[2], bounds[3]
    n_out = -(-n_cols // tn) * tn
    wi_lane = (seg_ends[-1] + HEAD_DIM) % tn
    assert n_cols == seg_ends[-1] + HEAD_DIM + IDX_HEADS and (n_out - tn) <= seg_ends[-1] + HEAD_DIM
    tm = min(2048, T)
    row_chunk = min(512, tm)
    c, s1, s2 = tabs
    gains = jnp.zeros((8, HEAD_DIM), F32).at[0].set(q_gain * (HEAD_DIM ** -0.5 * LOG2_E)).at[1].set(k_gain)
    gains = gains.at[2].set(ki_gain)
    once = dict(pipeline_mode=pl.Buffered(1))
    row = lambda i, j: (i, 0)

    def head_block(i, j):
        pos = jnp.where(j < b_v, j + (b_qi - b_v), jnp.where(j < b_qi, j - b_v, j))
        return (pos, i, 0)

    return pl.pallas_call(
        functools.partial(_dsa_proj_kernel, n_valid=n_cols, bounds=bounds, wi_lane=wi_lane,
                          wscale=IDX_HEADS ** -0.5 * HEAD_DIM ** -0.5, row_chunk=row_chunk),
        grid=(T // tm, n_out // tn),
        in_specs=[pl.BlockSpec((tm, K), row, **once),
                  pl.BlockSpec((None, K, tn), lambda i, j: (layer, 0, j)),
                  pl.BlockSpec((tm, HEAD_DIM), row, **once), pl.BlockSpec((tm, HEAD_DIM), row, **once),
                  pl.BlockSpec((tm, HEAD_DIM), row, **once), pl.BlockSpec((8, HEAD_DIM), lambda i, j: (0, 0))],
        out_specs=[pl.BlockSpec((hb, tm, HEAD_DIM), head_block), pl.BlockSpec((tm, HEAD_DIM), row)],
        out_shape=[jax.ShapeDtypeStruct((n_out // HEAD_DIM, T, HEAD_DIM), BF16),
                   jax.ShapeDtypeStruct((T, HEAD_DIM), F32)],
        compiler_params=_cparams(("parallel", "arbitrary")),
        name="dsa_in_proj",
    )(hn, w_in, c, s1, s2, gains)


_NT = (((1,), (1,)), ((), ()))


IDX_UNROLL = 4
LOG2_E = 1.4426950408889634


def _dsa_tile(qi_ref, ki_ref, wi_ref, q_ref, k_ref, v_ref, o_ref, sc_ref, u_ref, bias_ref,
              *, top_k, sub, steps_per_kv, nk, q0):
    tq = sc_ref.shape[1]
    keys = pl.ds(0, nk)

    def idx_term(h):
        s = lax.dot_general(ki_ref[keys, :], qi_ref[h], _NT, preferred_element_type=F32)
        return jnp.maximum(s, 0.0) * wi_ref[pl.ds(h, 1), :]

    def idx_terms(h0):
        acc = idx_term(h0)
        for d in range(1, IDX_UNROLL):
            acc = acc + idx_term(h0 + d)
        return acc

    sc_ref[keys, :] = idx_terms(0)

    def idx_body(g, carry):
        sc_ref[keys, :] += idx_terms(g * IDX_UNROLL)
        return carry

    lax.fori_loop(1, IDX_HEADS // IDX_UNROLL, idx_body, 0)

    key_pos = lax.broadcasted_iota(I32, (nk, tq), 0)
    q_pos = q0 + lax.broadcasted_iota(I32, (nk, tq), 1)
    causal = key_pos <= q_pos
    bits = pltpu.bitcast(sc_ref[keys, :], I32)
    u = bits ^ ((bits >> 31) & 0x7FFFFFFF)
    u_ref[keys, :] = jnp.where(causal, u, INT_MIN)

    def count_ge(cand):
        return jnp.sum(jnp.where(u_ref[keys, :] >= cand, 1.0, 0.0), axis=0, keepdims=True)

    zero = jnp.zeros((1, tq), I32)
    v0 = jnp.where(count_ge(zero) >= top_k, zero, jnp.full((1, tq), INT_MIN, I32))

    def bis_body(it, v):
        cand = v + lax.shift_left(jnp.int32(1), 30 - it)
        return jnp.where(count_ge(cand) >= top_k, cand, v)

    thr = lax.fori_loop(0, 31, bis_body, v0)
    keep = jnp.logical_and(u_ref[keys, :] >= thr, causal)
    bias_ref[:, keys] = jnp.transpose(jnp.where(keep, 0.0, NEG_BIG))

    def att_body(st, carry):
        n = st // steps_per_kv
        q = q_ref[pl.ds(st * sub, sub)].reshape(sub * tq, HEAD_DIM)
        lg = lax.dot_general(q, k_ref[n, keys, :], _NT, preferred_element_type=F32)
        ps, ls = [], []
        for g in range(sub):
            x = lg[g * tq:(g + 1) * tq] + bias_ref[:, keys]
            p = jnp.exp2(x - jnp.max(x, axis=-1, keepdims=True))
            ls.append(jnp.sum(p, axis=-1, keepdims=True))
            ps.append(p.astype(BF16))
        o = jnp.dot(jnp.concatenate(ps, axis=0), v_ref[n, keys, :], preferred_element_type=F32)
        for g in range(sub):
            o_ref[st * sub + g] = (o[g * tq:(g + 1) * tq] / ls[g]).astype(o_ref.dtype)
        return carry

    lax.fori_loop(0, q_ref.shape[0] // sub, att_body, 0)


def _dsa_kernel(*refs, **static):
    S, tq = refs[7].shape
    i = pl.program_id(1)
    for c in range(S // tq):
        @pl.when(i == c)
        def _(c=c):
            _dsa_tile(*refs, nk=(c + 1) * tq, q0=c * tq, **static)


def dsa_attention_core(heads, wi, n_heads, n_kv, B, S, top_k):
    T = heads.shape[1]
    tq = _pick(S, (256, 128))
    nq = S // tq
    group = n_heads // n_kv
    sub = 2 if group % 2 == 0 else 1
    off_q, off_k = IDX_HEADS, IDX_HEADS + n_heads
    off_v, off_ki = off_k + n_kv, off_k + 2 * n_kv
    assert off_q % n_heads == 0 and off_k % n_kv == 0 and off_v % n_kv == 0
    once = dict(pipeline_mode=pl.Buffered(1))
    return pl.pallas_call(
        functools.partial(_dsa_kernel, top_k=top_k, sub=sub, steps_per_kv=group // sub),
        grid=(B, nq),
        in_specs=[pl.BlockSpec((IDX_HEADS, tq, HEAD_DIM), lambda b, i: (0, b * nq + i, 0)),
                  pl.BlockSpec((None, S, HEAD_DIM), lambda b, i: (off_ki, b, 0), **once),
                  pl.BlockSpec((IDX_HEADS, tq), lambda b, i: (0, b * nq + i)),
                  pl.BlockSpec((n_heads, tq, HEAD_DIM), lambda b, i: (off_q // n_heads, b * nq + i, 0)),
                  pl.BlockSpec((n_kv, S, HEAD_DIM), lambda b, i: (off_k // n_kv, b, 0), **once),
                  pl.BlockSpec((n_kv, S, HEAD_DIM), lambda b, i: (off_v // n_kv, b, 0), **once)],
        out_specs=pl.BlockSpec((n_heads, tq, HEAD_DIM), lambda b, i: (0, b * nq + i, 0)),
        out_shape=jax.ShapeDtypeStruct((n_heads, T, HEAD_DIM), BF16),
        scratch_shapes=[pltpu.VMEM((S, tq), F32), pltpu.VMEM((S, tq), I32), pltpu.VMEM((tq, S), F32)],
        compiler_params=_cparams(("parallel", "arbitrary")),
        name="dsa_core",
    )(heads, heads, wi, heads, heads, heads)


def _route_kernel(x_ref, g_ref, w_ref, b_ref, o_ref):
    x = x_ref[...]
    ms = jnp.mean(x * x, axis=-1, keepdims=True)
    t = x * lax.rsqrt(ms + RMS_EPS) * g_ref[...]
    logits = jnp.dot(t, w_ref[...], precision=lax.Precision.HIGHEST, preferred_element_type=F32) + b_ref[...]
    lane = lax.broadcasted_iota(I32, logits.shape, 1)
    lane_f = lane.astype(F32)
    big = float(LANES)
    gl = jnp.where(lane < N_GROUPS, logits, -jnp.inf)
    gmax = jnp.max(gl, axis=-1, keepdims=True)
    g_sel = jnp.min(jnp.where(gl == gmax, lane_f, big), axis=-1, keepdims=True)
    g_w = 1.0 / jnp.sum(jnp.exp(gl - gmax), axis=-1, keepdims=True)
    lo = N_GROUPS + g_sel * EXPERTS_PER_GROUP
    in_grp = jnp.logical_and(lane_f >= lo, lane_f < lo + EXPERTS_PER_GROUP)
    el = jnp.where(in_grp, logits, -jnp.inf)
    m1 = jnp.max(el, axis=-1, keepdims=True)
    i1 = jnp.min(jnp.where(el == m1, lane_f, big), axis=-1, keepdims=True)
    el2 = jnp.where(lane_f == i1, -jnp.inf, el)
    m2 = jnp.max(el2, axis=-1, keepdims=True)
    i2 = jnp.min(jnp.where(el2 == m2, lane_f, big), axis=-1, keepdims=True)
    z = jnp.sum(jnp.exp(el - m1), axis=-1, keepdims=True)
    p1 = 1.0 / z
    p2 = jnp.exp(m2 - m1) / z
    w1 = p1 / (p1 + p2) * g_w
    w2 = p2 / (p1 + p2) * g_w
    out = jnp.where(lane == 0, i1 - N_GROUPS, jnp.where(lane == 1, i2 - N_GROUPS,
          jnp.where(lane == 2, w1, jnp.where(lane == 3, w2, 0.0))))
    o_ref[...] = out


def moe_route(x2d, gain, w_group, b_group, w_expert, b_expert):
    T, D = x2d.shape
    n_log = w_group.shape[1] + w_expert.shape[1]
    w = jnp.concatenate([w_group, w_expert, jnp.zeros((D, LANES - n_log), F32)], axis=1)
    b = jnp.concatenate([b_group, b_expert, jnp.zeros((LANES - n_log,), F32)]).reshape(1, LANES)
    tm = _pick(T, (256, 128, 8))
    return pl.pallas_call(
        _route_kernel,
        grid=(T // tm,),
        in_specs=[pl.BlockSpec((tm, D), lambda i: (i, 0)), pl.BlockSpec((1, D), lambda i: (0, 0)),
                  pl.BlockSpec((D, LANES), lambda i: (0, 0)), pl.BlockSpec((1, LANES), lambda i: (0, 0))],
        out_specs=pl.BlockSpec((tm, LANES), lambda i: (i, 0)),
        out_shape=jax.ShapeDtypeStruct((T, LANES), F32),
        compiler_params=_cparams(("parallel",)),
        name="moe_route",
    )(x2d, gain.reshape(1, D), w, b)


def _row_copy(src_hbm, dst_ref, sem, src_row, dst_row):
    return pltpu.make_async_copy(src_hbm.at[pl.ds(src_row, 1), :], dst_ref.at[pl.ds(dst_row, 1), :], sem)


DMA_PRIORITIES = 2


def _start_row_gather(idx_ref, src_hbm, dst_ref, sem, base, n_rows):
    def start(g, c):
        for pr in range(DMA_PRIORITIES):
            r = g * DMA_PRIORITIES + pr
            _row_copy(src_hbm, dst_ref, sem, idx_ref[base + r], r).start(priority=pr)
        return c

    lax.fori_loop(0, n_rows // DMA_PRIORITIES, start, 0)


def _wait_row_gather(src_hbm, dst_ref, sem, n_rows):
    pltpu.make_async_copy(src_hbm.at[pl.ds(0, n_rows), :], dst_ref, sem).wait()


FFN_SLOTS = 3


def _ffn_kernel(src_ref, te_ref, nu_ref, x_hbm, gate_ref, g_ref, wg_ref, wu_ref, wd_ref, o_ref, xbuf, sem):
    j = pl.program_id(0)
    tm = o_ref.shape[0]
    n_used = nu_ref[0]
    ahead = FFN_SLOTS - 1

    def fetch(tile, slot):
        _start_row_gather(src_ref, x_hbm, xbuf.at[slot], sem.at[slot], tile * tm, tm)

    for t in range(ahead):
        @pl.when(jnp.logical_and(j == 0, t < n_used))
        def _(t=t):
            fetch(t, t)

    @pl.when(j + ahead < n_used)
    def _():
        fetch(j + ahead, lax.rem(j + ahead, FFN_SLOTS))

    @pl.when(j < n_used)
    def _():
        slot = lax.rem(j, FFN_SLOTS)
        _wait_row_gather(x_hbm, xbuf.at[slot], sem.at[slot], tm)
        x = xbuf[slot]
        ms = jnp.mean(x * x, axis=-1, keepdims=True)
        t = (x * lax.rsqrt(ms + RMS_EPS) * g_ref[...]).astype(BF16)
        a = jnp.dot(t, wg_ref[0].astype(BF16), preferred_element_type=F32)
        u = jnp.dot(t, wu_ref[0].astype(BF16), preferred_element_type=F32)
        hid = a * jax.nn.sigmoid(a) * u * gate_ref[...]
        o_ref[...] = jnp.dot(hid.astype(BF16), wd_ref[0].astype(BF16), preferred_element_type=F32)

    @pl.when(j >= n_used)
    def _():
        o_ref[...] = jnp.zeros_like(o_ref)


def moe_ffn(x2d, src_tok, gate_s, gain, w_gate, w_up, w_down, tile_expert, n_used, tm):
    D = x2d.shape[1]
    P = src_tok.shape[0]
    F = w_gate.shape[-1]
    assert P % tm == 0 and tm % DMA_PRIORITIES == 0
    wmap = lambda j, src, te, nu: (te[j], 0, 0)
    return pl.pallas_call(
        _ffn_kernel,
        grid_spec=pltpu.PrefetchScalarGridSpec(
            num_scalar_prefetch=3, grid=(P // tm,),
            in_specs=[pl.BlockSpec(memory_space=pl.ANY),
                      pl.BlockSpec((tm, 1), lambda j, src, te, nu: (j, 0)),
                      pl.BlockSpec((1, D), lambda j, src, te, nu: (0, 0)),
                      pl.BlockSpec((1, D, F), wmap), pl.BlockSpec((1, D, F), wmap),
                      pl.BlockSpec((1, F, D), wmap)],
            out_specs=pl.BlockSpec((tm, D), lambda j, src, te, nu: (j, 0)),
            scratch_shapes=[pltpu.VMEM((FFN_SLOTS, tm, D), F32), pltpu.SemaphoreType.DMA((FFN_SLOTS,))]),
        out_shape=jax.ShapeDtypeStruct((P, D), F32),
        compiler_params=_cparams(("arbitrary",)),
        name="moe_ffn",
    )(src_tok, tile_expert, n_used, x2d, gate_s, gain.reshape(1, D), w_gate, w_up, w_down)


def _combine_kernel(p1_ref, p2_ref, ys_hbm, x_ref, o_ref, buf, sem):
    i = pl.program_id(0)
    R = x_ref.shape[0]

    def fetch(tile, slot):
        _start_row_gather(p1_ref, ys_hbm, buf.at[2 * slot], sem.at[2 * slot], tile * R, R)
        _start_row_gather(p2_ref, ys_hbm, buf.at[2 * slot + 1], sem.at[2 * slot + 1], tile * R, R)

    @pl.when(i == 0)
    def _():
        fetch(0, 0)

    @pl.when(i + 1 < pl.num_programs(0))
    def _():
        fetch(i + 1, lax.rem(i + 1, 2))

    slot = lax.rem(i, 2)
    _wait_row_gather(ys_hbm, buf.at[2 * slot], sem.at[2 * slot], R)
    _wait_row_gather(ys_hbm, buf.at[2 * slot + 1], sem.at[2 * slot + 1], R)
    o_ref[...] = x_ref[...] + (buf[2 * slot] + buf[2 * slot + 1])


def moe_combine(x2d, ys, pos1, pos2, rows_per_step):
    T, D = x2d.shape
    R = rows_per_step
    return pl.pallas_call(
        _combine_kernel,
        grid_spec=pltpu.PrefetchScalarGridSpec(
            num_scalar_prefetch=2, grid=(T // R,),
            in_specs=[pl.BlockSpec(memory_space=pl.ANY),
                      pl.BlockSpec((R, D), lambda i, a, b: (i, 0))],
            out_specs=pl.BlockSpec((R, D), lambda i, a, b: (i, 0)),
            scratch_shapes=[pltpu.VMEM((4, R, D), F32), pltpu.SemaphoreType.DMA((4,))]),
        out_shape=jax.ShapeDtypeStruct((T, D), F32),
        compiler_params=_cparams(("arbitrary",)),
        name="moe_combine",
    )(pos1, pos2, ys, x2d)


def hier_moe_residual(x2d, gain, w_group, b_group, w_expert, b_expert, w_gate, w_up, w_down, layer):
    T, D = x2d.shape
    depth, G, E = w_gate.shape[0], w_gate.shape[1], w_gate.shape[2]
    n_exp = G * E
    F = w_gate.shape[-1]
    route = moe_route(x2d, gain, w_group, b_group, w_expert, b_expert)
    eid = route[:, 0:2].astype(I32).reshape(2 * T)
    wts = route[:, 2:4].reshape(2 * T)
    tm = 256 if 2 * T >= 4096 else 128
    nt = (2 * T) // tm + n_exp
    onehot = (eid[None, :] == jnp.arange(n_exp, dtype=I32)[:, None]).astype(I32)
    csum = jnp.cumsum(onehot, axis=1)
    counts = csum[:, -1]
    rank = jnp.sum(onehot * csum, axis=0) - 1
    tiles = (counts + tm - 1) // tm
    tile_end = jnp.cumsum(tiles)
    tile_start = tile_end - tiles
    n_used = tile_end[-1:]
    j = jnp.arange(nt, dtype=I32)
    tile_expert = jnp.minimum(jnp.sum(tile_end[None, :] <= j[:, None], axis=1, dtype=I32), n_exp - 1)
    last_expert = jnp.sum(tile_end < n_used[0], dtype=I32)
    tile_expert = jnp.where(j < n_used[0], tile_expert, last_expert)
    pos = (tile_start[eid] * tm + rank).astype(I32)
    inv = jnp.full((nt * tm,), -1, I32).at[pos].set(jnp.arange(2 * T, dtype=I32))
    valid = inv >= 0
    src_pair = jnp.maximum(inv, 0)
    src_tok = jnp.where(valid, src_pair // 2, 0)
    gate_s = jnp.where(valid, wts[src_pair], 0.0).reshape(nt * tm, 1)
    pos = pos.reshape(T, 2)

    ys = moe_ffn(x2d, src_tok, gate_s, gain, w_gate.reshape(depth * n_exp, D, F),
                 w_up.reshape(depth * n_exp, D, F), w_down.reshape(depth * n_exp, F, D),
                 tile_expert + layer * n_exp, n_used, tm)
    return moe_combine(x2d, ys, pos[:, 0], pos[:, 1], min(256, T))


def _ple_kernel(a_ref, w_ref, p_ref, pw_ref, x_ref, o_ref):
    z = jnp.dot(a_ref[...], w_ref[...].astype(BF16), preferred_element_type=F32)
    e = jnp.dot(p_ref[...].astype(BF16), pw_ref[...].astype(BF16), preferred_element_type=F32)
    o_ref[...] = x_ref[...] + jax.nn.sigmoid(z) * e


def ple_residual(x2d, gain, w_gate, p3d, w_proj, layer):
    T, D = x2d.shape
    P = p3d.shape[2]
    a = rmsnorm_cast(x2d, gain)
    tm, tn = min(1024, T), 512
    return pl.pallas_call(
        _ple_kernel,
        grid=(T // tm, D // tn),
        in_specs=[pl.BlockSpec((tm, D), lambda i, j: (i, 0), pipeline_mode=pl.Buffered(1)),
                  pl.BlockSpec((None, D, tn), lambda i, j: (layer, 0, j)),
                  pl.BlockSpec((None, tm, P), lambda i, j: (layer, i, 0)),
                  pl.BlockSpec((None, P, tn), lambda i, j: (layer, 0, j)),
                  pl.BlockSpec((tm, tn), lambda i, j: (i, j))],
        out_specs=pl.BlockSpec((tm, tn), lambda i, j: (i, j)),
        out_shape=jax.ShapeDtypeStruct((T, D), F32),
        compiler_params=_cparams(("parallel", "parallel")),
        name="ple",
    )(a, w_gate, p3d, w_proj, x2d)


def _ret_proj_kernel(a_ref, w_ref, cos_ref, sin_ref, qkv_ref, g_ref, *, n_qk, n_qkv, row_chunk):
    j = pl.program_id(1)
    tm = a_ref.shape[0]
    tn = w_ref.shape[1]
    half = RET_QK_DIM // 2
    w = w_ref[...].astype(BF16)
    shape = (row_chunk, half)
    is_rot = _flag(j < n_qk, shape)
    kscale = jnp.where(_flag(jnp.logical_and(j >= n_qk // 2, j < n_qk), shape), RET_QK_DIM ** -0.5, 1.0)
    for rc in range(tm // row_chunk):
        rows = pl.ds(rc * row_chunk, row_chunk)
        acc = jnp.dot(a_ref[rows, :], w, preferred_element_type=F32)

        @pl.when(j < n_qkv)
        def _():
            cos = jnp.where(is_rot, cos_ref[rows, :], 1.0) * kscale
            sin = jnp.where(is_rot, sin_ref[rows, :], 0.0) * kscale
            for hh in range(tn // RET_QK_DIM):
                x1 = acc[:, hh * RET_QK_DIM:hh * RET_QK_DIM + half]
                x2 = acc[:, hh * RET_QK_DIM + half:(hh + 1) * RET_QK_DIM]
                qkv_ref[rows, hh * RET_QK_DIM:hh * RET_QK_DIM + half] = (x1 * cos - x2 * sin).astype(BF16)
                qkv_ref[rows, hh * RET_QK_DIM + half:(hh + 1) * RET_QK_DIM] = (x1 * sin + x2 * cos).astype(BF16)

        @pl.when(j >= n_qkv)
        def _():
            g_ref[rows, :] = acc


def ret_in_proj(hn, w_in, layer, cos, sin, n_heads):
    T, K = hn.shape
    dk = RET_QK_DIM
    qk_cols, v_cols = 2 * n_heads * dk, 2 * n_heads * dk
    tn = 512 if (n_heads * dk) % 512 == 0 else dk
    n_qk, n_qkv = qk_cols // tn, (qk_cols + v_cols) // tn
    n_blocks = w_in.shape[2] // tn
    tm = min(2048, T)
    once = dict(pipeline_mode=pl.Buffered(1))
    row = lambda i, j: (i, 0)
    return pl.pallas_call(
        functools.partial(_ret_proj_kernel, n_qk=n_qk, n_qkv=n_qkv, row_chunk=min(512, tm)),
        grid=(T // tm, n_blocks),
        in_specs=[pl.BlockSpec((tm, K), row, **once),
                  pl.BlockSpec((None, K, tn), lambda i, j: (layer, 0, j)),
                  pl.BlockSpec((tm, dk // 2), row, **once), pl.BlockSpec((tm, dk // 2), row, **once)],
        out_specs=[pl.BlockSpec((tm, tn), lambda i, j: (i, jnp.minimum(j, n_qkv - 1))),
                   pl.BlockSpec((tm, tn), lambda i, j: (i, jnp.maximum(j - n_qkv, 0)))],
        out_shape=[jax.ShapeDtypeStruct((T, qk_cols + v_cols), BF16),
                   jax.ShapeDtypeStruct((T, w_in.shape[2] - qk_cols - v_cols), F32)],
        compiler_params=_cparams(("parallel", "arbitrary")),
        name="ret_in_proj",
    )(hn, w_in, cos, sin)


RET_HEADS_PER_STEP = 2


def _retention_kernel(q_ref, k_ref, v_ref, g_ref, intra_ref, qd_ref, kd_ref, cd_ref, gn_ref, o_ref, state_ref):
    S = q_ref.shape[0]
    C = intra_ref.shape[1]
    dk = RET_QK_DIM
    dv = 2 * dk
    state_ref[...] = jnp.zeros_like(state_ref)

    def chunk(c, carry):
        rows = pl.ds(pl.multiple_of(c * C, C), C)
        for hh in range(RET_HEADS_PER_STEP):
            qc = q_ref[rows, hh * dk:(hh + 1) * dk]
            kc = k_ref[rows, hh * dk:(hh + 1) * dk]
            vc = v_ref[rows, hh * dv:(hh + 1) * dv]
            att = lax.dot_general(qc, kc, _NT, preferred_element_type=F32) * intra_ref[hh]
            inner = jnp.dot(att.astype(BF16), vc, preferred_element_type=F32)
            state = state_ref[hh]
            cross = jnp.dot((qc.astype(F32) * qd_ref[hh]).astype(BF16), state.astype(BF16),
                            preferred_element_type=F32)
            kt = jnp.transpose(kc.astype(F32) * kd_ref[hh]).astype(BF16)
            state_ref[hh] = state * cd_ref[hh] + jnp.dot(kt, vc, preferred_element_type=F32)
            o = inner + cross
            o = o * lax.rsqrt(jnp.mean(o * o, axis=-1, keepdims=True) + RMS_EPS) * gn_ref[hh]
            g = g_ref[rows, hh * dv:(hh + 1) * dv]
            o_ref[rows, hh * dv:(hh + 1) * dv] = (g * jax.nn.sigmoid(g) * o).astype(o_ref.dtype)
        return carry

    lax.fori_loop(0, S // C, chunk, 0)


def retention_core(qkv, g, out_norm, B, S, n_heads):
    T = qkv.shape[0]
    dk = RET_QK_DIM
    dv = 2 * dk
    C = RET_CHUNK
    hs = RET_HEADS_PER_STEP
    assert n_heads % hs == 0
    log_gamma = jnp.log1p(-jnp.exp2(-5.0 - jnp.arange(n_heads, dtype=F32)))
    i = jnp.arange(C, dtype=F32)
    dist = i[:, None] - i[None, :]
    intra = jnp.where(dist[None] >= 0, jnp.exp(log_gamma[:, None, None] * jnp.maximum(dist, 0.0)[None]), 0.0)
    q_decay = jnp.broadcast_to(jnp.exp(log_gamma[:, None] * (i[None, :] + 1.0))[:, :, None], (n_heads, C, dk))
    k_decay = jnp.broadcast_to(jnp.exp(log_gamma[:, None] * (C - 1.0 - i)[None, :])[:, :, None], (n_heads, C, dk))
    chunk_decay = jnp.broadcast_to(jnp.exp(log_gamma * C)[:, None, None], (n_heads, 1, dv))
    nq = n_heads // hs
    per_head = lambda b, h: (h, 0, 0)
    return pl.pallas_call(
        _retention_kernel,
        grid=(B, nq),
        in_specs=[pl.BlockSpec((S, hs * dk), lambda b, h: (b, h)),
                  pl.BlockSpec((S, hs * dk), lambda b, h: (b, nq + h)),
                  pl.BlockSpec((S, hs * dv), lambda b, h: (b, nq + h)),
                  pl.BlockSpec((S, hs * dv), lambda b, h: (b, h)),
                  pl.BlockSpec((hs, C, C), per_head), pl.BlockSpec((hs, C, dk), per_head),
                  pl.BlockSpec((hs, C, dk), per_head), pl.BlockSpec((hs, 1, dv), per_head),
                  pl.BlockSpec((hs, 1, dv), per_head)],
        out_specs=pl.BlockSpec((S, hs * dv), lambda b, h: (b, h)),
        out_shape=jax.ShapeDtypeStruct((T, n_heads * dv), BF16),
        scratch_shapes=[pltpu.VMEM((hs, dk, dv), F32)],
        compiler_params=_cparams(("parallel", "arbitrary")),
        name="retention_core",
    )(qkv, qkv, qkv, g, intra, q_decay, k_decay, chunk_decay, out_norm.reshape(n_heads, 1, dv))


def _rot_tables(positions, dim, theta):
    inv = theta ** (-jnp.arange(0, dim, 2, dtype=F32) / dim)
    ang = positions.astype(F32).reshape(-1, 1) * inv[None, :]
    return jnp.cos(ang), jnp.sin(ang)


def dsa_layer(x2d, positions, gain, w_in, w_out, q_norm, k_norm, idx_k_norm, layer, B, S):
    T, D = x2d.shape
    n_heads = D // HEAD_DIM
    n_kv = n_heads // 4
    rot_dim = HEAD_DIM // 4
    cos, sin = _rot_tables(positions, rot_dim, ROPE_THETA)
    ones = jnp.ones((T, HEAD_DIM - rot_dim), F32)
    zeros_r = jnp.zeros((T, HEAD_DIM - rot_dim // 2), F32)
    tabs = (jnp.concatenate([cos, cos, ones], axis=1),
            jnp.concatenate([-sin, zeros_r], axis=1),
            jnp.concatenate([jnp.zeros_like(sin), sin, jnp.zeros((T, HEAD_DIM - rot_dim), F32)], axis=1))
    hn = rmsnorm_cast(x2d, gain)
    heads, wi = dsa_in_proj(hn, w_in, layer, tabs, q_norm, k_norm, idx_k_norm, n_heads, n_kv)
    top_k = min(IDX_TOPK_MAX, S // 4)
    o = dsa_attention_core(heads, jnp.transpose(wi[:, :IDX_HEADS]), n_heads, n_kv, B, S, top_k)
    return matmul(o, w_out, layer, res=x2d, head_major=True, tm=2048, tn=512, name="dsa_out_proj")


def retention_layer(x2d, positions, gain, w_in, w_out, out_norm, layer, B, S):
    T, D = x2d.shape
    n_heads = D // RET_QK_DIM
    cos, sin = _rot_tables(positions, RET_QK_DIM, RET_THETA)
    hn = rmsnorm_cast(x2d, gain)
    qkv, g = ret_in_proj(hn, w_in, layer, cos, sin, n_heads)
    og = retention_core(qkv, g, out_norm, B, S, n_heads)
    return matmul(og, w_out, layer, res=x2d, tm=1024, tn=256, name="ret_out_proj")


def kernel(x, p, positions, attn_norm, ffn_norm, ple_norm, a_w_in, a_w_out, a_q_norm, a_k_norm, a_idx_k_norm,
           b_w_in, b_w_out, b_out_norm, moe_w_group, moe_b_group, moe_w_expert, moe_b_expert, moe_w_gate,
           moe_w_up, moe_w_down, ple_proj, ple_gate):
    B, S, D = x.shape
    depth = p.shape[0]
    T = B * S
    x2d = x.reshape(T, D)
    p3d = p.reshape(depth, T, p.shape[-1])
    for i in range(depth):
        j = i // 2
        if i % 2 == 0:
            x2d = dsa_layer(x2d, positions, attn_norm[i], a_w_in, a_w_out, a_q_norm[j], a_k_norm[j],
                            a_idx_k_norm[j], j, B, S)
        else:
            x2d = retention_layer(x2d, positions, attn_norm[i], b_w_in, b_w_out, b_out_norm[j], j, B, S)
        x2d = hier_moe_residual(x2d, ffn_norm[i], moe_w_group[i], moe_b_group[i], moe_w_expert[i],
                                moe_b_expert[i], moe_w_gate, moe_w_up, moe_w_down, i)
        x2d = ple_residual(x2d, ple_norm[i], ple_gate, p3d, ple_proj, i)
    return x2d.reshape(B, S, D)
```

```python
import functools

import jax
import jax.numpy as jnp
from jax import lax
from jax.experimental import pallas as pl
from jax.experimental.pallas import tpu as pltpu

F32 = jnp.float32
BF16 = jnp.bfloat16
I32 = jnp.int32

RMS_EPS = 1e-6
HEAD_DIM = 128
IDX_HEADS = 64
IDX_TOPK_MAX = 256
ROPE_THETA = 500000.0
RET_QK_DIM = 256
RET_THETA = 10000.0
RET_CHUNK = 128
N_GROUPS = 8
EXPERTS_PER_GROUP = 8
LANES = 128
VMEM_LIMIT_BYTES = 56 * 1024 * 1024
NEG_BIG = -1e30
INT_MIN = -2147483648


def _cparams(sem):
    return pltpu.CompilerParams(dimension_semantics=sem, vmem_limit_bytes=VMEM_LIMIT_BYTES)


def _pick(n, cands):
    for c in cands:
        if n % c == 0:
            return c
    raise ValueError(f"no tile for {n} in {cands}")


def _rmsnorm_kernel(x_ref, g_ref, o_ref):
    x = x_ref[...]
    ms = jnp.mean(x * x, axis=-1, keepdims=True)
    o_ref[...] = (x * lax.rsqrt(ms + RMS_EPS) * g_ref[...]).astype(o_ref.dtype)


def rmsnorm_cast(x2d, gain):
    T, D = x2d.shape
    tm = _pick(T, (512, 256, 128, 8))
    return pl.pallas_call(
        _rmsnorm_kernel,
        grid=(T // tm,),
        in_specs=[pl.BlockSpec((tm, D), lambda i: (i, 0)), pl.BlockSpec((1, D), lambda i: (0, 0))],
        out_specs=pl.BlockSpec((tm, D), lambda i: (i, 0)),
        out_shape=jax.ShapeDtypeStruct((T, D), BF16),
        compiler_params=_cparams(("parallel",)),
        name="rmsnorm_cast",
    )(x2d, gain.reshape(1, D))


def _mm_kernel(*refs, nk, head_major, has_res, n_valid, col_block0):
    a_ref, w_ref = refs[0], refs[1]
    r_ref = refs[2] if has_res else None
    o_ref = refs[2 + has_res]
    acc_ref = refs[3 + has_res] if nk > 1 else None
    if head_major:
        a = jnp.concatenate([a_ref[h] for h in range(a_ref.shape[0])], axis=1)
    else:
        a = a_ref[...]
    w = w_ref[...]
    if n_valid % w.shape[1]:
        col = (pl.program_id(1) + col_block0) * w.shape[1] + lax.broadcasted_iota(I32, (1, w.shape[1]), 1)
        w = jnp.where(col < n_valid, w, 0.0)
    part = jnp.dot(a, w.astype(BF16), preferred_element_type=F32)

    def finish(acc):
        if has_res:
            acc = acc + r_ref[...]
        o_ref[...] = acc.astype(o_ref.dtype)

    if nk == 1:
        finish(part)
    else:
        k = pl.program_id(2)

        @pl.when(k == 0)
        def _():
            acc_ref[...] = part

        @pl.when(k > 0)
        def _():
            acc_ref[...] += part

        @pl.when(k == nk - 1)
        def _():
            finish(acc_ref[...])


def matmul(a, w, layer, *, col0=0, n_out=None, res=None, out_dtype=F32, tm=1024, tn=512, tk=None,
           head_major=False, name="matmul"):
    if head_major:
        nh, M, hd = a.shape
        K = nh * hd
    else:
        M, K = a.shape
    N = w.shape[2]
    n_out = N - col0 if n_out is None else n_out
    assert col0 % tn == 0
    tm = min(tm, M)
    tk = K if (tk is None or head_major) else min(tk, K)
    assert M % tm == 0 and K % tk == 0 and n_out % tn == 0
    nk = K // tk
    a_mode = dict(pipeline_mode=pl.Buffered(1)) if nk == 1 else {}
    if head_major:
        a_spec = pl.BlockSpec((nh, tm, hd), lambda i, j, k: (0, i, 0), **a_mode)
    else:
        a_spec = pl.BlockSpec((tm, tk), lambda i, j, k: (i, k), **a_mode)
    in_specs = [a_spec, pl.BlockSpec((None, tk, tn), lambda i, j, k: (layer, k, j + col0 // tn))]
    args = [a, w]
    if res is not None:
        in_specs.append(pl.BlockSpec((tm, tn), lambda i, j, k: (i, j)))
        args.append(res)
    return pl.pallas_call(
        functools.partial(_mm_kernel, nk=nk, head_major=head_major, has_res=res is not None, n_valid=N,
                          col_block0=col0 // tn),
        grid=(M // tm, n_out // tn, nk),
        in_specs=in_specs,
        out_specs=pl.BlockSpec((tm, tn), lambda i, j, k: (i, j)),
        out_shape=jax.ShapeDtypeStruct((M, n_out), out_dtype),
        scratch_shapes=[pltpu.VMEM((tm, tn), F32)] if nk > 1 else [],
        compiler_params=_cparams(("parallel", "parallel", "arbitrary")),
        name=name,
    )(*args)


def _rot_head(x, c, s1, s2):
    half = HEAD_DIM // 8
    return x * c + pltpu.roll(x, HEAD_DIM - half, 1) * s1 + pltpu.roll(x, half, 1) * s2


def _flag(pred, shape):
    return jnp.broadcast_to(pred.astype(I32), shape) == 1


def _dsa_proj_kernel(a_ref, w_ref, c_ref, s1_ref, s2_ref, gains_ref, heads_ref, wi_ref, *, n_valid, bounds,
                     wi_lane, wscale, row_chunk):
    j = pl.program_id(1)
    tm = a_ref.shape[0]
    tn = w_ref.shape[1]
    b_q, b_k, b_v, b_qi = bounds
    col = j * tn + lax.broadcasted_iota(I32, (1, tn), 1)
    w = jnp.where(col < n_valid, w_ref[...], 0.0).astype(BF16)
    is_q = j < b_q
    is_k = jnp.logical_and(j >= b_q, j < b_k)
    is_v = jnp.logical_and(j >= b_k, j < b_v)
    is_ki = j == b_qi
    lane = (1, HEAD_DIM)
    gain = jnp.where(_flag(is_q, lane), gains_ref[0:1, :],
                     jnp.where(_flag(is_k, lane), gains_ref[1:2, :],
                               jnp.where(_flag(is_ki, lane), gains_ref[2:3, :], 1.0)))
    use_norm = _flag(jnp.logical_or(jnp.logical_or(is_q, is_k), is_ki), (row_chunk, 1))
    no_rot = _flag(is_v, (row_chunk, HEAD_DIM))
    for rc in range(tm // row_chunk):
        rows = pl.ds(rc * row_chunk, row_chunk)
        acc = jnp.dot(a_ref[rows, :], w, preferred_element_type=F32)
        c = jnp.where(no_rot, 1.0, c_ref[rows, :])
        s1 = jnp.where(no_rot, 0.0, s1_ref[rows, :])
        s2 = jnp.where(no_rot, 0.0, s2_ref[rows, :])
        for hh in range(tn // HEAD_DIM):
            x = acc[:, hh * HEAD_DIM:(hh + 1) * HEAD_DIM]
            ms = jnp.mean(x * x, axis=-1, keepdims=True)
            x = x * jnp.where(use_norm, lax.rsqrt(ms + RMS_EPS), 1.0) * gain
            heads_ref[hh, rows, :] = _rot_head(x, c, s1, s2).astype(heads_ref.dtype)
        wi_ref[rows, :] = acc[:, wi_lane:wi_lane + HEAD_DIM] * wscale


def dsa_in_proj(hn, w_in, layer, tabs, q_gain, k_gain, ki_gain, n_heads, n_kv):
    T, K = hn.shape
    n_cols = w_in.shape[2]
    seg_ends = [n_heads * HEAD_DIM]
    for width in (n_kv * HEAD_DIM, n_kv * HEAD_DIM, IDX_HEADS * HEAD_DIM):
        seg_ends.append(seg_ends[-1] + width)
    tn = next(t for t in (512, 256, 128) if all(e % t == 0 for e in seg_ends))
    hb = tn // HEAD_DIM
    bounds = tuple(e // tn for e in seg_ends)
    b_v, b_qi = bounds[2], bounds[3]
    n_out = -(-n_cols // tn) * tn
    wi_lane = (seg_ends[-1] + HEAD_DIM) % tn
    assert n_cols == seg_ends[-1] + HEAD_DIM + IDX_HEADS and (n_out - tn) <= seg_ends[-1] + HEAD_DIM
    tm = min(2048, T)
    row_chunk = min(512, tm)
    c, s1, s2 = tabs
    gains = jnp.zeros((8, HEAD_DIM), F32).at[0].set(q_gain * (HEAD_DIM ** -0.5 * LOG2_E)).at[1].set(k_gain)
    gains = gains.at[2].set(ki_gain)
    once = dict(pipeline_mode=pl.Buffered(1))
    row = lambda i, j: (i, 0)

    def head_block(i, j):
        pos = jnp.where(j < b_v, j + (b_qi - b_v), jnp.where(j < b_qi, j - b_v, j))
        return (pos, i, 0)

    return pl.pallas_call(
        functools.partial(_dsa_proj_kernel, n_valid=n_cols, bounds=bounds, wi_lane=wi_lane,
                          wscale=IDX_HEADS ** -0.5 * HEAD_DIM ** -0.5, row_chunk=row_chunk),
        grid=(T // tm, n_out // tn),
        in_specs=[pl.BlockSpec((tm, K), row, **once),
                  pl.BlockSpec((None, K, tn), lambda i, j: (layer, 0, j)),
                  pl.BlockSpec((tm, HEAD_DIM), row, **once), pl.BlockSpec((tm, HEAD_DIM), row, **once),
                  pl.BlockSpec((tm, HEAD_DIM), row, **once), pl.BlockSpec((8, HEAD_DIM), lambda i, j: (0, 0))],
        out_specs=[pl.BlockSpec((hb, tm, HEAD_DIM), head_block), pl.BlockSpec((tm, HEAD_DIM), row)],
        out_shape=[jax.ShapeDtypeStruct((n_out // HEAD_DIM, T, HEAD_DIM), BF16),
                   jax.ShapeDtypeStruct((T, HEAD_DIM), F32)],
        compiler_params=_cparams(("parallel", "arbitrary")),
        name="dsa_in_proj",
    )(hn, w_in, c, s1, s2, gains)


_NT = (((1,), (1,)), ((), ()))


IDX_UNROLL = 4
LOG2_E = 1.4426950408889634
SOFTMAX_SHIFT_MAX = 40.0


def _dsa_tile(qi_ref, ki_ref, wi_ref, q_ref, k_ref, v_ref, o_ref, sc_ref, u_ref, bias_ref,
              *, top_k, sub, steps_per_kv, nk, q0):
    tq = sc_ref.shape[1]
    keys = pl.ds(0, nk)

    def idx_term(h):
        s = lax.dot_general(ki_ref[keys, :], qi_ref[h], _NT, preferred_element_type=F32)
        return jnp.maximum(s, 0.0) * wi_ref[pl.ds(h, 1), :]

    def idx_terms(h0):
        acc = idx_term(h0)
        for d in range(1, IDX_UNROLL):
            acc = acc + idx_term(h0 + d)
        return acc

    sc_ref[keys, :] = idx_terms(0)

    def idx_body(g, carry):
        sc_ref[keys, :] += idx_terms(g * IDX_UNROLL)
        return carry

    lax.fori_loop(1, IDX_HEADS // IDX_UNROLL, idx_body, 0)

    key_pos = lax.broadcasted_iota(I32, (nk, tq), 0)
    q_pos = q0 + lax.broadcasted_iota(I32, (nk, tq), 1)
    causal = key_pos <= q_pos
    bits = pltpu.bitcast(sc_ref[keys, :], I32)
    u = bits ^ ((bits >> 31) & 0x7FFFFFFF)
    u_ref[keys, :] = jnp.where(causal, u, INT_MIN)

    def count_ge(cand):
        return jnp.sum(jnp.where(u_ref[keys, :] >= cand, 1.0, 0.0), axis=0, keepdims=True)

    zero = jnp.zeros((1, tq), I32)
    v0 = jnp.where(count_ge(zero) >= top_k, zero, jnp.full((1, tq), INT_MIN, I32))

    def bis_body(it, v):
        cand = v + lax.shift_left(jnp.int32(1), 30 - it)
        return jnp.where(count_ge(cand) >= top_k, cand, v)

    thr = lax.fori_loop(0, 31, bis_body, v0)
    keep = jnp.logical_and(u_ref[keys, :] >= thr, causal)
    bias_ref[:, keys] = jnp.transpose(jnp.where(keep, 0.0, NEG_BIG))

    def att_body(st, carry):
        n = st // steps_per_kv
        q = q_ref[pl.ds(st * sub, sub)].reshape(sub * tq, HEAD_DIM)
        kf = k_ref[n, keys, :].astype(F32)
        kmax = jnp.sqrt(jnp.max(jnp.sum(kf * kf, axis=-1, keepdims=True), axis=0, keepdims=True))
        qf = q.astype(F32)
        bound = jnp.sqrt(jnp.sum(qf * qf, axis=-1, keepdims=True)) * kmax
        bound_ok = jnp.max(bound) <= SOFTMAX_SHIFT_MAX

        def attend(shift_by_bound):
            lg = lax.dot_general(q, k_ref[n, keys, :], _NT, preferred_element_type=F32)
            ps, ls = [], []
            for g in range(sub):
                x = lg[g * tq:(g + 1) * tq] + bias_ref[:, keys]
                shift = bound[g * tq:(g + 1) * tq] if shift_by_bound else jnp.max(x, axis=-1, keepdims=True)
                p = jnp.exp2(x - shift)
                ls.append(jnp.sum(p, axis=-1, keepdims=True))
                ps.append(p.astype(BF16))
            o = jnp.dot(jnp.concatenate(ps, axis=0), v_ref[n, keys, :], preferred_element_type=F32)
            for g in range(sub):
                o_ref[st * sub + g] = (o[g * tq:(g + 1) * tq] / ls[g]).astype(o_ref.dtype)

        @pl.when(bound_ok)
        def _():
            attend(True)

        @pl.when(jnp.logical_not(bound_ok))
        def _():
            attend(False)

        return carry

    lax.fori_loop(0, q_ref.shape[0] // sub, att_body, 0)


def _dsa_kernel(*refs, **static):
    S, tq = refs[7].shape
    i = pl.program_id(1)
    for c in range(S // tq):
        @pl.when(i == c)
        def _(c=c):
            _dsa_tile(*refs, nk=(c + 1) * tq, q0=c * tq, **static)


def dsa_attention_core(heads, wi, n_heads, n_kv, B, S, top_k):
    T = heads.shape[1]
    tq = _pick(S, (256, 128))
    nq = S // tq
    group = n_heads // n_kv
    sub = 2 if group % 2 == 0 else 1
    off_q, off_k = IDX_HEADS, IDX_HEADS + n_heads
    off_v, off_ki = off_k + n_kv, off_k + 2 * n_kv
    assert off_q % n_heads == 0 and off_k % n_kv == 0 and off_v % n_kv == 0
    once = dict(pipeline_mode=pl.Buffered(1))
    return pl.pallas_call(
        functools.partial(_dsa_kernel, top_k=top_k, sub=sub, steps_per_kv=group // sub),
        grid=(B, nq),
        in_specs=[pl.BlockSpec((IDX_HEADS, tq, HEAD_DIM), lambda b, i: (0, b * nq + i, 0)),
                  pl.BlockSpec((None, S, HEAD_DIM), lambda b, i: (off_ki, b, 0), **once),
                  pl.BlockSpec((IDX_HEADS, tq), lambda b, i: (0, b * nq + i)),
                  pl.BlockSpec((n_heads, tq, HEAD_DIM), lambda b, i: (off_q // n_heads, b * nq + i, 0)),
                  pl.BlockSpec((n_kv, S, HEAD_DIM), lambda b, i: (off_k // n_kv, b, 0), **once),
                  pl.BlockSpec((n_kv, S, HEAD_DIM), lambda b, i: (off_v // n_kv, b, 0), **once)],
        out_specs=pl.BlockSpec((n_heads, tq, HEAD_DIM), lambda b, i: (0, b * nq + i, 0)),
        out_shape=jax.ShapeDtypeStruct((n_heads, T, HEAD_DIM), BF16),
        scratch_shapes=[pltpu.VMEM((S, tq), F32), pltpu.VMEM((S, tq), I32), pltpu.VMEM((tq, S), F32)],
        compiler_params=_cparams(("parallel", "arbitrary")),
        name="dsa_core",
    )(heads, heads, wi, heads, heads, heads)


def _route_kernel(x_ref, g_ref, w_ref, b_ref, o_ref):
    x = x_ref[...]
    ms = jnp.mean(x * x, axis=-1, keepdims=True)
    t = x * lax.rsqrt(ms + RMS_EPS) * g_ref[...]
    logits = jnp.dot(t, w_ref[...], precision=lax.Precision.HIGHEST, preferred_element_type=F32) + b_ref[...]
    lane = lax.broadcasted_iota(I32, logits.shape, 1)
    lane_f = lane.astype(F32)
    big = float(LANES)
    gl = jnp.where(lane < N_GROUPS, logits, -jnp.inf)
    gmax = jnp.max(gl, axis=-1, keepdims=True)
    g_sel = jnp.min(jnp.where(gl == gmax, lane_f, big), axis=-1, keepdims=True)
    g_w = 1.0 / jnp.sum(jnp.exp(gl - gmax), axis=-1, keepdims=True)
    lo = N_GROUPS + g_sel * EXPERTS_PER_GROUP
    in_grp = jnp.logical_and(lane_f >= lo, lane_f < lo + EXPERTS_PER_GROUP)
    el = jnp.where(in_grp, logits, -jnp.inf)
    m1 = jnp.max(el, axis=-1, keepdims=True)
    i1 = jnp.min(jnp.where(el == m1, lane_f, big), axis=-1, keepdims=True)
    el2 = jnp.where(lane_f == i1, -jnp.inf, el)
    m2 = jnp.max(el2, axis=-1, keepdims=True)
    i2 = jnp.min(jnp.where(el2 == m2, lane_f, big), axis=-1, keepdims=True)
    z = jnp.sum(jnp.exp(el - m1), axis=-1, keepdims=True)
    p1 = 1.0 / z
    p2 = jnp.exp(m2 - m1) / z
    w1 = p1 / (p1 + p2) * g_w
    w2 = p2 / (p1 + p2) * g_w
    out = jnp.where(lane == 0, i1 - N_GROUPS, jnp.where(lane == 1, i2 - N_GROUPS,
          jnp.where(lane == 2, w1, jnp.where(lane == 3, w2, 0.0))))
    o_ref[...] = out


def moe_route(x2d, gain, w_group, b_group, w_expert, b_expert):
    T, D = x2d.shape
    n_log = w_group.shape[1] + w_expert.shape[1]
    w = jnp.concatenate([w_group, w_expert, jnp.zeros((D, LANES - n_log), F32)], axis=1)
    b = jnp.concatenate([b_group, b_expert, jnp.zeros((LANES - n_log,), F32)]).reshape(1, LANES)
    tm = _pick(T, (256, 128, 8))
    return pl.pallas_call(
        _route_kernel,
        grid=(T // tm,),
        in_specs=[pl.BlockSpec((tm, D), lambda i: (i, 0)), pl.BlockSpec((1, D), lambda i: (0, 0)),
                  pl.BlockSpec((D, LANES), lambda i: (0, 0)), pl.BlockSpec((1, LANES), lambda i: (0, 0))],
        out_specs=pl.BlockSpec((tm, LANES), lambda i: (i, 0)),
        out_shape=jax.ShapeDtypeStruct((T, LANES), F32),
        compiler_params=_cparams(("parallel",)),
        name="moe_route",
    )(x2d, gain.reshape(1, D), w, b)


def _row_copy(src_hbm, dst_ref, sem, src_row, dst_row):
    return pltpu.make_async_copy(src_hbm.at[pl.ds(src_row, 1), :], dst_ref.at[pl.ds(dst_row, 1), :], sem)


DMA_PRIORITIES = 2


def _start_row_gather(idx_ref, src_hbm, dst_ref, sem, base, n_rows, priorities=(0, 1)):
    def start(g, c):
        for d, pr in enumerate(priorities):
            r = g * len(priorities) + d
            _row_copy(src_hbm, dst_ref, sem, idx_ref[base + r], r).start(priority=pr)
        return c

    lax.fori_loop(0, n_rows // len(priorities), start, 0)


def _wait_row_gather(src_hbm, dst_ref, sem, n_rows):
    pltpu.make_async_copy(src_hbm.at[pl.ds(0, n_rows), :], dst_ref, sem).wait()


FFN_SLOTS = 3


def _ffn_kernel(src_ref, te_ref, nu_ref, x_hbm, gate_ref, g_ref, wg_ref, wu_ref, wd_ref, o_ref, xbuf, sem, *,
                priorities):
    j = pl.program_id(0)
    tm = o_ref.shape[0]
    n_used = nu_ref[0]
    ahead = FFN_SLOTS - 1

    def fetch(tile, slot):
        _start_row_gather(src_ref, x_hbm, xbuf.at[slot], sem.at[slot], tile * tm, tm, priorities)

    for t in range(ahead):
        @pl.when(jnp.logical_and(j == 0, t < n_used))
        def _(t=t):
            fetch(t, t)

    @pl.when(j + ahead < n_used)
    def _():
        fetch(j + ahead, lax.rem(j + ahead, FFN_SLOTS))

    @pl.when(j < n_used)
    def _():
        slot = lax.rem(j, FFN_SLOTS)
        _wait_row_gather(x_hbm, xbuf.at[slot], sem.at[slot], tm)
        x = xbuf[slot]
        ms = jnp.mean(x * x, axis=-1, keepdims=True)
        t = (x * lax.rsqrt(ms + RMS_EPS) * g_ref[...]).astype(BF16)
        a = jnp.dot(t, wg_ref[0].astype(BF16), preferred_element_type=F32)
        u = jnp.dot(t, wu_ref[0].astype(BF16), preferred_element_type=F32)
        hid = a * jax.nn.sigmoid(a) * u * gate_ref[...]
        o_ref[...] = jnp.dot(hid.astype(BF16), wd_ref[0].astype(BF16), preferred_element_type=F32)

    @pl.when(j >= n_used)
    def _():
        o_ref[...] = jnp.zeros_like(o_ref)


def moe_ffn(x2d, src_tok, gate_s, gain, w_gate, w_up, w_down, tile_expert, n_used, tm, priorities):
    D = x2d.shape[1]
    P = src_tok.shape[0]
    F = w_gate.shape[-1]
    assert P % tm == 0 and tm % DMA_PRIORITIES == 0
    wmap = lambda j, src, te, nu: (te[j], 0, 0)
    return pl.pallas_call(
        functools.partial(_ffn_kernel, priorities=priorities),
        grid_spec=pltpu.PrefetchScalarGridSpec(
            num_scalar_prefetch=3, grid=(P // tm,),
            in_specs=[pl.BlockSpec(memory_space=pl.ANY),
                      pl.BlockSpec((tm, 1), lambda j, src, te, nu: (j, 0)),
                      pl.BlockSpec((1, D), lambda j, src, te, nu: (0, 0)),
                      pl.BlockSpec((1, D, F), wmap), pl.BlockSpec((1, D, F), wmap),
                      pl.BlockSpec((1, F, D), wmap)],
            out_specs=pl.BlockSpec((tm, D), lambda j, src, te, nu: (j, 0)),
            scratch_shapes=[pltpu.VMEM((FFN_SLOTS, tm, D), F32), pltpu.SemaphoreType.DMA((FFN_SLOTS,))]),
        out_shape=jax.ShapeDtypeStruct((P, D), F32),
        compiler_params=_cparams(("arbitrary",)),
        name="moe_ffn",
    )(src_tok, tile_expert, n_used, x2d, gate_s, gain.reshape(1, D), w_gate, w_up, w_down)


def _combine_kernel(p1_ref, p2_ref, ys_hbm, x_ref, gain_ref, o_ref, on_ref, buf, sem):
    i = pl.program_id(0)
    R = x_ref.shape[0]

    def fetch(tile, slot):
        _start_row_gather(p1_ref, ys_hbm, buf.at[2 * slot], sem.at[2 * slot], tile * R, R)
        _start_row_gather(p2_ref, ys_hbm, buf.at[2 * slot + 1], sem.at[2 * slot + 1], tile * R, R)

    @pl.when(i == 0)
    def _():
        fetch(0, 0)

    @pl.when(i + 1 < pl.num_programs(0))
    def _():
        fetch(i + 1, lax.rem(i + 1, 2))

    slot = lax.rem(i, 2)
    _wait_row_gather(ys_hbm, buf.at[2 * slot], sem.at[2 * slot], R)
    _wait_row_gather(ys_hbm, buf.at[2 * slot + 1], sem.at[2 * slot + 1], R)
    y = x_ref[...] + (buf[2 * slot] + buf[2 * slot + 1])
    o_ref[...] = y
    ms = jnp.mean(y * y, axis=-1, keepdims=True)
    on_ref[...] = (y * lax.rsqrt(ms + RMS_EPS) * gain_ref[...]).astype(on_ref.dtype)


def moe_combine(x2d, ys, pos1, pos2, next_gain, rows_per_step):
    T, D = x2d.shape
    R = rows_per_step
    tile = lambda i, a, b: (i, 0)
    return pl.pallas_call(
        _combine_kernel,
        grid_spec=pltpu.PrefetchScalarGridSpec(
            num_scalar_prefetch=2, grid=(T // R,),
            in_specs=[pl.BlockSpec(memory_space=pl.ANY), pl.BlockSpec((R, D), tile),
                      pl.BlockSpec((1, D), lambda i, a, b: (0, 0))],
            out_specs=[pl.BlockSpec((R, D), tile), pl.BlockSpec((R, D), tile)],
            scratch_shapes=[pltpu.VMEM((4, R, D), F32), pltpu.SemaphoreType.DMA((4,))]),
        out_shape=[jax.ShapeDtypeStruct((T, D), F32), jax.ShapeDtypeStruct((T, D), BF16)],
        compiler_params=_cparams(("arbitrary",)),
        name="moe_combine",
    )(pos1, pos2, ys, x2d, next_gain.reshape(1, D))


def hier_moe_residual(x2d, gain, w_group, b_group, w_expert, b_expert, w_gate, w_up, w_down, layer, next_gain):
    T, D = x2d.shape
    depth, G, E = w_gate.shape[0], w_gate.shape[1], w_gate.shape[2]
    n_exp = G * E
    F = w_gate.shape[-1]
    route = moe_route(x2d, gain, w_group, b_group, w_expert, b_expert)
    eid = route[:, 0:2].astype(I32).reshape(2 * T)
    wts = route[:, 2:4].reshape(2 * T)
    tm = 256 if 2 * T >= 4096 else 128
    nt = (2 * T) // tm + n_exp
    onehot = (eid[None, :] == jnp.arange(n_exp, dtype=I32)[:, None]).astype(I32)
    csum = jnp.cumsum(onehot, axis=1)
    counts = csum[:, -1]
    rank = jnp.sum(onehot * csum, axis=0) - 1
    tiles = (counts + tm - 1) // tm
    tile_end = jnp.cumsum(tiles)
    tile_start = tile_end - tiles
    n_used = tile_end[-1:]
    j = jnp.arange(nt, dtype=I32)
    tile_expert = jnp.minimum(jnp.sum(tile_end[None, :] <= j[:, None], axis=1, dtype=I32), n_exp - 1)
    last_expert = jnp.sum(tile_end < n_used[0], dtype=I32)
    tile_expert = jnp.where(j < n_used[0], tile_expert, last_expert)
    pos = (tile_start[eid] * tm + rank).astype(I32)
    inv = jnp.full((nt * tm,), -1, I32).at[pos].set(jnp.arange(2 * T, dtype=I32))
    valid = inv >= 0
    src_pair = jnp.maximum(inv, 0)
    src_tok = jnp.where(valid, src_pair // 2, 0)
    gate_s = jnp.where(valid, wts[src_pair], 0.0).reshape(nt * tm, 1)
    pos = pos.reshape(T, 2)

    ys = moe_ffn(x2d, src_tok, gate_s, gain, w_gate.reshape(depth * n_exp, D, F),
                 w_up.reshape(depth * n_exp, D, F), w_down.reshape(depth * n_exp, F, D),
                 tile_expert + layer * n_exp, n_used, tm, (1, 1) if layer % 2 == 0 else (0, 0))
    return moe_combine(x2d, ys, pos[:, 0], pos[:, 1], next_gain, min(256, T))


def _ple_kernel(a_ref, w_ref, p_ref, pw_ref, x_ref, o_ref):
    z = jnp.dot(a_ref[...], w_ref[...].astype(BF16), preferred_element_type=F32)
    e = jnp.dot(p_ref[...].astype(BF16), pw_ref[...].astype(BF16), preferred_element_type=F32)
    o_ref[...] = x_ref[...] + jax.nn.sigmoid(z) * e


def ple_residual(x2d, a, w_gate, p3d, w_proj, layer):
    T, D = x2d.shape
    P = p3d.shape[2]
    tm, tn = min(1024, T), 512
    return pl.pallas_call(
        _ple_kernel,
        grid=(T // tm, D // tn),
        in_specs=[pl.BlockSpec((tm, D), lambda i, j: (i, 0), pipeline_mode=pl.Buffered(1)),
                  pl.BlockSpec((None, D, tn), lambda i, j: (layer, 0, j)),
                  pl.BlockSpec((None, tm, P), lambda i, j: (layer, i, 0)),
                  pl.BlockSpec((None, P, tn), lambda i, j: (layer, 0, j)),
                  pl.BlockSpec((tm, tn), lambda i, j: (i, j))],
        out_specs=pl.BlockSpec((tm, tn), lambda i, j: (i, j)),
        out_shape=jax.ShapeDtypeStruct((T, D), F32),
        compiler_params=_cparams(("parallel", "parallel")),
        name="ple",
    )(a, w_gate, p3d, w_proj, x2d)


def _ret_proj_kernel(a_ref, w_ref, cos_ref, sin_ref, qkv_ref, *, n_qk, row_chunk):
    j = pl.program_id(1)
    tm = a_ref.shape[0]
    tn = w_ref.shape[1]
    half = RET_QK_DIM // 2
    w = w_ref[...].astype(BF16)
    shape = (row_chunk, half)
    is_rot = _flag(j < n_qk, shape)
    kscale = jnp.where(_flag(jnp.logical_and(j >= n_qk // 2, j < n_qk), shape), RET_QK_DIM ** -0.5, 1.0)
    for rc in range(tm // row_chunk):
        rows = pl.ds(rc * row_chunk, row_chunk)
        acc = jnp.dot(a_ref[rows, :], w, preferred_element_type=F32)
        cos = jnp.where(is_rot, cos_ref[rows, :], 1.0) * kscale
        sin = jnp.where(is_rot, sin_ref[rows, :], 0.0) * kscale
        for hh in range(tn // RET_QK_DIM):
            x1 = acc[:, hh * RET_QK_DIM:hh * RET_QK_DIM + half]
            x2 = acc[:, hh * RET_QK_DIM + half:(hh + 1) * RET_QK_DIM]
            qkv_ref[rows, hh * RET_QK_DIM:hh * RET_QK_DIM + half] = (x1 * cos - x2 * sin).astype(BF16)
            qkv_ref[rows, hh * RET_QK_DIM + half:(hh + 1) * RET_QK_DIM] = (x1 * sin + x2 * cos).astype(BF16)


def ret_in_proj(hn, w_in, layer, cos, sin, n_heads):
    T, K = hn.shape
    dk = RET_QK_DIM
    qkv_cols = 4 * n_heads * dk
    tn = 512 if (n_heads * dk) % 512 == 0 else dk
    tm = min(2048, T)
    once = dict(pipeline_mode=pl.Buffered(1))
    row = lambda i, j: (i, 0)
    qkv = pl.pallas_call(
        functools.partial(_ret_proj_kernel, n_qk=2 * n_heads * dk // tn, row_chunk=min(512, tm)),
        grid=(T // tm, qkv_cols // tn),
        in_specs=[pl.BlockSpec((tm, K), row, **once),
                  pl.BlockSpec((None, K, tn), lambda i, j: (layer, 0, j)),
                  pl.BlockSpec((tm, dk // 2), row, **once), pl.BlockSpec((tm, dk // 2), row, **once)],
        out_specs=pl.BlockSpec((tm, tn), lambda i, j: (i, j)),
        out_shape=jax.ShapeDtypeStruct((T, qkv_cols), BF16),
        compiler_params=_cparams(("parallel", "parallel")),
        name="ret_qkv_proj",
    )(hn, w_in, cos, sin)
    g = matmul(hn, w_in, layer, col0=qkv_cols, tm=tm, tn=tn, name="ret_g_proj")
    return qkv, g


RET_HEADS_PER_STEP = 2


def _retention_kernel(q_ref, k_ref, v_ref, g_ref, intra_ref, qd_ref, kd_ref, cd_ref, gn_ref, o_ref, state_ref):
    S = q_ref.shape[0]
    C = intra_ref.shape[1]
    dk = RET_QK_DIM
    dv = 2 * dk
    state_ref[...] = jnp.zeros_like(state_ref)

    def chunk(c, carry):
        rows = pl.ds(pl.multiple_of(c * C, C), C)
        for hh in range(RET_HEADS_PER_STEP):
            qc = q_ref[rows, hh * dk:(hh + 1) * dk]
            kc = k_ref[rows, hh * dk:(hh + 1) * dk]
            vc = v_ref[rows, hh * dv:(hh + 1) * dv]
            att = lax.dot_general(qc, kc, _NT, preferred_element_type=F32) * intra_ref[hh]
            inner = jnp.dot(att.astype(BF16), vc, preferred_element_type=F32)
            state = state_ref[hh]
            cross = jnp.dot((qc.astype(F32) * qd_ref[hh]).astype(BF16), state.astype(BF16),
                            preferred_element_type=F32)
            kt = jnp.transpose(kc.astype(F32) * kd_ref[hh]).astype(BF16)
            state_ref[hh] = state * cd_ref[hh] + jnp.dot(kt, vc, preferred_element_type=F32)
            o = inner + cross
            o = o * lax.rsqrt(jnp.mean(o * o, axis=-1, keepdims=True) + RMS_EPS) * gn_ref[hh]
            g = g_ref[rows, hh * dv:(hh + 1) * dv]
            o_ref[rows, hh * dv:(hh + 1) * dv] = (g * jax.nn.sigmoid(g) * o).astype(o_ref.dtype)
        return carry

    lax.fori_loop(0, S // C, chunk, 0)


def retention_core(qkv, g, out_norm, B, S, n_heads):
    T = qkv.shape[0]
    dk = RET_QK_DIM
    dv = 2 * dk
    C = RET_CHUNK
    hs = RET_HEADS_PER_STEP
    assert n_heads % hs == 0
    log_gamma = jnp.log1p(-jnp.exp2(-5.0 - jnp.arange(n_heads, dtype=F32)))
    i = jnp.arange(C, dtype=F32)
    dist = i[:, None] - i[None, :]
    intra = jnp.where(dist[None] >= 0, jnp.exp(log_gamma[:, None, None] * jnp.maximum(dist, 0.0)[None]), 0.0)
    q_decay = jnp.broadcast_to(jnp.exp(log_gamma[:, None] * (i[None, :] + 1.0))[:, :, None], (n_heads, C, dk))
    k_decay = jnp.broadcast_to(jnp.exp(log_gamma[:, None] * (C - 1.0 - i)[None, :])[:, :, None], (n_heads, C, dk))
    chunk_decay = jnp.broadcast_to(jnp.exp(log_gamma * C)[:, None, None], (n_heads, 1, dv))
    nq = n_heads // hs
    per_head = lambda b, h: (h, 0, 0)
    return pl.pallas_call(
        _retention_kernel,
        grid=(B, nq),
        in_specs=[pl.BlockSpec((S, hs * dk), lambda b, h: (b, h)),
                  pl.BlockSpec((S, hs * dk), lambda b, h: (b, nq + h)),
                  pl.BlockSpec((S, hs * dv), lambda b, h: (b, nq + h)),
                  pl.BlockSpec((S, hs * dv), lambda b, h: (b, h)),
                  pl.BlockSpec((hs, C, C), per_head), pl.BlockSpec((hs, C, dk), per_head),
                  pl.BlockSpec((hs, C, dk), per_head), pl.BlockSpec((hs, 1, dv), per_head),
                  pl.BlockSpec((hs, 1, dv), per_head)],
        out_specs=pl.BlockSpec((S, hs * dv), lambda b, h: (b, h)),
        out_shape=jax.ShapeDtypeStruct((T, n_heads * dv), BF16),
        scratch_shapes=[pltpu.VMEM((hs, dk, dv), F32)],
        compiler_params=_cparams(("parallel", "arbitrary")),
        name="retention_core",
    )(qkv, qkv, qkv, g, intra, q_decay, k_decay, chunk_decay, out_norm.reshape(n_heads, 1, dv))


def _rot_tables(positions, dim, theta):
    inv = theta ** (-jnp.arange(0, dim, 2, dtype=F32) / dim)
    ang = positions.astype(F32).reshape(-1, 1) * inv[None, :]
    return jnp.cos(ang), jnp.sin(ang)


def dsa_layer(x2d, positions, gain, w_in, w_out, q_norm, k_norm, idx_k_norm, layer, B, S):
    T, D = x2d.shape
    n_heads = D // HEAD_DIM
    n_kv = n_heads // 4
    rot_dim = HEAD_DIM // 4
    cos, sin = _rot_tables(positions, rot_dim, ROPE_THETA)
    ones = jnp.ones((T, HEAD_DIM - rot_dim), F32)
    zeros_r = jnp.zeros((T, HEAD_DIM - rot_dim // 2), F32)
    tabs = (jnp.concatenate([cos, cos, ones], axis=1),
            jnp.concatenate([-sin, zeros_r], axis=1),
            jnp.concatenate([jnp.zeros_like(sin), sin, jnp.zeros((T, HEAD_DIM - rot_dim), F32)], axis=1))
    hn = rmsnorm_cast(x2d, gain)
    heads, wi = dsa_in_proj(hn, w_in, layer, tabs, q_norm, k_norm, idx_k_norm, n_heads, n_kv)
    top_k = min(IDX_TOPK_MAX, S // 4)
    o = dsa_attention_core(heads, jnp.transpose(wi[:, :IDX_HEADS]), n_heads, n_kv, B, S, top_k)
    return matmul(o, w_out, layer, res=x2d, head_major=True, tm=2048, tn=512, name="dsa_out_proj")


def retention_layer(x2d, positions, gain, w_in, w_out, out_norm, layer, B, S):
    T, D = x2d.shape
    n_heads = D // RET_QK_DIM
    cos, sin = _rot_tables(positions, RET_QK_DIM, RET_THETA)
    hn = rmsnorm_cast(x2d, gain)
    qkv, g = ret_in_proj(hn, w_in, layer, cos, sin, n_heads)
    og = retention_core(qkv, g, out_norm, B, S, n_heads)
    return matmul(og, w_out, layer, res=x2d, tm=1024, tn=256, name="ret_out_proj")


def kernel(x, p, positions, attn_norm, ffn_norm, ple_norm, a_w_in, a_w_out, a_q_norm, a_k_norm, a_idx_k_norm,
           b_w_in, b_w_out, b_out_norm, moe_w_group, moe_b_group, moe_w_expert, moe_b_expert, moe_w_gate,
           moe_w_up, moe_w_down, ple_proj, ple_gate):
    B, S, D = x.shape
    depth = p.shape[0]
    T = B * S
    x2d = x.reshape(T, D)
    p3d = p.reshape(depth, T, p.shape[-1])
    for i in range(depth):
        j = i // 2
        if i % 2 == 0:
            x2d = dsa_layer(x2d, positions, attn_norm[i], a_w_in, a_w_out, a_q_norm[j], a_k_norm[j],
                            a_idx_k_norm[j], j, B, S)
        else:
            x2d = retention_layer(x2d, positions, attn_norm[i], b_w_in, b_w_out, b_out_norm[j], j, B, S)
        x2d, xn = hier_moe_residual(x2d, ffn_norm[i], moe_w_group[i], moe_b_group[i], moe_w_expert[i],
                                    moe_b_expert[i], moe_w_gate, moe_w_up, moe_w_down, i, ple_norm[i])
        x2d = ple_residual(x2d, xn, ple_gate, p3d, ple_proj, i)
    return x2d.reshape(B, S, D)
```

```python
import functools

import jax
import jax.numpy as jnp
from jax import lax
from jax.experimental import pallas as pl
from jax.experimental.pallas import tpu as pltpu

F32 = jnp.float32
BF16 = jnp.bfloat16
I32 = jnp.int32

RMS_EPS = 1e-6
HEAD_DIM = 128
IDX_HEADS = 64
IDX_TOPK_MAX = 256
ROPE_THETA = 500000.0
RET_QK_DIM = 256
RET_THETA = 10000.0
RET_CHUNK = 128
N_GROUPS = 8
EXPERTS_PER_GROUP = 8
LANES = 128
VMEM_LIMIT_BYTES = 56 * 1024 * 1024
NEG_BIG = -1e30
INT_MIN = -2147483648


def _cparams(sem):
    return pltpu.CompilerParams(dimension_semantics=sem, vmem_limit_bytes=VMEM_LIMIT_BYTES)


def _pick(n, cands):
    for c in cands:
        if n % c == 0:
            return c
    raise ValueError(f"no tile for {n} in {cands}")


def _rmsnorm_kernel(x_ref, g_ref, o_ref):
    x = x_ref[...]
    ms = jnp.mean(x * x, axis=-1, keepdims=True)
    o_ref[...] = (x * lax.rsqrt(ms + RMS_EPS) * g_ref[...]).astype(o_ref.dtype)


def rmsnorm_cast(x2d, gain):
    T, D = x2d.shape
    tm = _pick(T, (512, 256, 128, 8))
    return pl.pallas_call(
        _rmsnorm_kernel,
        grid=(T // tm,),
        in_specs=[pl.BlockSpec((tm, D), lambda i: (i, 0)), pl.BlockSpec((1, D), lambda i: (0, 0))],
        out_specs=pl.BlockSpec((tm, D), lambda i: (i, 0)),
        out_shape=jax.ShapeDtypeStruct((T, D), BF16),
        compiler_params=_cparams(("parallel",)),
        name="rmsnorm_cast",
    )(x2d, gain.reshape(1, D))


ROW_CHUNK = 512


def _mm_kernel(*refs, head_major, has_res, row_chunk):
    a_ref, w_ref = refs[0], refs[1]
    r_ref = refs[2] if has_res else None
    o_ref = refs[2 + has_res]
    w = w_ref[...].astype(BF16)
    for rc in range(o_ref.shape[0] // row_chunk):
        rows = pl.ds(rc * row_chunk, row_chunk)
        if head_major:
            a = jnp.concatenate([a_ref[h, rows, :] for h in range(a_ref.shape[0])], axis=1)
        else:
            a = a_ref[rows, :]
        acc = jnp.dot(a, w, preferred_element_type=F32)
        if has_res:
            acc = acc + r_ref[rows, :]
        o_ref[rows, :] = acc.astype(o_ref.dtype)


def matmul(a, w, layer, *, col0=0, res=None, out_dtype=F32, tm=1024, tn=512, head_major=False, name="matmul"):
    if head_major:
        nh, M, hd = a.shape
        K = nh * hd
    else:
        M, K = a.shape
    n_out = w.shape[2] - col0
    tm = min(tm, M)
    assert col0 % tn == 0 and M % tm == 0 and n_out % tn == 0
    once = dict(pipeline_mode=pl.Buffered(1))
    if head_major:
        a_spec = pl.BlockSpec((nh, tm, hd), lambda i, j: (0, i, 0), **once)
    else:
        a_spec = pl.BlockSpec((tm, K), lambda i, j: (i, 0), **once)
    in_specs = [a_spec, pl.BlockSpec((None, K, tn), lambda i, j: (layer, 0, j + col0 // tn))]
    args = [a, w]
    if res is not None:
        in_specs.append(pl.BlockSpec((tm, tn), lambda i, j: (i, j)))
        args.append(res)
    return pl.pallas_call(
        functools.partial(_mm_kernel, head_major=head_major, has_res=res is not None,
                          row_chunk=min(ROW_CHUNK, tm)),
        grid=(M // tm, n_out // tn),
        in_specs=in_specs,
        out_specs=pl.BlockSpec((tm, tn), lambda i, j: (i, j)),
        out_shape=jax.ShapeDtypeStruct((M, n_out), out_dtype),
        compiler_params=_cparams(("parallel", "parallel")),
        name=name,
    )(*args)


def _rot_head(x, c, s1, s2):
    half = HEAD_DIM // 8
    return x * c + pltpu.roll(x, HEAD_DIM - half, 1) * s1 + pltpu.roll(x, half, 1) * s2


def _flag(pred, shape):
    return jnp.broadcast_to(pred.astype(I32), shape) == 1


def _dsa_proj_kernel(a_ref, w_ref, c_ref, s1_ref, s2_ref, gains_ref, heads_ref, wi_ref, *, n_valid, bounds,
                     wi_lane, wscale, row_chunk):
    j = pl.program_id(1)
    tm = a_ref.shape[0]
    tn = w_ref.shape[1]
    b_q, b_k, b_v, b_qi = bounds
    col = j * tn + lax.broadcasted_iota(I32, (1, tn), 1)
    w = jnp.where(col < n_valid, w_ref[...], 0.0).astype(BF16)
    is_q = j < b_q
    is_k = jnp.logical_and(j >= b_q, j < b_k)
    is_v = jnp.logical_and(j >= b_k, j < b_v)
    is_ki = j == b_qi
    lane = (1, HEAD_DIM)
    gain = jnp.where(_flag(is_q, lane), gains_ref[0:1, :],
                     jnp.where(_flag(is_k, lane), gains_ref[1:2, :],
                               jnp.where(_flag(is_ki, lane), gains_ref[2:3, :], 1.0)))
    use_norm = _flag(jnp.logical_or(jnp.logical_or(is_q, is_k), is_ki), (row_chunk, 1))
    no_rot = _flag(is_v, (row_chunk, HEAD_DIM))
    for rc in range(tm // row_chunk):
        rows = pl.ds(rc * row_chunk, row_chunk)
        acc = jnp.dot(a_ref[rows, :], w, preferred_element_type=F32)
        c = jnp.where(no_rot, 1.0, c_ref[rows, :])
        s1 = jnp.where(no_rot, 0.0, s1_ref[rows, :])
        s2 = jnp.where(no_rot, 0.0, s2_ref[rows, :])
        for hh in range(tn // HEAD_DIM):
            x = acc[:, hh * HEAD_DIM:(hh + 1) * HEAD_DIM]
            ms = jnp.mean(x * x, axis=-1, keepdims=True)
            x = x * jnp.where(use_norm, lax.rsqrt(ms + RMS_EPS), 1.0) * gain
            heads_ref[hh, rows, :] = _rot_head(x, c, s1, s2).astype(heads_ref.dtype)
        wi_ref[rows, :] = acc[:, wi_lane:wi_lane + HEAD_DIM] * wscale


def dsa_in_proj(hn, w_in, layer, tabs, q_gain, k_gain, ki_gain, n_heads, n_kv):
    T, K = hn.shape
    n_cols = w_in.shape[2]
    seg_ends = [n_heads * HEAD_DIM]
    for width in (n_kv * HEAD_DIM, n_kv * HEAD_DIM, IDX_HEADS * HEAD_DIM):
        seg_ends.append(seg_ends[-1] + width)
    tn = next(t for t in (512, 256, 128) if all(e % t == 0 for e in seg_ends))
    hb = tn // HEAD_DIM
    bounds = tuple(e // tn for e in seg_ends)
    b_v, b_qi = bounds[2], bounds[3]
    n_out = -(-n_cols // tn) * tn
    wi_lane = (seg_ends[-1] + HEAD_DIM) % tn
    assert n_cols == seg_ends[-1] + HEAD_DIM + IDX_HEADS and (n_out - tn) <= seg_ends[-1] + HEAD_DIM
    tm = min(2048, T)
    row_chunk = min(512, tm)
    c, s1, s2 = tabs
    gains = jnp.zeros((8, HEAD_DIM), F32).at[0].set(q_gain * (HEAD_DIM ** -0.5 * LOG2_E)).at[1].set(k_gain)
    gains = gains.at[2].set(ki_gain)
    once = dict(pipeline_mode=pl.Buffered(1))
    row = lambda i, j: (i, 0)

    def head_block(i, j):
        pos = jnp.where(j < b_v, j + (b_qi - b_v), jnp.where(j < b_qi, j - b_v, j))
        return (pos, i, 0)

    return pl.pallas_call(
        functools.partial(_dsa_proj_kernel, n_valid=n_cols, bounds=bounds, wi_lane=wi_lane,
                          wscale=IDX_HEADS ** -0.5 * HEAD_DIM ** -0.5, row_chunk=row_chunk),
        grid=(T // tm, n_out // tn),
        in_specs=[pl.BlockSpec((tm, K), row, **once),
                  pl.BlockSpec((None, K, tn), lambda i, j: (layer, 0, j)),
                  pl.BlockSpec((tm, HEAD_DIM), row, **once), pl.BlockSpec((tm, HEAD_DIM), row, **once),
                  pl.BlockSpec((tm, HEAD_DIM), row, **once), pl.BlockSpec((8, HEAD_DIM), lambda i, j: (0, 0))],
        out_specs=[pl.BlockSpec((hb, tm, HEAD_DIM), head_block), pl.BlockSpec((tm, HEAD_DIM), row)],
        out_shape=[jax.ShapeDtypeStruct((n_out // HEAD_DIM, T, HEAD_DIM), BF16),
                   jax.ShapeDtypeStruct((T, HEAD_DIM), F32)],
        compiler_params=_cparams(("parallel", "arbitrary")),
        name="dsa_in_proj",
    )(hn, w_in, c, s1, s2, gains)


_NT = (((1,), (1,)), ((), ()))


IDX_UNROLL = 4
LOG2_E = 1.4426950408889634
SOFTMAX_SHIFT_MAX = 40.0
NORM_SLACK = 1.0 + 2.0 ** -6


def _dsa_tile(bound_ref, qi_ref, ki_ref, wi_ref, q_ref, k_ref, v_ref, o_ref, sc_ref, u_ref, bias_ref,
              *, top_k, sub, steps_per_kv, nk, q0):
    tq = sc_ref.shape[1]
    keys = pl.ds(0, nk)

    def idx_term(h):
        s = lax.dot_general(ki_ref[keys, :], qi_ref[h], _NT, preferred_element_type=F32)
        return jnp.maximum(s, 0.0) * wi_ref[pl.ds(h, 1), :]

    def idx_terms(h0):
        acc = idx_term(h0)
        for d in range(1, IDX_UNROLL):
            acc = acc + idx_term(h0 + d)
        return acc

    sc_ref[keys, :] = idx_terms(0)

    def idx_body(g, carry):
        sc_ref[keys, :] += idx_terms(g * IDX_UNROLL)
        return carry

    lax.fori_loop(1, IDX_HEADS // IDX_UNROLL, idx_body, 0)

    key_pos = lax.broadcasted_iota(I32, (nk, tq), 0)
    q_pos = q0 + lax.broadcasted_iota(I32, (nk, tq), 1)
    causal = key_pos <= q_pos
    bits = pltpu.bitcast(sc_ref[keys, :], I32)
    u = bits ^ ((bits >> 31) & 0x7FFFFFFF)
    u_ref[keys, :] = jnp.where(causal, u, INT_MIN)

    def count_ge(cand):
        return jnp.sum(jnp.where(u_ref[keys, :] >= cand, 1.0, 0.0), axis=0, keepdims=True)

    zero = jnp.zeros((1, tq), I32)
    v0 = jnp.where(count_ge(zero) >= top_k, zero, jnp.full((1, tq), INT_MIN, I32))

    def bis_body(it, v):
        cand = v + lax.shift_left(jnp.int32(1), 30 - it)
        return jnp.where(count_ge(cand) >= top_k, cand, v)

    thr = lax.fori_loop(0, 31, bis_body, v0)
    keep = jnp.logical_and(u_ref[keys, :] >= thr, causal)
    bias_ref[:, keys] = jnp.transpose(jnp.where(keep, 0.0, NEG_BIG))

    bound = bound_ref[0]
    bound_ok = bound <= SOFTMAX_SHIFT_MAX

    def att_body(shift_by_bound, st, carry):
        n = st // steps_per_kv
        q = q_ref[pl.ds(st * sub, sub)].reshape(sub * tq, HEAD_DIM)
        lg = lax.dot_general(q, k_ref[n, keys, :], _NT, preferred_element_type=F32)
        ps, ls = [], []
        for g in range(sub):
            x = lg[g * tq:(g + 1) * tq] + bias_ref[:, keys]
            shift = bound if shift_by_bound else jnp.max(x, axis=-1, keepdims=True)
            p = jnp.exp2(x - shift)
            ls.append(jnp.sum(p, axis=-1, keepdims=True))
            ps.append(p.astype(BF16))
        o = jnp.dot(jnp.concatenate(ps, axis=0), v_ref[n, keys, :], preferred_element_type=F32)
        for g in range(sub):
            o_ref[st * sub + g] = (o[g * tq:(g + 1) * tq] / ls[g]).astype(o_ref.dtype)
        return carry

    n_steps = q_ref.shape[0] // sub

    @pl.when(bound_ok)
    def _():
        lax.fori_loop(0, n_steps, functools.partial(att_body, True), 0)

    @pl.when(jnp.logical_not(bound_ok))
    def _():
        lax.fori_loop(0, n_steps, functools.partial(att_body, False), 0)


def _dsa_kernel(*refs, **static):
    S, tq = refs[8].shape
    i = pl.program_id(1)
    for c in range(S // tq):
        @pl.when(i == c)
        def _(c=c):
            _dsa_tile(*refs, nk=(c + 1) * tq, q0=c * tq, **static)


def dsa_attention_core(heads, wi, logit_bound, n_heads, n_kv, B, S, top_k):
    T = heads.shape[1]
    tq = _pick(S, (256, 128))
    nq = S // tq
    group = n_heads // n_kv
    sub = 2 if group % 2 == 0 else 1
    off_q, off_k = IDX_HEADS, IDX_HEADS + n_heads
    off_v, off_ki = off_k + n_kv, off_k + 2 * n_kv
    assert off_q % n_heads == 0 and off_k % n_kv == 0 and off_v % n_kv == 0
    once = dict(pipeline_mode=pl.Buffered(1))
    return pl.pallas_call(
        functools.partial(_dsa_kernel, top_k=top_k, sub=sub, steps_per_kv=group // sub),
        grid=(B, nq),
        in_specs=[pl.BlockSpec(memory_space=pltpu.SMEM),
                  pl.BlockSpec((IDX_HEADS, tq, HEAD_DIM), lambda b, i: (0, b * nq + i, 0)),
                  pl.BlockSpec((None, S, HEAD_DIM), lambda b, i: (off_ki, b, 0), **once),
                  pl.BlockSpec((IDX_HEADS, tq), lambda b, i: (0, b * nq + i)),
                  pl.BlockSpec((n_heads, tq, HEAD_DIM), lambda b, i: (off_q // n_heads, b * nq + i, 0)),
                  pl.BlockSpec((n_kv, S, HEAD_DIM), lambda b, i: (off_k // n_kv, b, 0), **once),
                  pl.BlockSpec((n_kv, S, HEAD_DIM), lambda b, i: (off_v // n_kv, b, 0), **once)],
        out_specs=pl.BlockSpec((n_heads, tq, HEAD_DIM), lambda b, i: (0, b * nq + i, 0)),
        out_shape=jax.ShapeDtypeStruct((n_heads, T, HEAD_DIM), BF16),
        scratch_shapes=[pltpu.VMEM((S, tq), F32), pltpu.VMEM((S, tq), I32), pltpu.VMEM((tq, S), F32)],
        compiler_params=_cparams(("parallel", "arbitrary")),
        name="dsa_core",
    )(logit_bound, heads, heads, wi, heads, heads, heads)


def _route_kernel(x_ref, g_ref, w_ref, b_ref, o_ref):
    x = x_ref[...]
    ms = jnp.mean(x * x, axis=-1, keepdims=True)
    t = x * lax.rsqrt(ms + RMS_EPS) * g_ref[...]
    logits = jnp.dot(t, w_ref[...], precision=lax.Precision.HIGHEST, preferred_element_type=F32) + b_ref[...]
    lane = lax.broadcasted_iota(I32, logits.shape, 1)
    lane_f = lane.astype(F32)
    big = float(LANES)
    gl = jnp.where(lane < N_GROUPS, logits, -jnp.inf)
    gmax = jnp.max(gl, axis=-1, keepdims=True)
    g_sel = jnp.min(jnp.where(gl == gmax, lane_f, big), axis=-1, keepdims=True)
    g_w = 1.0 / jnp.sum(jnp.exp(gl - gmax), axis=-1, keepdims=True)
    lo = N_GROUPS + g_sel * EXPERTS_PER_GROUP
    in_grp = jnp.logical_and(lane_f >= lo, lane_f < lo + EXPERTS_PER_GROUP)
    el = jnp.where(in_grp, logits, -jnp.inf)
    m1 = jnp.max(el, axis=-1, keepdims=True)
    i1 = jnp.min(jnp.where(el == m1, lane_f, big), axis=-1, keepdims=True)
    el2 = jnp.where(lane_f == i1, -jnp.inf, el)
    m2 = jnp.max(el2, axis=-1, keepdims=True)
    i2 = jnp.min(jnp.where(el2 == m2, lane_f, big), axis=-1, keepdims=True)
    z = jnp.sum(jnp.exp(el - m1), axis=-1, keepdims=True)
    p1 = 1.0 / z
    p2 = jnp.exp(m2 - m1) / z
    w1 = p1 / (p1 + p2) * g_w
    w2 = p2 / (p1 + p2) * g_w
    out = jnp.where(lane == 0, i1 - N_GROUPS, jnp.where(lane == 1, i2 - N_GROUPS,
          jnp.where(lane == 2, w1, jnp.where(lane == 3, w2, 0.0))))
    o_ref[...] = out


def moe_route(x2d, gain, w_group, b_group, w_expert, b_expert):
    T, D = x2d.shape
    n_log = w_group.shape[1] + w_expert.shape[1]
    w = jnp.concatenate([w_group, w_expert, jnp.zeros((D, LANES - n_log), F32)], axis=1)
    b = jnp.concatenate([b_group, b_expert, jnp.zeros((LANES - n_log,), F32)]).reshape(1, LANES)
    tm = _pick(T, (256, 128, 8))
    return pl.pallas_call(
        _route_kernel,
        grid=(T // tm,),
        in_specs=[pl.BlockSpec((tm, D), lambda i: (i, 0)), pl.BlockSpec((1, D), lambda i: (0, 0)),
                  pl.BlockSpec((D, LANES), lambda i: (0, 0)), pl.BlockSpec((1, LANES), lambda i: (0, 0))],
        out_specs=pl.BlockSpec((tm, LANES), lambda i: (i, 0)),
        out_shape=jax.ShapeDtypeStruct((T, LANES), F32),
        compiler_params=_cparams(("parallel",)),
        name="moe_route",
    )(x2d, gain.reshape(1, D), w, b)


def _row_copy(src_hbm, dst_ref, sem, src_row, dst_row):
    return pltpu.make_async_copy(src_hbm.at[pl.ds(src_row, 1), :], dst_ref.at[pl.ds(dst_row, 1), :], sem)


DMA_PRIORITIES = 2


def _start_row_gather(idx_ref, src_hbm, dst_ref, sem, base, n_rows, priorities=(0, 1)):
    def start(g, c):
        for d, pr in enumerate(priorities):
            r = g * len(priorities) + d
            _row_copy(src_hbm, dst_ref, sem, idx_ref[base + r], r).start(priority=pr)
        return c

    lax.fori_loop(0, n_rows // len(priorities), start, 0)


def _wait_row_gather(src_hbm, dst_ref, sem, n_rows):
    pltpu.make_async_copy(src_hbm.at[pl.ds(0, n_rows), :], dst_ref, sem).wait()


FFN_SLOTS = 3
FFN_GATHER_PRIORITIES = (1, 1)


def _ffn_kernel(src_ref, te_ref, nu_ref, x_hbm, gate_ref, g_ref, wg_ref, wu_ref, wd_ref, o_ref, xbuf, sem, *,
                priorities):
    j = pl.program_id(0)
    tm = o_ref.shape[0]
    n_used = nu_ref[0]
    ahead = FFN_SLOTS - 1

    def fetch(tile, slot):
        _start_row_gather(src_ref, x_hbm, xbuf.at[slot], sem.at[slot], tile * tm, tm, priorities)

    for t in range(ahead):
        @pl.when(jnp.logical_and(j == 0, t < n_used))
        def _(t=t):
            fetch(t, t)

    @pl.when(j + ahead < n_used)
    def _():
        fetch(j + ahead, lax.rem(j + ahead, FFN_SLOTS))

    @pl.when(j < n_used)
    def _():
        slot = lax.rem(j, FFN_SLOTS)
        _wait_row_gather(x_hbm, xbuf.at[slot], sem.at[slot], tm)
        x = xbuf[slot]
        ms = jnp.mean(x * x, axis=-1, keepdims=True)
        t = (x * lax.rsqrt(ms + RMS_EPS) * g_ref[...]).astype(BF16)
        a = jnp.dot(t, wg_ref[0].astype(BF16), preferred_element_type=F32)
        u = jnp.dot(t, wu_ref[0].astype(BF16), preferred_element_type=F32)
        hid = a * jax.nn.sigmoid(a) * u * gate_ref[...]
        o_ref[...] = jnp.dot(hid.astype(BF16), wd_ref[0].astype(BF16), preferred_element_type=F32)

    @pl.when(j >= n_used)
    def _():
        o_ref[...] = jnp.zeros_like(o_ref)


def moe_ffn(x2d, src_tok, gate_s, gain, w_gate, w_up, w_down, tile_expert, n_used, tm, priorities):
    D = x2d.shape[1]
    P = src_tok.shape[0]
    F = w_gate.shape[-1]
    assert P % tm == 0 and tm % DMA_PRIORITIES == 0
    wmap = lambda j, src, te, nu: (te[j], 0, 0)
    return pl.pallas_call(
        functools.partial(_ffn_kernel, priorities=priorities),
        grid_spec=pltpu.PrefetchScalarGridSpec(
            num_scalar_prefetch=3, grid=(P // tm,),
            in_specs=[pl.BlockSpec(memory_space=pl.ANY),
                      pl.BlockSpec((tm, 1), lambda j, src, te, nu: (j, 0)),
                      pl.BlockSpec((1, D), lambda j, src, te, nu: (0, 0)),
                      pl.BlockSpec((1, D, F), wmap), pl.BlockSpec((1, D, F), wmap),
                      pl.BlockSpec((1, F, D), wmap)],
            out_specs=pl.BlockSpec((tm, D), lambda j, src, te, nu: (j, 0)),
            scratch_shapes=[pltpu.VMEM((FFN_SLOTS, tm, D), F32), pltpu.SemaphoreType.DMA((FFN_SLOTS,))]),
        out_shape=jax.ShapeDtypeStruct((P, D), F32),
        compiler_params=_cparams(("arbitrary",)),
        name="moe_ffn",
    )(src_tok, tile_expert, n_used, x2d, gate_s, gain.reshape(1, D), w_gate, w_up, w_down)


def _combine_kernel(p1_ref, p2_ref, ys_hbm, x_ref, gain_ref, o_ref, on_ref, buf, sem):
    i = pl.program_id(0)
    R = x_ref.shape[0]

    def fetch(tile, slot):
        _start_row_gather(p1_ref, ys_hbm, buf.at[2 * slot], sem.at[2 * slot], tile * R, R)
        _start_row_gather(p2_ref, ys_hbm, buf.at[2 * slot + 1], sem.at[2 * slot + 1], tile * R, R)

    @pl.when(i == 0)
    def _():
        fetch(0, 0)

    @pl.when(i + 1 < pl.num_programs(0))
    def _():
        fetch(i + 1, lax.rem(i + 1, 2))

    slot = lax.rem(i, 2)
    _wait_row_gather(ys_hbm, buf.at[2 * slot], sem.at[2 * slot], R)
    _wait_row_gather(ys_hbm, buf.at[2 * slot + 1], sem.at[2 * slot + 1], R)
    y = x_ref[...] + (buf[2 * slot] + buf[2 * slot + 1])
    o_ref[...] = y
    ms = jnp.mean(y * y, axis=-1, keepdims=True)
    on_ref[...] = (y * lax.rsqrt(ms + RMS_EPS) * gain_ref[...]).astype(on_ref.dtype)


def moe_combine(x2d, ys, pos1, pos2, next_gain, rows_per_step):
    T, D = x2d.shape
    R = rows_per_step
    tile = lambda i, a, b: (i, 0)
    return pl.pallas_call(
        _combine_kernel,
        grid_spec=pltpu.PrefetchScalarGridSpec(
            num_scalar_prefetch=2, grid=(T // R,),
            in_specs=[pl.BlockSpec(memory_space=pl.ANY), pl.BlockSpec((R, D), tile),
                      pl.BlockSpec((1, D), lambda i, a, b: (0, 0))],
            out_specs=[pl.BlockSpec((R, D), tile), pl.BlockSpec((R, D), tile)],
            scratch_shapes=[pltpu.VMEM((4, R, D), F32), pltpu.SemaphoreType.DMA((4,))]),
        out_shape=[jax.ShapeDtypeStruct((T, D), F32), jax.ShapeDtypeStruct((T, D), BF16)],
        compiler_params=_cparams(("arbitrary",)),
        name="moe_combine",
    )(pos1, pos2, ys, x2d, next_gain.reshape(1, D))


def hier_moe_residual(x2d, gain, w_group, b_group, w_expert, b_expert, w_gate, w_up, w_down, layer, next_gain):
    T, D = x2d.shape
    depth, G, E = w_gate.shape[0], w_gate.shape[1], w_gate.shape[2]
    n_exp = G * E
    F = w_gate.shape[-1]
    route = moe_route(x2d, gain, w_group, b_group, w_expert, b_expert)
    eid = route[:, 0:2].astype(I32).reshape(2 * T)
    wts = route[:, 2:4].reshape(2 * T)
    tm = 256 if 2 * T >= 4096 else 128
    nt = (2 * T) // tm + n_exp
    onehot = (eid[None, :] == jnp.arange(n_exp, dtype=I32)[:, None]).astype(I32)
    csum = jnp.cumsum(onehot, axis=1)
    counts = csum[:, -1]
    rank = jnp.sum(onehot * csum, axis=0) - 1
    tiles = (counts + tm - 1) // tm
    tile_end = jnp.cumsum(tiles)
    tile_start = tile_end - tiles
    n_used = tile_end[-1:]
    j = jnp.arange(nt, dtype=I32)
    tile_expert = jnp.minimum(jnp.sum(tile_end[None, :] <= j[:, None], axis=1, dtype=I32), n_exp - 1)
    last_expert = jnp.sum(tile_end < n_used[0], dtype=I32)
    tile_expert = jnp.where(j < n_used[0], tile_expert, last_expert)
    pos = (tile_start[eid] * tm + rank).astype(I32)
    inv = jnp.full((nt * tm,), -1, I32).at[pos].set(jnp.arange(2 * T, dtype=I32))
    valid = inv >= 0
    src_pair = jnp.maximum(inv, 0)
    src_tok = jnp.where(valid, src_pair // 2, 0)
    gate_s = jnp.where(valid, wts[src_pair], 0.0).reshape(nt * tm, 1)
    pos = pos.reshape(T, 2)

    ys = moe_ffn(x2d, src_tok, gate_s, gain, w_gate.reshape(depth * n_exp, D, F),
                 w_up.reshape(depth * n_exp, D, F), w_down.reshape(depth * n_exp, F, D),
                 tile_expert + layer * n_exp, n_used, tm, FFN_GATHER_PRIORITIES)
    return moe_combine(x2d, ys, pos[:, 0], pos[:, 1], next_gain, min(256, T))


def _ple_kernel(a_ref, w_ref, p_ref, pw_ref, x_ref, o_ref, *, row_chunk):
    w = w_ref[...].astype(BF16)
    pw = pw_ref[...].astype(BF16)
    for rc in range(o_ref.shape[0] // row_chunk):
        rows = pl.ds(rc * row_chunk, row_chunk)
        z = jnp.dot(a_ref[rows, :], w, preferred_element_type=F32)
        e = jnp.dot(p_ref[rows, :].astype(BF16), pw, preferred_element_type=F32)
        o_ref[rows, :] = x_ref[rows, :] + jax.nn.sigmoid(z) * e


def ple_residual(x2d, a, w_gate, p3d, w_proj, layer):
    T, D = x2d.shape
    P = p3d.shape[2]
    tm, tn = min(1024, T), 512
    return pl.pallas_call(
        functools.partial(_ple_kernel, row_chunk=min(ROW_CHUNK, tm)),
        grid=(T // tm, D // tn),
        in_specs=[pl.BlockSpec((tm, D), lambda i, j: (i, 0), pipeline_mode=pl.Buffered(1)),
                  pl.BlockSpec((None, D, tn), lambda i, j: (layer, 0, j)),
                  pl.BlockSpec((None, tm, P), lambda i, j: (layer, i, 0)),
                  pl.BlockSpec((None, P, tn), lambda i, j: (layer, 0, j)),
                  pl.BlockSpec((tm, tn), lambda i, j: (i, j))],
        out_specs=pl.BlockSpec((tm, tn), lambda i, j: (i, j)),
        out_shape=jax.ShapeDtypeStruct((T, D), F32),
        compiler_params=_cparams(("parallel", "parallel")),
        name="ple",
    )(a, w_gate, p3d, w_proj, x2d)


def _ret_proj_kernel(a_ref, w_ref, cos_ref, sin_ref, qkv_ref, *, n_qk, row_chunk):
    j = pl.program_id(1)
    tm = a_ref.shape[0]
    tn = w_ref.shape[1]
    half = RET_QK_DIM // 2
    w = w_ref[...].astype(BF16)
    shape = (row_chunk, half)
    is_rot = _flag(j < n_qk, shape)
    kscale = jnp.where(_flag(jnp.logical_and(j >= n_qk // 2, j < n_qk), shape), RET_QK_DIM ** -0.5, 1.0)
    for rc in range(tm // row_chunk):
        rows = pl.ds(rc * row_chunk, row_chunk)
        acc = jnp.dot(a_ref[rows, :], w, preferred_element_type=F32)
        cos = jnp.where(is_rot, cos_ref[rows, :], 1.0) * kscale
        sin = jnp.where(is_rot, sin_ref[rows, :], 0.0) * kscale
        for hh in range(tn // RET_QK_DIM):
            x1 = acc[:, hh * RET_QK_DIM:hh * RET_QK_DIM + half]
            x2 = acc[:, hh * RET_QK_DIM + half:(hh + 1) * RET_QK_DIM]
            qkv_ref[rows, hh * RET_QK_DIM:hh * RET_QK_DIM + half] = (x1 * cos - x2 * sin).astype(BF16)
            qkv_ref[rows, hh * RET_QK_DIM + half:(hh + 1) * RET_QK_DIM] = (x1 * sin + x2 * cos).astype(BF16)


def ret_in_proj(hn, w_in, layer, cos, sin, n_heads):
    T, K = hn.shape
    dk = RET_QK_DIM
    qkv_cols = 4 * n_heads * dk
    tn = 512 if (n_heads * dk) % 512 == 0 else dk
    tm = min(2048, T)
    once = dict(pipeline_mode=pl.Buffered(1))
    row = lambda i, j: (i, 0)
    qkv = pl.pallas_call(
        functools.partial(_ret_proj_kernel, n_qk=2 * n_heads * dk // tn, row_chunk=min(512, tm)),
        grid=(T // tm, qkv_cols // tn),
        in_specs=[pl.BlockSpec((tm, K), row, **once),
                  pl.BlockSpec((None, K, tn), lambda i, j: (layer, 0, j)),
                  pl.BlockSpec((tm, dk // 2), row, **once), pl.BlockSpec((tm, dk // 2), row, **once)],
        out_specs=pl.BlockSpec((tm, tn), lambda i, j: (i, j)),
        out_shape=jax.ShapeDtypeStruct((T, qkv_cols), BF16),
        compiler_params=_cparams(("parallel", "parallel")),
        name="ret_qkv_proj",
    )(hn, w_in, cos, sin)
    g = matmul(hn, w_in, layer, col0=qkv_cols, tm=tm, tn=tn, name="ret_g_proj")
    return qkv, g


RET_HEADS_PER_STEP = 2


def _retention_kernel(q_ref, k_ref, v_ref, g_ref, intra_ref, qd_ref, kd_ref, cd_ref, gn_ref, o_ref, state_ref):
    S = q_ref.shape[0]
    C = intra_ref.shape[1]
    dk = RET_QK_DIM
    dv = 2 * dk
    state_ref[...] = jnp.zeros_like(state_ref)

    def chunk(c, carry):
        rows = pl.ds(pl.multiple_of(c * C, C), C)
        for hh in range(RET_HEADS_PER_STEP):
            qc = q_ref[rows, hh * dk:(hh + 1) * dk]
            kc = k_ref[rows, hh * dk:(hh + 1) * dk]
            vc = v_ref[rows, hh * dv:(hh + 1) * dv]
            att = lax.dot_general(qc, kc, _NT, preferred_element_type=F32) * intra_ref[hh]
            inner = jnp.dot(att.astype(BF16), vc, preferred_element_type=F32)
            state = state_ref[hh]
            cross = jnp.dot((qc.astype(F32) * qd_ref[hh]).astype(BF16), state.astype(BF16),
                            preferred_element_type=F32)
            kt = jnp.transpose(kc.astype(F32) * kd_ref[hh]).astype(BF16)
            state_ref[hh] = state * cd_ref[hh] + jnp.dot(kt, vc, preferred_element_type=F32)
            o = inner + cross
            o = o * lax.rsqrt(jnp.mean(o * o, axis=-1, keepdims=True) + RMS_EPS) * gn_ref[hh]
            g = g_ref[rows, hh * dv:(hh + 1) * dv]
            o_ref[rows, hh * dv:(hh + 1) * dv] = (g * jax.nn.sigmoid(g) * o).astype(o_ref.dtype)
        return carry

    lax.fori_loop(0, S // C, chunk, 0)


def retention_core(qkv, g, out_norm, B, S, n_heads):
    T = qkv.shape[0]
    dk = RET_QK_DIM
    dv = 2 * dk
    C = RET_CHUNK
    hs = RET_HEADS_PER_STEP
    assert n_heads % hs == 0
    log_gamma = jnp.log1p(-jnp.exp2(-5.0 - jnp.arange(n_heads, dtype=F32)))
    i = jnp.arange(C, dtype=F32)
    dist = i[:, None] - i[None, :]
    intra = jnp.where(dist[None] >= 0, jnp.exp(log_gamma[:, None, None] * jnp.maximum(dist, 0.0)[None]), 0.0)
    q_decay = jnp.broadcast_to(jnp.exp(log_gamma[:, None] * (i[None, :] + 1.0))[:, :, None], (n_heads, C, dk))
    k_decay = jnp.broadcast_to(jnp.exp(log_gamma[:, None] * (C - 1.0 - i)[None, :])[:, :, None], (n_heads, C, dk))
    chunk_decay = jnp.broadcast_to(jnp.exp(log_gamma * C)[:, None, None], (n_heads, 1, dv))
    nq = n_heads // hs
    per_head = lambda b, h: (h, 0, 0)
    return pl.pallas_call(
        _retention_kernel,
        grid=(B, nq),
        in_specs=[pl.BlockSpec((S, hs * dk), lambda b, h: (b, h)),
                  pl.BlockSpec((S, hs * dk), lambda b, h: (b, nq + h)),
                  pl.BlockSpec((S, hs * dv), lambda b, h: (b, nq + h)),
                  pl.BlockSpec((S, hs * dv), lambda b, h: (b, h)),
                  pl.BlockSpec((hs, C, C), per_head), pl.BlockSpec((hs, C, dk), per_head),
                  pl.BlockSpec((hs, C, dk), per_head), pl.BlockSpec((hs, 1, dv), per_head),
                  pl.BlockSpec((hs, 1, dv), per_head)],
        out_specs=pl.BlockSpec((S, hs * dv), lambda b, h: (b, h)),
        out_shape=jax.ShapeDtypeStruct((T, n_heads * dv), BF16),
        scratch_shapes=[pltpu.VMEM((hs, dk, dv), F32)],
        compiler_params=_cparams(("parallel", "arbitrary")),
        name="retention_core",
    )(qkv, qkv, qkv, g, intra, q_decay, k_decay, chunk_decay, out_norm.reshape(n_heads, 1, dv))


def _rot_tables(positions, dim, theta):
    inv = theta ** (-jnp.arange(0, dim, 2, dtype=F32) / dim)
    ang = positions.astype(F32).reshape(-1, 1) * inv[None, :]
    return jnp.cos(ang), jnp.sin(ang)


def dsa_layer(x2d, positions, gain, w_in, w_out, q_norm, k_norm, idx_k_norm, layer, B, S):
    T, D = x2d.shape
    n_heads = D // HEAD_DIM
    n_kv = n_heads // 4
    rot_dim = HEAD_DIM // 4
    cos, sin = _rot_tables(positions, rot_dim, ROPE_THETA)
    ones = jnp.ones((T, HEAD_DIM - rot_dim), F32)
    zeros_r = jnp.zeros((T, HEAD_DIM - rot_dim // 2), F32)
    tabs = (jnp.concatenate([cos, cos, ones], axis=1),
            jnp.concatenate([-sin, zeros_r], axis=1),
            jnp.concatenate([jnp.zeros_like(sin), sin, jnp.zeros((T, HEAD_DIM - rot_dim), F32)], axis=1))
    hn = rmsnorm_cast(x2d, gain)
    heads, wi = dsa_in_proj(hn, w_in, layer, tabs, q_norm, k_norm, idx_k_norm, n_heads, n_kv)
    top_k = min(IDX_TOPK_MAX, S // 4)
    q_cap = HEAD_DIM ** 0.5 * jnp.max(jnp.abs(q_norm)) * (HEAD_DIM ** -0.5 * LOG2_E) * NORM_SLACK
    k_cap = HEAD_DIM ** 0.5 * jnp.max(jnp.abs(k_norm)) * NORM_SLACK
    o = dsa_attention_core(heads, jnp.transpose(wi[:, :IDX_HEADS]), (q_cap * k_cap).reshape(1), n_heads, n_kv,
                           B, S, top_k)
    return matmul(o, w_out, layer, res=x2d, head_major=True, tm=2048, tn=512, name="dsa_out_proj")


def retention_layer(x2d, positions, gain, w_in, w_out, out_norm, layer, B, S):
    T, D = x2d.shape
    n_heads = D // RET_QK_DIM
    cos, sin = _rot_tables(positions, RET_QK_DIM, RET_THETA)
    hn = rmsnorm_cast(x2d, gain)
    qkv, g = ret_in_proj(hn, w_in, layer, cos, sin, n_heads)
    og = retention_core(qkv, g, out_norm, B, S, n_heads)
    return matmul(og, w_out, layer, res=x2d, tm=1024, tn=256, name="ret_out_proj")


def kernel(x, p, positions, attn_norm, ffn_norm, ple_norm, a_w_in, a_w_out, a_q_norm, a_k_norm, a_idx_k_norm,
           b_w_in, b_w_out, b_out_norm, moe_w_group, moe_b_group, moe_w_expert, moe_b_expert, moe_w_gate,
           moe_w_up, moe_w_down, ple_proj, ple_gate):
    B, S, D = x.shape
    depth = p.shape[0]
    T = B * S
    x2d = x.reshape(T, D)
    p3d = p.reshape(depth, T, p.shape[-1])
    for i in range(depth):
        j = i // 2
        if i % 2 == 0:
            x2d = dsa_layer(x2d, positions, attn_norm[i], a_w_in, a_w_out, a_q_norm[j], a_k_norm[j],
                            a_idx_k_norm[j], j, B, S)
        else:
            x2d = retention_layer(x2d, positions, attn_norm[i], b_w_in, b_w_out, b_out_norm[j], j, B, S)
        x2d, xn = hier_moe_residual(x2d, ffn_norm[i], moe_w_group[i], moe_b_group[i], moe_w_expert[i],
                                    moe_b_expert[i], moe_w_gate, moe_w_up, moe_w_down, i, ple_norm[i])
        x2d = ple_residual(x2d, xn, ple_gate, p3d, ple_proj, i)
    return x2d.reshape(B, S, D)
```

```python
import functools

import jax
import jax.numpy as jnp
from jax import lax
from jax.experimental import pallas as pl
from jax.experimental.pallas import tpu as pltpu

F32 = jnp.float32
BF16 = jnp.bfloat16
I32 = jnp.int32

RMS_EPS = 1e-6
HEAD_DIM = 128
IDX_HEADS = 64
IDX_TOPK_MAX = 256
ROPE_THETA = 500000.0
RET_QK_DIM = 256
RET_THETA = 10000.0
RET_CHUNK = 128
N_GROUPS = 8
EXPERTS_PER_GROUP = 8
LANES = 128
VMEM_LIMIT_BYTES = 56 * 1024 * 1024
NEG_BIG = -1e30
INT_MIN = -2147483648


def _cparams(sem):
    return pltpu.CompilerParams(dimension_semantics=sem, vmem_limit_bytes=VMEM_LIMIT_BYTES)


def _pick(n, cands):
    for c in cands:
        if n % c == 0:
            return c
    raise ValueError(f"no tile for {n} in {cands}")


def _rmsnorm_kernel(x_ref, g_ref, o_ref):
    x = x_ref[...]
    ms = jnp.mean(x * x, axis=-1, keepdims=True)
    o_ref[...] = (x * lax.rsqrt(ms + RMS_EPS) * g_ref[...]).astype(o_ref.dtype)


def rmsnorm_cast(x2d, gain):
    T, D = x2d.shape
    tm = _pick(T, (512, 256, 128, 8))
    return pl.pallas_call(
        _rmsnorm_kernel,
        grid=(T // tm,),
        in_specs=[pl.BlockSpec((tm, D), lambda i: (i, 0)), pl.BlockSpec((1, D), lambda i: (0, 0))],
        out_specs=pl.BlockSpec((tm, D), lambda i: (i, 0)),
        out_shape=jax.ShapeDtypeStruct((T, D), BF16),
        compiler_params=_cparams(("parallel",)),
        name="rmsnorm_cast",
    )(x2d, gain.reshape(1, D))


ROW_CHUNK = 512


def _mm_kernel(*refs, head_major, has_res, row_chunk):
    a_ref, w_ref = refs[0], refs[1]
    r_ref = refs[2] if has_res else None
    o_ref = refs[2 + has_res]
    w = w_ref[...].astype(BF16)
    for rc in range(o_ref.shape[0] // row_chunk):
        rows = pl.ds(rc * row_chunk, row_chunk)
        if head_major:
            a = jnp.concatenate([a_ref[h, rows, :] for h in range(a_ref.shape[0])], axis=1)
        else:
            a = a_ref[rows, :]
        acc = jnp.dot(a, w, preferred_element_type=F32)
        if has_res:
            acc = acc + r_ref[rows, :]
        o_ref[rows, :] = acc.astype(o_ref.dtype)


def matmul(a, w, layer, *, col0=0, res=None, out_dtype=F32, tm=1024, tn=512, head_major=False, name="matmul"):
    if head_major:
        nh, M, hd = a.shape
        K = nh * hd
    else:
        M, K = a.shape
    n_out = w.shape[2] - col0
    tm = min(tm, M)
    assert col0 % tn == 0 and M % tm == 0 and n_out % tn == 0
    once = dict(pipeline_mode=pl.Buffered(1))
    if head_major:
        a_spec = pl.BlockSpec((nh, tm, hd), lambda i, j: (0, i, 0), **once)
    else:
        a_spec = pl.BlockSpec((tm, K), lambda i, j: (i, 0), **once)
    in_specs = [a_spec, pl.BlockSpec((None, K, tn), lambda i, j: (layer, 0, j + col0 // tn))]
    args = [a, w]
    if res is not None:
        in_specs.append(pl.BlockSpec((tm, tn), lambda i, j: (i, j)))
        args.append(res)
    return pl.pallas_call(
        functools.partial(_mm_kernel, head_major=head_major, has_res=res is not None,
                          row_chunk=min(ROW_CHUNK, tm)),
        grid=(M // tm, n_out // tn),
        in_specs=in_specs,
        out_specs=pl.BlockSpec((tm, tn), lambda i, j: (i, j)),
        out_shape=jax.ShapeDtypeStruct((M, n_out), out_dtype),
        compiler_params=_cparams(("parallel", "parallel")),
        name=name,
    )(*args)


def _rot_head(x, c, s1, s2):
    half = HEAD_DIM // 8
    return x * c + pltpu.roll(x, HEAD_DIM - half, 1) * s1 + pltpu.roll(x, half, 1) * s2


def _flag(pred, shape):
    return jnp.broadcast_to(pred.astype(I32), shape) == 1


def _dsa_proj_kernel(a_ref, w_ref, c_ref, s1_ref, s2_ref, gains_ref, heads_ref, wi_ref, *, n_valid, bounds,
                     wi_lane, wscale, row_chunk):
    j = pl.program_id(1)
    tm = a_ref.shape[0]
    tn = w_ref.shape[1]
    b_q, b_k, b_v, b_qi = bounds
    col = j * tn + lax.broadcasted_iota(I32, (1, tn), 1)
    w = jnp.where(col < n_valid, w_ref[...], 0.0).astype(BF16)
    is_q = j < b_q
    is_k = jnp.logical_and(j >= b_q, j < b_k)
    is_v = jnp.logical_and(j >= b_k, j < b_v)
    is_ki = j == b_qi
    lane = (1, HEAD_DIM)
    gain = jnp.where(_flag(is_q, lane), gains_ref[0:1, :],
                     jnp.where(_flag(is_k, lane), gains_ref[1:2, :],
                               jnp.where(_flag(is_ki, lane), gains_ref[2:3, :], 1.0)))
    use_norm = _flag(jnp.logical_or(jnp.logical_or(is_q, is_k), is_ki), (row_chunk, 1))
    no_rot = _flag(is_v, (row_chunk, HEAD_DIM))
    for rc in range(tm // row_chunk):
        rows = pl.ds(rc * row_chunk, row_chunk)
        acc = jnp.dot(a_ref[rows, :], w, preferred_element_type=F32)
        c = jnp.where(no_rot, 1.0, c_ref[rows, :])
        s1 = jnp.where(no_rot, 0.0, s1_ref[rows, :])
        s2 = jnp.where(no_rot, 0.0, s2_ref[rows, :])
        for hh in range(tn // HEAD_DIM):
            x = acc[:, hh * HEAD_DIM:(hh + 1) * HEAD_DIM]
            ms = jnp.mean(x * x, axis=-1, keepdims=True)
            x = x * jnp.where(use_norm, lax.rsqrt(ms + RMS_EPS), 1.0) * gain
            heads_ref[hh, rows, :] = _rot_head(x, c, s1, s2).astype(heads_ref.dtype)
        wi_ref[rows, :] = acc[:, wi_lane:wi_lane + HEAD_DIM] * wscale


def dsa_in_proj(hn, w_in, layer, tabs, q_gain, k_gain, ki_gain, n_heads, n_kv):
    T, K = hn.shape
    n_cols = w_in.shape[2]
    seg_ends = [n_heads * HEAD_DIM]
    for width in (n_kv * HEAD_DIM, n_kv * HEAD_DIM, IDX_HEADS * HEAD_DIM):
        seg_ends.append(seg_ends[-1] + width)
    tn = next(t for t in (512, 256, 128) if all(e % t == 0 for e in seg_ends))
    hb = tn // HEAD_DIM
    bounds = tuple(e // tn for e in seg_ends)
    b_v, b_qi = bounds[2], bounds[3]
    n_out = -(-n_cols // tn) * tn
    wi_lane = (seg_ends[-1] + HEAD_DIM) % tn
    assert n_cols == seg_ends[-1] + HEAD_DIM + IDX_HEADS and (n_out - tn) <= seg_ends[-1] + HEAD_DIM
    tm = min(2048, T)
    row_chunk = min(512, tm)
    c, s1, s2 = tabs
    gains = jnp.zeros((8, HEAD_DIM), F32).at[0].set(q_gain * (HEAD_DIM ** -0.5 * LOG2_E)).at[1].set(k_gain)
    gains = gains.at[2].set(ki_gain)
    once = dict(pipeline_mode=pl.Buffered(1))
    row = lambda i, j: (i, 0)

    def head_block(i, j):
        pos = jnp.where(j < b_v, j + (b_qi - b_v), jnp.where(j < b_qi, j - b_v, j))
        return (pos, i, 0)

    return pl.pallas_call(
        functools.partial(_dsa_proj_kernel, n_valid=n_cols, bounds=bounds, wi_lane=wi_lane,
                          wscale=IDX_HEADS ** -0.5 * HEAD_DIM ** -0.5, row_chunk=row_chunk),
        grid=(T // tm, n_out // tn),
        in_specs=[pl.BlockSpec((tm, K), row, **once),
                  pl.BlockSpec((None, K, tn), lambda i, j: (layer, 0, j)),
                  pl.BlockSpec((tm, HEAD_DIM), row, **once), pl.BlockSpec((tm, HEAD_DIM), row, **once),
                  pl.BlockSpec((tm, HEAD_DIM), row, **once), pl.BlockSpec((8, HEAD_DIM), lambda i, j: (0, 0))],
        out_specs=[pl.BlockSpec((hb, tm, HEAD_DIM), head_block), pl.BlockSpec((tm, HEAD_DIM), row)],
        out_shape=[jax.ShapeDtypeStruct((n_out // HEAD_DIM, T, HEAD_DIM), BF16),
                   jax.ShapeDtypeStruct((T, HEAD_DIM), F32)],
        compiler_params=_cparams(("parallel", "arbitrary")),
        name="dsa_in_proj",
    )(hn, w_in, c, s1, s2, gains)


_NT = (((1,), (1,)), ((), ()))


IDX_UNROLL = 4
LOG2_E = 1.4426950408889634
SOFTMAX_SHIFT_MAX = 40.0
NORM_SLACK = 1.0 + 2.0 ** -6


def _dsa_tile(bound_ref, qi_ref, ki_ref, wi_ref, q_ref, k_ref, v_ref, o_ref, sc_ref, u_ref, bias_ref,
              *, top_k, sub, steps_per_kv, nk, q0):
    tq = sc_ref.shape[1]
    keys = pl.ds(0, nk)

    def idx_term(h):
        s = lax.dot_general(ki_ref[keys, :], qi_ref[h], _NT, preferred_element_type=F32)
        return jnp.maximum(s, 0.0) * wi_ref[pl.ds(h, 1), :]

    def idx_terms(h0):
        acc = idx_term(h0)
        for d in range(1, IDX_UNROLL):
            acc = acc + idx_term(h0 + d)
        return acc

    sc_ref[keys, :] = idx_terms(0)

    def idx_body(g, carry):
        sc_ref[keys, :] += idx_terms(g * IDX_UNROLL)
        return carry

    lax.fori_loop(1, IDX_HEADS // IDX_UNROLL, idx_body, 0)

    key_pos = lax.broadcasted_iota(I32, (nk, tq), 0)
    q_pos = q0 + lax.broadcasted_iota(I32, (nk, tq), 1)
    causal = key_pos <= q_pos
    bits = pltpu.bitcast(sc_ref[keys, :], I32)
    u = bits ^ ((bits >> 31) & 0x7FFFFFFF)
    u_ref[keys, :] = jnp.where(causal, u, INT_MIN)

    def count_ge(cand):
        return jnp.sum(jnp.where(u_ref[keys, :] >= cand, 1.0, 0.0), axis=0, keepdims=True)

    zero = jnp.zeros((1, tq), I32)
    v0 = jnp.where(count_ge(zero) >= top_k, zero, jnp.full((1, tq), INT_MIN, I32))

    def bis_body(it, v):
        cand = v + lax.shift_left(jnp.int32(1), 30 - it)
        return jnp.where(count_ge(cand) >= top_k, cand, v)

    thr = lax.fori_loop(0, 31, bis_body, v0)
    keep = jnp.logical_and(u_ref[keys, :] >= thr, causal)
    bias_ref[:, keys] = jnp.transpose(jnp.where(keep, 0.0, NEG_BIG))

    bound = bound_ref[0]
    bound_ok = bound <= SOFTMAX_SHIFT_MAX

    def att_body(shift_by_bound, st, carry):
        n = st // steps_per_kv
        q = q_ref[pl.ds(st * sub, sub)].reshape(sub * tq, HEAD_DIM)
        lg = lax.dot_general(q, k_ref[n, keys, :], _NT, preferred_element_type=F32)
        ps, ls = [], []
        for g in range(sub):
            x = lg[g * tq:(g + 1) * tq] + bias_ref[:, keys]
            shift = bound if shift_by_bound else jnp.max(x, axis=-1, keepdims=True)
            p = jnp.exp2(x - shift)
            ls.append(jnp.sum(p, axis=-1, keepdims=True))
            ps.append(p.astype(BF16))
        o = jnp.dot(jnp.concatenate(ps, axis=0), v_ref[n, keys, :], preferred_element_type=F32)
        for g in range(sub):
            o_ref[st * sub + g] = (o[g * tq:(g + 1) * tq] / ls[g]).astype(o_ref.dtype)
        return carry

    n_steps = q_ref.shape[0] // sub

    @pl.when(bound_ok)
    def _():
        lax.fori_loop(0, n_steps, functools.partial(att_body, True), 0)

    @pl.when(jnp.logical_not(bound_ok))
    def _():
        lax.fori_loop(0, n_steps, functools.partial(att_body, False), 0)


def _dsa_kernel(*refs, **static):
    S, tq = refs[8].shape
    i = pl.program_id(1)
    for c in range(S // tq):
        @pl.when(i == c)
        def _(c=c):
            _dsa_tile(*refs, nk=(c + 1) * tq, q0=c * tq, **static)


def dsa_attention_core(heads, wi, logit_bound, n_heads, n_kv, B, S, top_k):
    T = heads.shape[1]
    tq = _pick(S, (256, 128))
    nq = S // tq
    group = n_heads // n_kv
    sub = 2 if group % 2 == 0 else 1
    off_q, off_k = IDX_HEADS, IDX_HEADS + n_heads
    off_v, off_ki = off_k + n_kv, off_k + 2 * n_kv
    assert off_q % n_heads == 0 and off_k % n_kv == 0 and off_v % n_kv == 0
    once = dict(pipeline_mode=pl.Buffered(1))
    return pl.pallas_call(
        functools.partial(_dsa_kernel, top_k=top_k, sub=sub, steps_per_kv=group // sub),
        grid=(B, nq),
        in_specs=[pl.BlockSpec(memory_space=pltpu.SMEM),
                  pl.BlockSpec((IDX_HEADS, tq, HEAD_DIM), lambda b, i: (0, b * nq + i, 0)),
                  pl.BlockSpec((None, S, HEAD_DIM), lambda b, i: (off_ki, b, 0), **once),
                  pl.BlockSpec((IDX_HEADS, tq), lambda b, i: (0, b * nq + i)),
                  pl.BlockSpec((n_heads, tq, HEAD_DIM), lambda b, i: (off_q // n_heads, b * nq + i, 0)),
                  pl.BlockSpec((n_kv, S, HEAD_DIM), lambda b, i: (off_k // n_kv, b, 0), **once),
                  pl.BlockSpec((n_kv, S, HEAD_DIM), lambda b, i: (off_v // n_kv, b, 0), **once)],
        out_specs=pl.BlockSpec((n_heads, tq, HEAD_DIM), lambda b, i: (0, b * nq + i, 0)),
        out_shape=jax.ShapeDtypeStruct((n_heads, T, HEAD_DIM), BF16),
        scratch_shapes=[pltpu.VMEM((S, tq), F32), pltpu.VMEM((S, tq), I32), pltpu.VMEM((tq, S), F32)],
        compiler_params=_cparams(("parallel", "arbitrary")),
        name="dsa_core",
    )(logit_bound, heads, heads, wi, heads, heads, heads)


def _route_kernel(x_ref, g_ref, w_ref, b_ref, o_ref):
    x = x_ref[...]
    ms = jnp.mean(x * x, axis=-1, keepdims=True)
    t = x * lax.rsqrt(ms + RMS_EPS) * g_ref[...]
    logits = jnp.dot(t, w_ref[...], precision=lax.Precision.HIGHEST, preferred_element_type=F32) + b_ref[...]
    lane = lax.broadcasted_iota(I32, logits.shape, 1)
    lane_f = lane.astype(F32)
    big = float(LANES)
    gl = jnp.where(lane < N_GROUPS, logits, -jnp.inf)
    gmax = jnp.max(gl, axis=-1, keepdims=True)
    g_sel = jnp.min(jnp.where(gl == gmax, lane_f, big), axis=-1, keepdims=True)
    g_w = 1.0 / jnp.sum(jnp.exp(gl - gmax), axis=-1, keepdims=True)
    lo = N_GROUPS + g_sel * EXPERTS_PER_GROUP
    in_grp = jnp.logical_and(lane_f >= lo, lane_f < lo + EXPERTS_PER_GROUP)
    el = jnp.where(in_grp, logits, -jnp.inf)
    m1 = jnp.max(el, axis=-1, keepdims=True)
    i1 = jnp.min(jnp.where(el == m1, lane_f, big), axis=-1, keepdims=True)
    el2 = jnp.where(lane_f == i1, -jnp.inf, el)
    m2 = jnp.max(el2, axis=-1, keepdims=True)
    i2 = jnp.min(jnp.where(el2 == m2, lane_f, big), axis=-1, keepdims=True)
    z = jnp.sum(jnp.exp(el - m1), axis=-1, keepdims=True)
    p1 = 1.0 / z
    p2 = jnp.exp(m2 - m1) / z
    w1 = p1 / (p1 + p2) * g_w
    w2 = p2 / (p1 + p2) * g_w
    out = jnp.where(lane == 0, i1 - N_GROUPS, jnp.where(lane == 1, i2 - N_GROUPS,
          jnp.where(lane == 2, w1, jnp.where(lane == 3, w2, 0.0))))
    o_ref[...] = out


def moe_route(x2d, gain, w_group, b_group, w_expert, b_expert):
    T, D = x2d.shape
    n_log = w_group.shape[1] + w_expert.shape[1]
    w = jnp.concatenate([w_group, w_expert, jnp.zeros((D, LANES - n_log), F32)], axis=1)
    b = jnp.concatenate([b_group, b_expert, jnp.zeros((LANES - n_log,), F32)]).reshape(1, LANES)
    tm = _pick(T, (256, 128, 8))
    return pl.pallas_call(
        _route_kernel,
        grid=(T // tm,),
        in_specs=[pl.BlockSpec((tm, D), lambda i: (i, 0)), pl.BlockSpec((1, D), lambda i: (0, 0)),
                  pl.BlockSpec((D, LANES), lambda i: (0, 0)), pl.BlockSpec((1, LANES), lambda i: (0, 0))],
        out_specs=pl.BlockSpec((tm, LANES), lambda i: (i, 0)),
        out_shape=jax.ShapeDtypeStruct((T, LANES), F32),
        compiler_params=_cparams(("parallel",)),
        name="moe_route",
    )(x2d, gain.reshape(1, D), w, b)


def _row_copy(src_hbm, dst_ref, sem, src_row, dst_row):
    return pltpu.make_async_copy(src_hbm.at[pl.ds(src_row, 1), :], dst_ref.at[pl.ds(dst_row, 1), :], sem)


DMA_PRIORITIES = 2


def _start_row_gather(idx_ref, src_hbm, dst_ref, sem, base, n_rows, priorities=(0, 1)):
    def start(g, c):
        for d, pr in enumerate(priorities):
            r = g * len(priorities) + d
            _row_copy(src_hbm, dst_ref, sem, idx_ref[base + r], r).start(priority=pr)
        return c

    lax.fori_loop(0, n_rows // len(priorities), start, 0)


def _wait_row_gather(src_hbm, dst_ref, sem, n_rows):
    pltpu.make_async_copy(src_hbm.at[pl.ds(0, n_rows), :], dst_ref, sem).wait()


FFN_SLOTS = 3
FFN_GATHER_PRIORITIES = (1, 1)


def _ffn_kernel(src_ref, te_ref, nu_ref, x_hbm, g_ref, wg_ref, wu_ref, wd_ref, o_ref, xbuf, sem, *, priorities):
    j = pl.program_id(0)
    tm = o_ref.shape[0]
    n_used = nu_ref[0]
    ahead = FFN_SLOTS - 1

    def fetch(tile, slot):
        _start_row_gather(src_ref, x_hbm, xbuf.at[slot], sem.at[slot], tile * tm, tm, priorities)

    for t in range(ahead):
        @pl.when(jnp.logical_and(j == 0, t < n_used))
        def _(t=t):
            fetch(t, t)

    @pl.when(j + ahead < n_used)
    def _():
        fetch(j + ahead, lax.rem(j + ahead, FFN_SLOTS))

    @pl.when(j < n_used)
    def _():
        slot = lax.rem(j, FFN_SLOTS)
        _wait_row_gather(x_hbm, xbuf.at[slot], sem.at[slot], tm)
        x = xbuf[slot]
        ms = jnp.mean(x * x, axis=-1, keepdims=True)
        t = (x * lax.rsqrt(ms + RMS_EPS) * g_ref[...]).astype(BF16)
        a = jnp.dot(t, wg_ref[0].astype(BF16), preferred_element_type=F32)
        u = jnp.dot(t, wu_ref[0].astype(BF16), preferred_element_type=F32)
        hid = a * jax.nn.sigmoid(a) * u
        o_ref[...] = jnp.dot(hid.astype(BF16), wd_ref[0].astype(BF16), preferred_element_type=F32)

    @pl.when(j >= n_used)
    def _():
        o_ref[...] = jnp.zeros_like(o_ref)


def moe_ffn(x2d, src_tok, gain, w_gate, w_up, w_down, tile_expert, n_used, tm, priorities):
    D = x2d.shape[1]
    P = src_tok.shape[0]
    F = w_gate.shape[-1]
    assert P % tm == 0 and tm % DMA_PRIORITIES == 0
    wmap = lambda j, src, te, nu: (te[j], 0, 0)
    return pl.pallas_call(
        functools.partial(_ffn_kernel, priorities=priorities),
        grid_spec=pltpu.PrefetchScalarGridSpec(
            num_scalar_prefetch=3, grid=(P // tm,),
            in_specs=[pl.BlockSpec(memory_space=pl.ANY),
                      pl.BlockSpec((1, D), lambda j, src, te, nu: (0, 0)),
                      pl.BlockSpec((1, D, F), wmap), pl.BlockSpec((1, D, F), wmap),
                      pl.BlockSpec((1, F, D), wmap)],
            out_specs=pl.BlockSpec((tm, D), lambda j, src, te, nu: (j, 0)),
            scratch_shapes=[pltpu.VMEM((FFN_SLOTS, tm, D), F32), pltpu.SemaphoreType.DMA((FFN_SLOTS,))]),
        out_shape=jax.ShapeDtypeStruct((P, D), F32),
        compiler_params=_cparams(("arbitrary",)),
        name="moe_ffn",
    )(src_tok, tile_expert, n_used, x2d, gain.reshape(1, D), w_gate, w_up, w_down)


ROUTE_W_LANE = 2


def _combine_kernel(p1_ref, p2_ref, ys_hbm, x_ref, route_ref, gain_ref, o_ref, on_ref, buf, sem):
    i = pl.program_id(0)
    R = x_ref.shape[0]

    def fetch(tile, slot):
        _start_row_gather(p1_ref, ys_hbm, buf.at[2 * slot], sem.at[2 * slot], tile * R, R)
        _start_row_gather(p2_ref, ys_hbm, buf.at[2 * slot + 1], sem.at[2 * slot + 1], tile * R, R)

    @pl.when(i == 0)
    def _():
        fetch(0, 0)

    @pl.when(i + 1 < pl.num_programs(0))
    def _():
        fetch(i + 1, lax.rem(i + 1, 2))

    slot = lax.rem(i, 2)
    _wait_row_gather(ys_hbm, buf.at[2 * slot], sem.at[2 * slot], R)
    _wait_row_gather(ys_hbm, buf.at[2 * slot + 1], sem.at[2 * slot + 1], R)
    w1 = route_ref[:, ROUTE_W_LANE:ROUTE_W_LANE + 1]
    w2 = route_ref[:, ROUTE_W_LANE + 1:ROUTE_W_LANE + 2]
    y = x_ref[...] + (buf[2 * slot] * w1 + buf[2 * slot + 1] * w2)
    o_ref[...] = y
    ms = jnp.mean(y * y, axis=-1, keepdims=True)
    on_ref[...] = (y * lax.rsqrt(ms + RMS_EPS) * gain_ref[...]).astype(on_ref.dtype)


def moe_combine(x2d, ys, pos1, pos2, route, next_gain, rows_per_step):
    T, D = x2d.shape
    R = rows_per_step
    tile = lambda i, a, b: (i, 0)
    return pl.pallas_call(
        _combine_kernel,
        grid_spec=pltpu.PrefetchScalarGridSpec(
            num_scalar_prefetch=2, grid=(T // R,),
            in_specs=[pl.BlockSpec(memory_space=pl.ANY), pl.BlockSpec((R, D), tile),
                      pl.BlockSpec((R, LANES), tile), pl.BlockSpec((1, D), lambda i, a, b: (0, 0))],
            out_specs=[pl.BlockSpec((R, D), tile), pl.BlockSpec((R, D), tile)],
            scratch_shapes=[pltpu.VMEM((4, R, D), F32), pltpu.SemaphoreType.DMA((4,))]),
        out_shape=[jax.ShapeDtypeStruct((T, D), F32), jax.ShapeDtypeStruct((T, D), BF16)],
        compiler_params=_cparams(("arbitrary",)),
        name="moe_combine",
    )(pos1, pos2, ys, x2d, route, next_gain.reshape(1, D))


def hier_moe_residual(x2d, gain, w_group, b_group, w_expert, b_expert, w_gate, w_up, w_down, layer, next_gain):
    T, D = x2d.shape
    depth, G, E = w_gate.shape[0], w_gate.shape[1], w_gate.shape[2]
    n_exp = G * E
    F = w_gate.shape[-1]
    route = moe_route(x2d, gain, w_group, b_group, w_expert, b_expert)
    eid = route[:, 0:2].astype(I32).reshape(2 * T)
    tm = 256 if 2 * T >= 4096 else 128
    nt = (2 * T) // tm + n_exp
    onehot = (eid[None, :] == jnp.arange(n_exp, dtype=I32)[:, None]).astype(I32)
    csum = jnp.cumsum(onehot, axis=1)
    counts = csum[:, -1]
    rank = jnp.sum(onehot * csum, axis=0) - 1
    tiles = (counts + tm - 1) // tm
    tile_end = jnp.cumsum(tiles)
    tile_start = tile_end - tiles
    n_used = tile_end[-1:]
    j = jnp.arange(nt, dtype=I32)
    tile_expert = jnp.minimum(jnp.sum(tile_end[None, :] <= j[:, None], axis=1, dtype=I32), n_exp - 1)
    last_expert = jnp.sum(tile_end < n_used[0], dtype=I32)
    tile_expert = jnp.where(j < n_used[0], tile_expert, last_expert)
    pos = (tile_start[eid] * tm + rank).astype(I32)
    src_tok = jnp.zeros((nt * tm,), I32).at[pos].set(jnp.arange(2 * T, dtype=I32) // 2)
    pos = pos.reshape(T, 2)

    ys = moe_ffn(x2d, src_tok, gain, w_gate.reshape(depth * n_exp, D, F),
                 w_up.reshape(depth * n_exp, D, F), w_down.reshape(depth * n_exp, F, D),
                 tile_expert + layer * n_exp, n_used, tm, FFN_GATHER_PRIORITIES)
    return moe_combine(x2d, ys, pos[:, 0], pos[:, 1], route, next_gain, min(256, T))


def _ple_kernel(a_ref, w_ref, p_ref, pw_ref, x_ref, o_ref, *, row_chunk):
    w = w_ref[...].astype(BF16)
    pw = pw_ref[...].astype(BF16)
    for rc in range(o_ref.shape[0] // row_chunk):
        rows = pl.ds(rc * row_chunk, row_chunk)
        z = jnp.dot(a_ref[rows, :], w, preferred_element_type=F32)
        e = jnp.dot(p_ref[rows, :].astype(BF16), pw, preferred_element_type=F32)
        o_ref[rows, :] = x_ref[rows, :] + jax.nn.sigmoid(z) * e


def ple_residual(x2d, a, w_gate, p3d, w_proj, layer):
    T, D = x2d.shape
    P = p3d.shape[2]
    tm, tn = min(2048, T), 256
    return pl.pallas_call(
        functools.partial(_ple_kernel, row_chunk=min(ROW_CHUNK, tm)),
        grid=(T // tm, D // tn),
        in_specs=[pl.BlockSpec((tm, D), lambda i, j: (i, 0), pipeline_mode=pl.Buffered(1)),
                  pl.BlockSpec((None, D, tn), lambda i, j: (layer, 0, j)),
                  pl.BlockSpec((None, tm, P), lambda i, j: (layer, i, 0)),
                  pl.BlockSpec((None, P, tn), lambda i, j: (layer, 0, j)),
                  pl.BlockSpec((tm, tn), lambda i, j: (i, j))],
        out_specs=pl.BlockSpec((tm, tn), lambda i, j: (i, j)),
        out_shape=jax.ShapeDtypeStruct((T, D), F32),
        compiler_params=_cparams(("parallel", "parallel")),
        name="ple",
    )(a, w_gate, p3d, w_proj, x2d)


def _ret_proj_kernel(a_ref, w_ref, cos_ref, sin_ref, qkv_ref, *, n_qk, row_chunk):
    j = pl.program_id(1)
    tm = a_ref.shape[0]
    tn = w_ref.shape[1]
    half = RET_QK_DIM // 2
    w = w_ref[...].astype(BF16)
    shape = (row_chunk, half)
    is_rot = _flag(j < n_qk, shape)
    kscale = jnp.where(_flag(jnp.logical_and(j >= n_qk // 2, j < n_qk), shape), RET_QK_DIM ** -0.5, 1.0)
    for rc in range(tm // row_chunk):
        rows = pl.ds(rc * row_chunk, row_chunk)
        acc = jnp.dot(a_ref[rows, :], w, preferred_element_type=F32)
        cos = jnp.where(is_rot, cos_ref[rows, :], 1.0) * kscale
        sin = jnp.where(is_rot, sin_ref[rows, :], 0.0) * kscale
        for hh in range(tn // RET_QK_DIM):
            x1 = acc[:, hh * RET_QK_DIM:hh * RET_QK_DIM + half]
            x2 = acc[:, hh * RET_QK_DIM + half:(hh + 1) * RET_QK_DIM]
            qkv_ref[rows, hh * RET_QK_DIM:hh * RET_QK_DIM + half] = (x1 * cos - x2 * sin).astype(BF16)
            qkv_ref[rows, hh * RET_QK_DIM + half:(hh + 1) * RET_QK_DIM] = (x1 * sin + x2 * cos).astype(BF16)


def ret_in_proj(hn, w_in, layer, cos, sin, n_heads):
    T, K = hn.shape
    dk = RET_QK_DIM
    qkv_cols = 4 * n_heads * dk
    tn = 512 if (n_heads * dk) % 512 == 0 else dk
    tm = min(2048, T)
    once = dict(pipeline_mode=pl.Buffered(1))
    row = lambda i, j: (i, 0)
    qkv = pl.pallas_call(
        functools.partial(_ret_proj_kernel, n_qk=2 * n_heads * dk // tn, row_chunk=min(512, tm)),
        grid=(T // tm, qkv_cols // tn),
        in_specs=[pl.BlockSpec((tm, K), row, **once),
                  pl.BlockSpec((None, K, tn), lambda i, j: (layer, 0, j)),
                  pl.BlockSpec((tm, dk // 2), row, **once), pl.BlockSpec((tm, dk // 2), row, **once)],
        out_specs=pl.BlockSpec((tm, tn), lambda i, j: (i, j)),
        out_shape=jax.ShapeDtypeStruct((T, qkv_cols), BF16),
        compiler_params=_cparams(("parallel", "parallel")),
        name="ret_qkv_proj",
    )(hn, w_in, cos, sin)
    g = matmul(hn, w_in, layer, col0=qkv_cols, tm=tm, tn=tn, name="ret_g_proj")
    return qkv, g


RET_HEADS_PER_STEP = 2


def _retention_kernel(q_ref, k_ref, v_ref, g_ref, intra_ref, qd_ref, kd_ref, cd_ref, gn_ref, o_ref, state_ref):
    S = q_ref.shape[0]
    C = intra_ref.shape[1]
    dk = RET_QK_DIM
    dv = 2 * dk
    state_ref[...] = jnp.zeros_like(state_ref)

    def chunk(c, carry):
        rows = pl.ds(pl.multiple_of(c * C, C), C)
        for hh in range(RET_HEADS_PER_STEP):
            qc = q_ref[rows, hh * dk:(hh + 1) * dk]
            kc = k_ref[rows, hh * dk:(hh + 1) * dk]
            vc = v_ref[rows, hh * dv:(hh + 1) * dv]
            att = lax.dot_general(qc, kc, _NT, preferred_element_type=F32) * intra_ref[hh]
            inner = jnp.dot(att.astype(BF16), vc, preferred_element_type=F32)
            state = state_ref[hh]
            cross = jnp.dot((qc.astype(F32) * qd_ref[hh]).astype(BF16), state.astype(BF16),
                            preferred_element_type=F32)
            kt = jnp.transpose(kc.astype(F32) * kd_ref[hh]).astype(BF16)
            state_ref[hh] = state * cd_ref[hh] + jnp.dot(kt, vc, preferred_element_type=F32)
            o = inner + cross
            o = o * lax.rsqrt(jnp.mean(o * o, axis=-1, keepdims=True) + RMS_EPS) * gn_ref[hh]
            g = g_ref[rows, hh * dv:(hh + 1) * dv]
            o_ref[rows, hh * dv:(hh + 1) * dv] = (g * jax.nn.sigmoid(g) * o).astype(o_ref.dtype)
        return carry

    lax.fori_loop(0, S // C, chunk, 0)


def retention_core(qkv, g, out_norm, B, S, n_heads):
    T = qkv.shape[0]
    dk = RET_QK_DIM
    dv = 2 * dk
    C = RET_CHUNK
    hs = RET_HEADS_PER_STEP
    assert n_heads % hs == 0
    log_gamma = jnp.log1p(-jnp.exp2(-5.0 - jnp.arange(n_heads, dtype=F32)))
    i = jnp.arange(C, dtype=F32)
    dist = i[:, None] - i[None, :]
    intra = jnp.where(dist[None] >= 0, jnp.exp(log_gamma[:, None, None] * jnp.maximum(dist, 0.0)[None]), 0.0)
    q_decay = jnp.broadcast_to(jnp.exp(log_gamma[:, None] * (i[None, :] + 1.0))[:, :, None], (n_heads, C, dk))
    k_decay = jnp.broadcast_to(jnp.exp(log_gamma[:, None] * (C - 1.0 - i)[None, :])[:, :, None], (n_heads, C, dk))
    chunk_decay = jnp.broadcast_to(jnp.exp(log_gamma * C)[:, None, None], (n_heads, 1, dv))
    nq = n_heads // hs
    per_head = lambda b, h: (h, 0, 0)
    return pl.pallas_call(
        _retention_kernel,
        grid=(B, nq),
        in_specs=[pl.BlockSpec((S, hs * dk), lambda b, h: (b, h)),
                  pl.BlockSpec((S, hs * dk), lambda b, h: (b, nq + h)),
                  pl.BlockSpec((S, hs * dv), lambda b, h: (b, nq + h)),
                  pl.BlockSpec((S, hs * dv), lambda b, h: (b, h)),
                  pl.BlockSpec((hs, C, C), per_head), pl.BlockSpec((hs, C, dk), per_head),
                  pl.BlockSpec((hs, C, dk), per_head), pl.BlockSpec((hs, 1, dv), per_head),
                  pl.BlockSpec((hs, 1, dv), per_head)],
        out_specs=pl.BlockSpec((S, hs * dv), lambda b, h: (b, h)),
        out_shape=jax.ShapeDtypeStruct((T, n_heads * dv), BF16),
        scratch_shapes=[pltpu.VMEM((hs, dk, dv), F32)],
        compiler_params=_cparams(("parallel", "arbitrary")),
        name="retention_core",
    )(qkv, qkv, qkv, g, intra, q_decay, k_decay, chunk_decay, out_norm.reshape(n_heads, 1, dv))


def _rot_tables(positions, dim, theta):
    inv = theta ** (-jnp.arange(0, dim, 2, dtype=F32) / dim)
    ang = positions.astype(F32).reshape(-1, 1) * inv[None, :]
    return jnp.cos(ang), jnp.sin(ang)


def dsa_layer(x2d, positions, gain, w_in, w_out, q_norm, k_norm, idx_k_norm, layer, B, S):
    T, D = x2d.shape
    n_heads = D // HEAD_DIM
    n_kv = n_heads // 4
    rot_dim = HEAD_DIM // 4
    cos, sin = _rot_tables(positions, rot_dim, ROPE_THETA)
    ones = jnp.ones((T, HEAD_DIM - rot_dim), F32)
    zeros_r = jnp.zeros((T, HEAD_DIM - rot_dim // 2), F32)
    tabs = (jnp.concatenate([cos, cos, ones], axis=1),
            jnp.concatenate([-sin, zeros_r], axis=1),
            jnp.concatenate([jnp.zeros_like(sin), sin, jnp.zeros((T, HEAD_DIM - rot_dim), F32)], axis=1))
    hn = rmsnorm_cast(x2d, gain)
    heads, wi = dsa_in_proj(hn, w_in, layer, tabs, q_norm, k_norm, idx_k_norm, n_heads, n_kv)
    top_k = min(IDX_TOPK_MAX, S // 4)
    q_cap = HEAD_DIM ** 0.5 * jnp.max(jnp.abs(q_norm)) * (HEAD_DIM ** -0.5 * LOG2_E) * NORM_SLACK
    k_cap = HEAD_DIM ** 0.5 * jnp.max(jnp.abs(k_norm)) * NORM_SLACK
    o = dsa_attention_core(heads, jnp.transpose(wi[:, :IDX_HEADS]), (q_cap * k_cap).reshape(1), n_heads, n_kv,
                           B, S, top_k)
    return matmul(o, w_out, layer, res=x2d, head_major=True, tm=2048, tn=512, name="dsa_out_proj")


def retention_layer(x2d, positions, gain, w_in, w_out, out_norm, layer, B, S):
    T, D = x2d.shape
    n_heads = D // RET_QK_DIM
    cos, sin = _rot_tables(positions, RET_QK_DIM, RET_THETA)
    hn = rmsnorm_cast(x2d, gain)
    qkv, g = ret_in_proj(hn, w_in, layer, cos, sin, n_heads)
    og = retention_core(qkv, g, out_norm, B, S, n_heads)
    return matmul(og, w_out.astype(BF16), layer, res=x2d, tm=2048, tn=256, name="ret_out_proj")


def kernel(x, p, positions, attn_norm, ffn_norm, ple_norm, a_w_in, a_w_out, a_q_norm, a_k_norm, a_idx_k_norm,
           b_w_in, b_w_out, b_out_norm, moe_w_group, moe_b_group, moe_w_expert, moe_b_expert, moe_w_gate,
           moe_w_up, moe_w_down, ple_proj, ple_gate):
    B, S, D = x.shape
    depth = p.shape[0]
    T = B * S
    x2d = x.reshape(T, D)
    p3d = p.reshape(depth, T, p.shape[-1])
    for i in range(depth):
        j = i // 2
        if i % 2 == 0:
            x2d = dsa_layer(x2d, positions, attn_norm[i], a_w_in, a_w_out, a_q_norm[j], a_k_norm[j],
                            a_idx_k_norm[j], j, B, S)
        else:
            x2d = retention_layer(x2d, positions, attn_norm[i], b_w_in, b_w_out, b_out_norm[j], j, B, S)
        x2d, xn = hier_moe_residual(x2d, ffn_norm[i], moe_w_group[i], moe_b_group[i], moe_w_expert[i],
                                    moe_b_expert[i], moe_w_gate, moe_w_up, moe_w_down, i, ple_norm[i])
        x2d = ple_residual(x2d, xn, ple_gate, p3d, ple_proj, i)
    return x2d.reshape(B, S, D)
```

```python
import functools

import jax
import jax.numpy as jnp
from jax import lax
from jax.experimental import pallas as pl
from jax.experimental.pallas import tpu as pltpu

F32 = jnp.float32
BF16 = jnp.bfloat16
I32 = jnp.int32

RMS_EPS = 1e-6
HEAD_DIM = 128
IDX_HEADS = 64
IDX_TOPK_MAX = 256
ROPE_THETA = 500000.0
RET_QK_DIM = 256
RET_THETA = 10000.0
RET_CHUNK = 128
N_GROUPS = 8
EXPERTS_PER_GROUP = 8
LANES = 128
VMEM_LIMIT_BYTES = 56 * 1024 * 1024
NEG_BIG = -1e30
INT_MIN = -2147483648


def _cparams(sem):
    return pltpu.CompilerParams(dimension_semantics=sem, vmem_limit_bytes=VMEM_LIMIT_BYTES)


def _pick(n, cands):
    for c in cands:
        if n % c == 0:
            return c
    raise ValueError(f"no tile for {n} in {cands}")


def _rmsnorm_kernel(x_ref, g_ref, o_ref):
    x = x_ref[...]
    ms = jnp.mean(x * x, axis=-1, keepdims=True)
    o_ref[...] = (x * lax.rsqrt(ms + RMS_EPS) * g_ref[...]).astype(o_ref.dtype)


def rmsnorm_cast(x2d, gain):
    T, D = x2d.shape
    tm = _pick(T, (512, 256, 128, 8))
    return pl.pallas_call(
        _rmsnorm_kernel,
        grid=(T // tm,),
        in_specs=[pl.BlockSpec((tm, D), lambda i: (i, 0)), pl.BlockSpec((1, D), lambda i: (0, 0))],
        out_specs=pl.BlockSpec((tm, D), lambda i: (i, 0)),
        out_shape=jax.ShapeDtypeStruct((T, D), BF16),
        compiler_params=_cparams(("parallel",)),
        name="rmsnorm_cast",
    )(x2d, gain.reshape(1, D))


ROW_CHUNK = 512


def _mm_kernel(*refs, head_major, has_res, row_chunk):
    a_ref, w_ref = refs[0], refs[1]
    r_ref = refs[2] if has_res else None
    o_ref = refs[2 + has_res]
    w = w_ref[...].astype(BF16)
    for rc in range(o_ref.shape[0] // row_chunk):
        rows = pl.ds(rc * row_chunk, row_chunk)
        if head_major:
            a = jnp.concatenate([a_ref[h, rows, :] for h in range(a_ref.shape[0])], axis=1)
        else:
            a = a_ref[rows, :]
        acc = jnp.dot(a, w, preferred_element_type=F32)
        if has_res:
            acc = acc + r_ref[rows, :]
        o_ref[rows, :] = acc.astype(o_ref.dtype)


def matmul(a, w, layer, *, col0=0, res=None, out_dtype=F32, tm=1024, tn=512, head_major=False, name="matmul"):
    if head_major:
        nh, M, hd = a.shape
        K = nh * hd
    else:
        M, K = a.shape
    n_out = w.shape[2] - col0
    tm = min(tm, M)
    assert col0 % tn == 0 and M % tm == 0 and n_out % tn == 0
    once = dict(pipeline_mode=pl.Buffered(1))
    if head_major:
        a_spec = pl.BlockSpec((nh, tm, hd), lambda i, j: (0, i, 0), **once)
    else:
        a_spec = pl.BlockSpec((tm, K), lambda i, j: (i, 0), **once)
    in_specs = [a_spec, pl.BlockSpec((None, K, tn), lambda i, j: (layer, 0, j + col0 // tn))]
    args = [a, w]
    if res is not None:
        in_specs.append(pl.BlockSpec((tm, tn), lambda i, j: (i, j)))
        args.append(res)
    return pl.pallas_call(
        functools.partial(_mm_kernel, head_major=head_major, has_res=res is not None,
                          row_chunk=min(ROW_CHUNK, tm)),
        grid=(M // tm, n_out // tn),
        in_specs=in_specs,
        out_specs=pl.BlockSpec((tm, tn), lambda i, j: (i, j)),
        out_shape=jax.ShapeDtypeStruct((M, n_out), out_dtype),
        compiler_params=_cparams(("parallel", "parallel")),
        name=name,
    )(*args)


def _rot_head(x, c, s1, s2):
    half = HEAD_DIM // 8
    return x * c + pltpu.roll(x, HEAD_DIM - half, 1) * s1 + pltpu.roll(x, half, 1) * s2


def _flag(pred, shape):
    return jnp.broadcast_to(pred.astype(I32), shape) == 1


def _dsa_proj_kernel(a_ref, w_ref, c_ref, s1_ref, s2_ref, gains_ref, heads_ref, wi_ref, *, n_valid, bounds,
                     wi_lane, wscale, row_chunk):
    j = pl.program_id(1)
    tm = a_ref.shape[0]
    tn = w_ref.shape[1]
    b_q, b_k, b_v, b_qi = bounds
    col = j * tn + lax.broadcasted_iota(I32, (1, tn), 1)
    w = jnp.where(col < n_valid, w_ref[...], 0.0).astype(BF16)
    is_q = j < b_q
    is_k = jnp.logical_and(j >= b_q, j < b_k)
    is_v = jnp.logical_and(j >= b_k, j < b_v)
    is_ki = j == b_qi
    lane = (1, HEAD_DIM)
    gain = jnp.where(_flag(is_q, lane), gains_ref[0:1, :],
                     jnp.where(_flag(is_k, lane), gains_ref[1:2, :],
                               jnp.where(_flag(is_ki, lane), gains_ref[2:3, :], 1.0)))
    use_norm = _flag(jnp.logical_or(jnp.logical_or(is_q, is_k), is_ki), (row_chunk, 1))
    no_rot = _flag(is_v, (row_chunk, HEAD_DIM))
    for rc in range(tm // row_chunk):
        rows = pl.ds(rc * row_chunk, row_chunk)
        acc = jnp.dot(a_ref[rows, :], w, preferred_element_type=F32)
        c = jnp.where(no_rot, 1.0, c_ref[rows, :])
        s1 = jnp.where(no_rot, 0.0, s1_ref[rows, :])
        s2 = jnp.where(no_rot, 0.0, s2_ref[rows, :])
        for hh in range(tn // HEAD_DIM):
            x = acc[:, hh * HEAD_DIM:(hh + 1) * HEAD_DIM]
            ms = jnp.mean(x * x, axis=-1, keepdims=True)
            x = x * jnp.where(use_norm, lax.rsqrt(ms + RMS_EPS), 1.0) * gain
            heads_ref[hh, rows, :] = _rot_head(x, c, s1, s2).astype(heads_ref.dtype)
        wi_ref[rows, :] = acc[:, wi_lane:wi_lane + HEAD_DIM] * wscale


def dsa_in_proj(hn, w_in, layer, tabs, q_gain, k_gain, ki_gain, n_heads, n_kv):
    T, K = hn.shape
    n_cols = w_in.shape[2]
    seg_ends = [n_heads * HEAD_DIM]
    for width in (n_kv * HEAD_DIM, n_kv * HEAD_DIM, IDX_HEADS * HEAD_DIM):
        seg_ends.append(seg_ends[-1] + width)
    tn = next(t for t in (512, 256, 128) if all(e % t == 0 for e in seg_ends))
    hb = tn // HEAD_DIM
    bounds = tuple(e // tn for e in seg_ends)
    b_v, b_qi = bounds[2], bounds[3]
    n_out = -(-n_cols // tn) * tn
    wi_lane = (seg_ends[-1] + HEAD_DIM) % tn
    assert n_cols == seg_ends[-1] + HEAD_DIM + IDX_HEADS and (n_out - tn) <= seg_ends[-1] + HEAD_DIM
    tm = min(2048, T)
    row_chunk = min(512, tm)
    c, s1, s2 = tabs
    gains = jnp.zeros((8, HEAD_DIM), F32).at[0].set(q_gain * (HEAD_DIM ** -0.5 * LOG2_E)).at[1].set(k_gain)
    gains = gains.at[2].set(ki_gain)
    once = dict(pipeline_mode=pl.Buffered(1))
    row = lambda i, j: (i, 0)

    def head_block(i, j):
        pos = jnp.where(j < b_v, j + (b_qi - b_v), jnp.where(j < b_qi, j - b_v, j))
        return (pos, i, 0)

    return pl.pallas_call(
        functools.partial(_dsa_proj_kernel, n_valid=n_cols, bounds=bounds, wi_lane=wi_lane,
                          wscale=IDX_HEADS ** -0.5 * HEAD_DIM ** -0.5, row_chunk=row_chunk),
        grid=(T // tm, n_out // tn),
        in_specs=[pl.BlockSpec((tm, K), row, **once),
                  pl.BlockSpec((None, K, tn), lambda i, j: (layer, 0, j)),
                  pl.BlockSpec((tm, HEAD_DIM), row, **once), pl.BlockSpec((tm, HEAD_DIM), row, **once),
                  pl.BlockSpec((tm, HEAD_DIM), row, **once), pl.BlockSpec((8, HEAD_DIM), lambda i, j: (0, 0))],
        out_specs=[pl.BlockSpec((hb, tm, HEAD_DIM), head_block), pl.BlockSpec((tm, HEAD_DIM), row)],
        out_shape=[jax.ShapeDtypeStruct((n_out // HEAD_DIM, T, HEAD_DIM), BF16),
                   jax.ShapeDtypeStruct((T, HEAD_DIM), F32)],
        compiler_params=_cparams(("parallel", "arbitrary")),
        name="dsa_in_proj",
    )(hn, w_in, c, s1, s2, gains)


_NT = (((1,), (1,)), ((), ()))


IDX_UNROLL = 4
LOG2_E = 1.4426950408889634
SOFTMAX_SHIFT_MAX = 40.0
NORM_SLACK = 1.0 + 2.0 ** -6


def _dsa_tile(bound_ref, qi_ref, ki_ref, wi_ref, q_ref, k_ref, v_ref, o_ref, sc_ref, u_ref, bias_ref,
              *, top_k, sub, steps_per_kv, nk, q0):
    tq = sc_ref.shape[1]
    keys = pl.ds(0, nk)

    def idx_term(h):
        s = lax.dot_general(ki_ref[keys, :], qi_ref[h], _NT, preferred_element_type=F32)
        return jnp.maximum(s, 0.0) * wi_ref[pl.ds(h, 1), :]

    def idx_terms(h0):
        acc = idx_term(h0)
        for d in range(1, IDX_UNROLL):
            acc = acc + idx_term(h0 + d)
        return acc

    sc_ref[keys, :] = idx_terms(0)

    def idx_body(g, carry):
        sc_ref[keys, :] += idx_terms(g * IDX_UNROLL)
        return carry

    lax.fori_loop(1, IDX_HEADS // IDX_UNROLL, idx_body, 0)

    key_pos = lax.broadcasted_iota(I32, (nk, tq), 0)
    q_pos = q0 + lax.broadcasted_iota(I32, (nk, tq), 1)
    causal = key_pos <= q_pos
    bits = pltpu.bitcast(sc_ref[keys, :], I32)
    u = bits ^ ((bits >> 31) & 0x7FFFFFFF)
    u_ref[keys, :] = jnp.where(causal, u, INT_MIN)

    def count_ge(cand):
        return jnp.sum(jnp.where(u_ref[keys, :] >= cand, 1.0, 0.0), axis=0, keepdims=True)

    zero = jnp.zeros((1, tq), I32)
    v0 = jnp.where(count_ge(zero) >= top_k, zero, jnp.full((1, tq), INT_MIN, I32))

    def bis_body(it, v):
        cand = v + lax.shift_left(jnp.int32(1), 30 - it)
        return jnp.where(count_ge(cand) >= top_k, cand, v)

    thr = lax.fori_loop(0, 31, bis_body, v0)
    keep = jnp.logical_and(u_ref[keys, :] >= thr, causal)
    bias_ref[:, keys] = jnp.transpose(jnp.where(keep, 0.0, NEG_BIG))

    bound = bound_ref[0]
    bound_ok = bound <= SOFTMAX_SHIFT_MAX

    def att_body(shift_by_bound, st, carry):
        n = st // steps_per_kv
        q = q_ref[pl.ds(st * sub, sub)].reshape(sub * tq, HEAD_DIM)
        lg = lax.dot_general(q, k_ref[n, keys, :], _NT, preferred_element_type=F32)
        ps, ls = [], []
        for g in range(sub):
            x = lg[g * tq:(g + 1) * tq] + bias_ref[:, keys]
            shift = bound if shift_by_bound else jnp.max(x, axis=-1, keepdims=True)
            p = jnp.exp2(x - shift)
            ls.append(jnp.sum(p, axis=-1, keepdims=True))
            ps.append(p.astype(BF16))
        o = jnp.dot(jnp.concatenate(ps, axis=0), v_ref[n, keys, :], preferred_element_type=F32)
        for g in range(sub):
            o_ref[st * sub + g] = (o[g * tq:(g + 1) * tq] / ls[g]).astype(o_ref.dtype)
        return carry

    n_steps = q_ref.shape[0] // sub

    @pl.when(bound_ok)
    def _():
        lax.fori_loop(0, n_steps, functools.partial(att_body, True), 0)

    @pl.when(jnp.logical_not(bound_ok))
    def _():
        lax.fori_loop(0, n_steps, functools.partial(att_body, False), 0)


def _dsa_kernel(*refs, **static):
    S, tq = refs[8].shape
    i = pl.program_id(1)
    for c in range(S // tq):
        @pl.when(i == c)
        def _(c=c):
            _dsa_tile(*refs, nk=(c + 1) * tq, q0=c * tq, **static)


def dsa_attention_core(heads, wi, logit_bound, n_heads, n_kv, B, S, top_k):
    T = heads.shape[1]
    tq = _pick(S, (256, 128))
    nq = S // tq
    group = n_heads // n_kv
    sub = 2 if group % 2 == 0 else 1
    off_q, off_k = IDX_HEADS, IDX_HEADS + n_heads
    off_v, off_ki = off_k + n_kv, off_k + 2 * n_kv
    assert off_q % n_heads == 0 and off_k % n_kv == 0 and off_v % n_kv == 0
    once = dict(pipeline_mode=pl.Buffered(1))
    return pl.pallas_call(
        functools.partial(_dsa_kernel, top_k=top_k, sub=sub, steps_per_kv=group // sub),
        grid=(B, nq),
        in_specs=[pl.BlockSpec(memory_space=pltpu.SMEM),
                  pl.BlockSpec((IDX_HEADS, tq, HEAD_DIM), lambda b, i: (0, b * nq + i, 0)),
                  pl.BlockSpec((None, S, HEAD_DIM), lambda b, i: (off_ki, b, 0), **once),
                  pl.BlockSpec((IDX_HEADS, tq), lambda b, i: (0, b * nq + i)),
                  pl.BlockSpec((n_heads, tq, HEAD_DIM), lambda b, i: (off_q // n_heads, b * nq + i, 0)),
                  pl.BlockSpec((n_kv, S, HEAD_DIM), lambda b, i: (off_k // n_kv, b, 0), **once),
                  pl.BlockSpec((n_kv, S, HEAD_DIM), lambda b, i: (off_v // n_kv, b, 0), **once)],
        out_specs=pl.BlockSpec((n_heads, tq, HEAD_DIM), lambda b, i: (0, b * nq + i, 0)),
        out_shape=jax.ShapeDtypeStruct((n_heads, T, HEAD_DIM), BF16),
        scratch_shapes=[pltpu.VMEM((S, tq), F32), pltpu.VMEM((S, tq), I32), pltpu.VMEM((tq, S), F32)],
        compiler_params=_cparams(("parallel", "arbitrary")),
        name="dsa_core",
    )(logit_bound, heads, heads, wi, heads, heads, heads)


def _route_kernel(x_ref, g_ref, w_ref, b_ref, o_ref):
    x = x_ref[...]
    ms = jnp.mean(x * x, axis=-1, keepdims=True)
    t = x * lax.rsqrt(ms + RMS_EPS) * g_ref[...]
    logits = jnp.dot(t, w_ref[...], precision=lax.Precision.HIGHEST, preferred_element_type=F32) + b_ref[...]
    lane = lax.broadcasted_iota(I32, logits.shape, 1)
    lane_f = lane.astype(F32)
    big = float(LANES)
    gl = jnp.where(lane < N_GROUPS, logits, -jnp.inf)
    gmax = jnp.max(gl, axis=-1, keepdims=True)
    g_sel = jnp.min(jnp.where(gl == gmax, lane_f, big), axis=-1, keepdims=True)
    g_w = 1.0 / jnp.sum(jnp.exp(gl - gmax), axis=-1, keepdims=True)
    lo = N_GROUPS + g_sel * EXPERTS_PER_GROUP
    in_grp = jnp.logical_and(lane_f >= lo, lane_f < lo + EXPERTS_PER_GROUP)
    el = jnp.where(in_grp, logits, -jnp.inf)
    m1 = jnp.max(el, axis=-1, keepdims=True)
    i1 = jnp.min(jnp.where(el == m1, lane_f, big), axis=-1, keepdims=True)
    el2 = jnp.where(lane_f == i1, -jnp.inf, el)
    m2 = jnp.max(el2, axis=-1, keepdims=True)
    i2 = jnp.min(jnp.where(el2 == m2, lane_f, big), axis=-1, keepdims=True)
    z = jnp.sum(jnp.exp(el - m1), axis=-1, keepdims=True)
    p1 = 1.0 / z
    p2 = jnp.exp(m2 - m1) / z
    w1 = p1 / (p1 + p2) * g_w
    w2 = p2 / (p1 + p2) * g_w
    out = jnp.where(lane == 0, i1 - N_GROUPS, jnp.where(lane == 1, i2 - N_GROUPS,
          jnp.where(lane == 2, w1, jnp.where(lane == 3, w2, 0.0))))
    o_ref[...] = out


def moe_route(x2d, gain, w_group, b_group, w_expert, b_expert):
    T, D = x2d.shape
    n_log = w_group.shape[1] + w_expert.shape[1]
    w = jnp.concatenate([w_group, w_expert, jnp.zeros((D, LANES - n_log), F32)], axis=1)
    b = jnp.concatenate([b_group, b_expert, jnp.zeros((LANES - n_log,), F32)]).reshape(1, LANES)
    tm = _pick(T, (256, 128, 8))
    return pl.pallas_call(
        _route_kernel,
        grid=(T // tm,),
        in_specs=[pl.BlockSpec((tm, D), lambda i: (i, 0)), pl.BlockSpec((1, D), lambda i: (0, 0)),
                  pl.BlockSpec((D, LANES), lambda i: (0, 0)), pl.BlockSpec((1, LANES), lambda i: (0, 0))],
        out_specs=pl.BlockSpec((tm, LANES), lambda i: (i, 0)),
        out_shape=jax.ShapeDtypeStruct((T, LANES), F32),
        compiler_params=_cparams(("parallel",)),
        name="moe_route",
    )(x2d, gain.reshape(1, D), w, b)


def _row_copy(src_hbm, dst_ref, sem, src_row, dst_row):
    return pltpu.make_async_copy(src_hbm.at[pl.ds(src_row, 1), :], dst_ref.at[pl.ds(dst_row, 1), :], sem)


DMA_PRIORITIES = 2


def _start_row_gather(idx_ref, src_hbm, dst_ref, sem, base, n_rows, priorities=(0, 1)):
    def start(g, c):
        for d, pr in enumerate(priorities):
            r = g * len(priorities) + d
            _row_copy(src_hbm, dst_ref, sem, idx_ref[base + r], r).start(priority=pr)
        return c

    lax.fori_loop(0, n_rows // len(priorities), start, 0)


def _wait_row_gather(src_hbm, dst_ref, sem, n_rows):
    pltpu.make_async_copy(src_hbm.at[pl.ds(0, n_rows), :], dst_ref, sem).wait()


FFN_SLOTS = 3
FFN_GATHER_PRIORITIES = (1, 1)


def _ffn_kernel(src_ref, te_ref, first_ref, wslot_ref, nxt_ref, nu_ref, x_hbm, g_ref, wg_hbm, wu_hbm, wd_hbm, o_ref,
                xbuf, sem, wg_buf, wu_buf, wd_buf, wsem, *, priorities):
    j = pl.program_id(0)
    tm = o_ref.shape[0]
    n_used = nu_ref[0]
    ahead = FFN_SLOTS - 1

    def fetch(tile, slot):
        _start_row_gather(src_ref, x_hbm, xbuf.at[slot], sem.at[slot], tile * tm, tm, priorities)

    def weight_copies(expert, slot):
        return [pltpu.make_async_copy(hbm.at[expert], buf.at[slot], wsem.at[slot])
                for hbm, buf in ((wg_hbm, wg_buf), (wu_hbm, wu_buf), (wd_hbm, wd_buf))]

    @pl.when(jnp.logical_and(j < n_used, first_ref[j] == 1))
    def _():
        slot = wslot_ref[j]

        @pl.when(j == 0)
        def _():
            for c in weight_copies(te_ref[0], slot):
                c.start()

        @pl.when(nxt_ref[j] >= 0)
        def _():
            for c in weight_copies(nxt_ref[j], 1 - slot):
                c.start()

        for c in weight_copies(te_ref[j], slot):
            c.wait()

    for t in range(ahead):
        @pl.when(jnp.logical_and(j == 0, t < n_used))
        def _(t=t):
            fetch(t, t)

    @pl.when(j + ahead < n_used)
    def _():
        fetch(j + ahead, lax.rem(j + ahead, FFN_SLOTS))

    @pl.when(j < n_used)
    def _():
        slot = lax.rem(j, FFN_SLOTS)
        _wait_row_gather(x_hbm, xbuf.at[slot], sem.at[slot], tm)
        x = xbuf[slot]
        ms = jnp.mean(x * x, axis=-1, keepdims=True)
        t = (x * lax.rsqrt(ms + RMS_EPS) * g_ref[...]).astype(BF16)
        ws = wslot_ref[j]
        a = jnp.dot(t, wg_buf[ws].astype(BF16), preferred_element_type=F32)
        u = jnp.dot(t, wu_buf[ws].astype(BF16), preferred_element_type=F32)
        hid = a * jax.nn.sigmoid(a) * u
        o_ref[...] = jnp.dot(hid.astype(BF16), wd_buf[ws].astype(BF16), preferred_element_type=F32)

    @pl.when(j >= n_used)
    def _():
        o_ref[...] = jnp.zeros_like(o_ref)


def moe_ffn(x2d, src_tok, gain, w_gate, w_up, w_down, tile_expert, first_tile, weight_slot, next_expert, n_used,
            tm, priorities):
    D = x2d.shape[1]
    P = src_tok.shape[0]
    F = w_gate.shape[-1]
    assert P % tm == 0 and tm % DMA_PRIORITIES == 0
    hbm = pl.BlockSpec(memory_space=pl.ANY)
    return pl.pallas_call(
        functools.partial(_ffn_kernel, priorities=priorities),
        grid_spec=pltpu.PrefetchScalarGridSpec(
            num_scalar_prefetch=6, grid=(P // tm,),
            in_specs=[hbm, pl.BlockSpec((1, D), lambda j, *_: (0, 0)), hbm, hbm, hbm],
            out_specs=pl.BlockSpec((tm, D), lambda j, *_: (j, 0)),
            scratch_shapes=[pltpu.VMEM((FFN_SLOTS, tm, D), F32), pltpu.SemaphoreType.DMA((FFN_SLOTS,)),
                            pltpu.VMEM((2, D, F), F32), pltpu.VMEM((2, D, F), F32), pltpu.VMEM((2, F, D), F32),
                            pltpu.SemaphoreType.DMA((2,))]),
        out_shape=jax.ShapeDtypeStruct((P, D), F32),
        compiler_params=_cparams(("arbitrary",)),
        name="moe_ffn",
    )(src_tok, tile_expert, first_tile, weight_slot, next_expert, n_used, x2d, gain.reshape(1, D),
      w_gate, w_up, w_down)


ROUTE_W_LANE = 2


def _combine_kernel(p1_ref, p2_ref, ys_hbm, x_ref, route_ref, gain_ref, o_ref, on_ref, buf, sem):
    i = pl.program_id(0)
    R = x_ref.shape[0]

    def fetch(tile, slot):
        _start_row_gather(p1_ref, ys_hbm, buf.at[2 * slot], sem.at[2 * slot], tile * R, R)
        _start_row_gather(p2_ref, ys_hbm, buf.at[2 * slot + 1], sem.at[2 * slot + 1], tile * R, R)

    @pl.when(i == 0)
    def _():
        fetch(0, 0)

    @pl.when(i + 1 < pl.num_programs(0))
    def _():
        fetch(i + 1, lax.rem(i + 1, 2))

    slot = lax.rem(i, 2)
    _wait_row_gather(ys_hbm, buf.at[2 * slot], sem.at[2 * slot], R)
    _wait_row_gather(ys_hbm, buf.at[2 * slot + 1], sem.at[2 * slot + 1], R)
    w1 = route_ref[:, ROUTE_W_LANE:ROUTE_W_LANE + 1]
    w2 = route_ref[:, ROUTE_W_LANE + 1:ROUTE_W_LANE + 2]
    y = x_ref[...] + (buf[2 * slot] * w1 + buf[2 * slot + 1] * w2)
    o_ref[...] = y
    ms = jnp.mean(y * y, axis=-1, keepdims=True)
    on_ref[...] = (y * lax.rsqrt(ms + RMS_EPS) * gain_ref[...]).astype(on_ref.dtype)


def moe_combine(x2d, ys, pos1, pos2, route, next_gain, rows_per_step):
    T, D = x2d.shape
    R = rows_per_step
    tile = lambda i, a, b: (i, 0)
    return pl.pallas_call(
        _combine_kernel,
        grid_spec=pltpu.PrefetchScalarGridSpec(
            num_scalar_prefetch=2, grid=(T // R,),
            in_specs=[pl.BlockSpec(memory_space=pl.ANY), pl.BlockSpec((R, D), tile),
                      pl.BlockSpec((R, LANES), tile), pl.BlockSpec((1, D), lambda i, a, b: (0, 0))],
            out_specs=[pl.BlockSpec((R, D), tile), pl.BlockSpec((R, D), tile)],
            scratch_shapes=[pltpu.VMEM((4, R, D), F32), pltpu.SemaphoreType.DMA((4,))]),
        out_shape=[jax.ShapeDtypeStruct((T, D), F32), jax.ShapeDtypeStruct((T, D), BF16)],
        compiler_params=_cparams(("arbitrary",)),
        name="moe_combine",
    )(pos1, pos2, ys, x2d, route, next_gain.reshape(1, D))


def hier_moe_residual(x2d, gain, w_group, b_group, w_expert, b_expert, w_gate, w_up, w_down, layer, next_gain):
    T, D = x2d.shape
    depth, G, E = w_gate.shape[0], w_gate.shape[1], w_gate.shape[2]
    n_exp = G * E
    F = w_gate.shape[-1]
    route = moe_route(x2d, gain, w_group, b_group, w_expert, b_expert)
    eid = route[:, 0:2].astype(I32).reshape(2 * T)
    tm = 256 if 2 * T >= 4096 else 128
    nt = (2 * T) // tm + n_exp
    onehot = (eid[None, :] == jnp.arange(n_exp, dtype=I32)[:, None]).astype(I32)
    csum = jnp.cumsum(onehot, axis=1)
    counts = csum[:, -1]
    rank = jnp.sum(onehot * csum, axis=0) - 1
    tiles = (counts + tm - 1) // tm
    tile_end = jnp.cumsum(tiles)
    tile_start = tile_end - tiles
    n_used = tile_end[-1:]
    j = jnp.arange(nt, dtype=I32)
    tile_expert = jnp.minimum(jnp.sum(tile_end[None, :] <= j[:, None], axis=1, dtype=I32), n_exp - 1)
    last_expert = jnp.sum(tile_end < n_used[0], dtype=I32)
    tile_expert = jnp.where(j < n_used[0], tile_expert, last_expert)
    pos = (tile_start[eid] * tm + rank).astype(I32)
    src_tok = jnp.zeros((nt * tm,), I32).at[pos].set(jnp.arange(2 * T, dtype=I32) // 2)
    pos = pos.reshape(T, 2)

    used = j < n_used[0]
    first_tile = jnp.logical_and(used, jnp.concatenate([jnp.ones((1,), bool), tile_expert[1:] != tile_expert[:-1]]))
    weight_slot = (jnp.cumsum(first_tile.astype(I32)) + 1) % 2
    e_idx = jnp.arange(n_exp, dtype=I32)
    later = jnp.logical_and(e_idx[None, :] > e_idx[:, None], (tiles > 0)[None, :])
    next_nonempty = jnp.min(jnp.where(later, e_idx[None, :], n_exp), axis=1)
    next_expert = next_nonempty[tile_expert]
    next_expert = jnp.where(next_expert < n_exp, next_expert + layer * n_exp, -1).astype(I32)

    ys = moe_ffn(x2d, src_tok, gain, w_gate.reshape(depth * n_exp, D, F),
                 w_up.reshape(depth * n_exp, D, F), w_down.reshape(depth * n_exp, F, D),
                 tile_expert + layer * n_exp, first_tile.astype(I32), weight_slot.astype(I32), next_expert,
                 n_used, tm, FFN_GATHER_PRIORITIES)
    return moe_combine(x2d, ys, pos[:, 0], pos[:, 1], route, next_gain, min(256, T))


def _ple_kernel(a_ref, w_ref, p_ref, pw_ref, x_ref, o_ref, *, row_chunk):
    w = w_ref[...].astype(BF16)
    pw = pw_ref[...].astype(BF16)
    for rc in range(o_ref.shape[0] // row_chunk):
        rows = pl.ds(rc * row_chunk, row_chunk)
        z = jnp.dot(a_ref[rows, :], w, preferred_element_type=F32)
        e = jnp.dot(p_ref[rows, :].astype(BF16), pw, preferred_element_type=F32)
        o_ref[rows, :] = x_ref[rows, :] + jax.nn.sigmoid(z) * e


def ple_residual(x2d, a, w_gate, p3d, w_proj, layer):
    T, D = x2d.shape
    P = p3d.shape[2]
    tm, tn = min(2048, T), 256
    return pl.pallas_call(
        functools.partial(_ple_kernel, row_chunk=min(ROW_CHUNK, tm)),
        grid=(T // tm, D // tn),
        in_specs=[pl.BlockSpec((tm, D), lambda i, j: (i, 0), pipeline_mode=pl.Buffered(1)),
                  pl.BlockSpec((None, D, tn), lambda i, j: (layer, 0, j)),
                  pl.BlockSpec((None, tm, P), lambda i, j: (layer, i, 0)),
                  pl.BlockSpec((None, P, tn), lambda i, j: (layer, 0, j)),
                  pl.BlockSpec((tm, tn), lambda i, j: (i, j))],
        out_specs=pl.BlockSpec((tm, tn), lambda i, j: (i, j)),
        out_shape=jax.ShapeDtypeStruct((T, D), F32),
        compiler_params=_cparams(("parallel", "parallel")),
        name="ple",
    )(a, w_gate, p3d, w_proj, x2d)


def _ret_proj_kernel(a_ref, w_ref, cos_ref, sin_ref, qkv_ref, *, n_qk, row_chunk):
    j = pl.program_id(1)
    tm = a_ref.shape[0]
    tn = w_ref.shape[1]
    half = RET_QK_DIM // 2
    w = w_ref[...].astype(BF16)
    shape = (row_chunk, half)
    is_rot = _flag(j < n_qk, shape)
    kscale = jnp.where(_flag(jnp.logical_and(j >= n_qk // 2, j < n_qk), shape), RET_QK_DIM ** -0.5, 1.0)
    for rc in range(tm // row_chunk):
        rows = pl.ds(rc * row_chunk, row_chunk)
        acc = jnp.dot(a_ref[rows, :], w, preferred_element_type=F32)
        cos = jnp.where(is_rot, cos_ref[rows, :], 1.0) * kscale
        sin = jnp.where(is_rot, sin_ref[rows, :], 0.0) * kscale
        for hh in range(tn // RET_QK_DIM):
            x1 = acc[:, hh * RET_QK_DIM:hh * RET_QK_DIM + half]
            x2 = acc[:, hh * RET_QK_DIM + half:(hh + 1) * RET_QK_DIM]
            qkv_ref[rows, hh * RET_QK_DIM:hh * RET_QK_DIM + half] = (x1 * cos - x2 * sin).astype(BF16)
            qkv_ref[rows, hh * RET_QK_DIM + half:(hh + 1) * RET_QK_DIM] = (x1 * sin + x2 * cos).astype(BF16)


def ret_in_proj(hn, w_in, layer, cos, sin, n_heads):
    T, K = hn.shape
    dk = RET_QK_DIM
    qkv_cols = 4 * n_heads * dk
    tn = 512 if (n_heads * dk) % 512 == 0 else dk
    tm = min(2048, T)
    once = dict(pipeline_mode=pl.Buffered(1))
    row = lambda i, j: (i, 0)
    qkv = pl.pallas_call(
        functools.partial(_ret_proj_kernel, n_qk=2 * n_heads * dk // tn, row_chunk=min(512, tm)),
        grid=(T // tm, qkv_cols // tn),
        in_specs=[pl.BlockSpec((tm, K), row, **once),
                  pl.BlockSpec((None, K, tn), lambda i, j: (layer, 0, j)),
                  pl.BlockSpec((tm, dk // 2), row, **once), pl.BlockSpec((tm, dk // 2), row, **once)],
        out_specs=pl.BlockSpec((tm, tn), lambda i, j: (i, j)),
        out_shape=jax.ShapeDtypeStruct((T, qkv_cols), BF16),
        compiler_params=_cparams(("parallel", "parallel")),
        name="ret_qkv_proj",
    )(hn, w_in, cos, sin)
    g = matmul(hn, w_in, layer, col0=qkv_cols, tm=tm, tn=tn, name="ret_g_proj")
    return qkv, g


RET_HEADS_PER_STEP = 2


def _retention_kernel(q_ref, k_ref, v_ref, g_ref, intra_ref, qd_ref, kd_ref, cd_ref, gn_ref, o_ref, state_ref):
    S = q_ref.shape[0]
    C = intra_ref.shape[1]
    dk = RET_QK_DIM
    dv = 2 * dk
    state_ref[...] = jnp.zeros_like(state_ref)

    def chunk(c, carry):
        rows = pl.ds(pl.multiple_of(c * C, C), C)
        for hh in range(RET_HEADS_PER_STEP):
            qc = q_ref[rows, hh * dk:(hh + 1) * dk]
            kc = k_ref[rows, hh * dk:(hh + 1) * dk]
            vc = v_ref[rows, hh * dv:(hh + 1) * dv]
            att = lax.dot_general(qc, kc, _NT, preferred_element_type=F32) * intra_ref[hh]
            inner = jnp.dot(att.astype(BF16), vc, preferred_element_type=F32)
            state = state_ref[hh]
            cross = jnp.dot((qc.astype(F32) * qd_ref[hh]).astype(BF16), state.astype(BF16),
                            preferred_element_type=F32)
            kt = jnp.transpose(kc.astype(F32) * kd_ref[hh]).astype(BF16)
            state_ref[hh] = state * cd_ref[hh] + jnp.dot(kt, vc, preferred_element_type=F32)
            o = inner + cross
            o = o * lax.rsqrt(jnp.mean(o * o, axis=-1, keepdims=True) + RMS_EPS) * gn_ref[hh]
            g = g_ref[rows, hh * dv:(hh + 1) * dv]
            o_ref[rows, hh * dv:(hh + 1) * dv] = (g * jax.nn.sigmoid(g) * o).astype(o_ref.dtype)
        return carry

    lax.fori_loop(0, S // C, chunk, 0)


def retention_core(qkv, g, out_norm, B, S, n_heads):
    T = qkv.shape[0]
    dk = RET_QK_DIM
    dv = 2 * dk
    C = RET_CHUNK
    hs = RET_HEADS_PER_STEP
    assert n_heads % hs == 0
    log_gamma = jnp.log1p(-jnp.exp2(-5.0 - jnp.arange(n_heads, dtype=F32)))
    i = jnp.arange(C, dtype=F32)
    dist = i[:, None] - i[None, :]
    intra = jnp.where(dist[None] >= 0, jnp.exp(log_gamma[:, None, None] * jnp.maximum(dist, 0.0)[None]), 0.0)
    q_decay = jnp.broadcast_to(jnp.exp(log_gamma[:, None] * (i[None, :] + 1.0))[:, :, None], (n_heads, C, dk))
    k_decay = jnp.broadcast_to(jnp.exp(log_gamma[:, None] * (C - 1.0 - i)[None, :])[:, :, None], (n_heads, C, dk))
    chunk_decay = jnp.broadcast_to(jnp.exp(log_gamma * C)[:, None, None], (n_heads, 1, dv))
    nq = n_heads // hs
    per_head = lambda b, h: (h, 0, 0)
    return pl.pallas_call(
        _retention_kernel,
        grid=(B, nq),
        in_specs=[pl.BlockSpec((S, hs * dk), lambda b, h: (b, h)),
                  pl.BlockSpec((S, hs * dk), lambda b, h: (b, nq + h)),
                  pl.BlockSpec((S, hs * dv), lambda b, h: (b, nq + h)),
                  pl.BlockSpec((S, hs * dv), lambda b, h: (b, h)),
                  pl.BlockSpec((hs, C, C), per_head), pl.BlockSpec((hs, C, dk), per_head),
                  pl.BlockSpec((hs, C, dk), per_head), pl.BlockSpec((hs, 1, dv), per_head),
                  pl.BlockSpec((hs, 1, dv), per_head)],
        out_specs=pl.BlockSpec((S, hs * dv), lambda b, h: (b, h)),
        out_shape=jax.ShapeDtypeStruct((T, n_heads * dv), BF16),
        scratch_shapes=[pltpu.VMEM((hs, dk, dv), F32)],
        compiler_params=_cparams(("parallel", "arbitrary")),
        name="retention_core",
    )(qkv, qkv, qkv, g, intra, q_decay, k_decay, chunk_decay, out_norm.reshape(n_heads, 1, dv))


def _rot_tables(positions, dim, theta):
    inv = theta ** (-jnp.arange(0, dim, 2, dtype=F32) / dim)
    ang = positions.astype(F32).reshape(-1, 1) * inv[None, :]
    return jnp.cos(ang), jnp.sin(ang)


def dsa_layer(x2d, positions, gain, w_in, w_out, q_norm, k_norm, idx_k_norm, layer, B, S):
    T, D = x2d.shape
    n_heads = D // HEAD_DIM
    n_kv = n_heads // 4
    rot_dim = HEAD_DIM // 4
    cos, sin = _rot_tables(positions, rot_dim, ROPE_THETA)
    ones = jnp.ones((T, HEAD_DIM - rot_dim), F32)
    zeros_r = jnp.zeros((T, HEAD_DIM - rot_dim // 2), F32)
    tabs = (jnp.concatenate([cos, cos, ones], axis=1),
            jnp.concatenate([-sin, zeros_r], axis=1),
            jnp.concatenate([jnp.zeros_like(sin), sin, jnp.zeros((T, HEAD_DIM - rot_dim), F32)], axis=1))
    hn = rmsnorm_cast(x2d, gain)
    heads, wi = dsa_in_proj(hn, w_in, layer, tabs, q_norm, k_norm, idx_k_norm, n_heads, n_kv)
    top_k = min(IDX_TOPK_MAX, S // 4)
    q_cap = HEAD_DIM ** 0.5 * jnp.max(jnp.abs(q_norm)) * (HEAD_DIM ** -0.5 * LOG2_E) * NORM_SLACK
    k_cap = HEAD_DIM ** 0.5 * jnp.max(jnp.abs(k_norm)) * NORM_SLACK
    o = dsa_attention_core(heads, jnp.transpose(wi[:, :IDX_HEADS]), (q_cap * k_cap).reshape(1), n_heads, n_kv,
                           B, S, top_k)
    return matmul(o, w_out, layer, res=x2d, head_major=True, tm=2048, tn=512, name="dsa_out_proj")


def retention_layer(x2d, positions, gain, w_in, w_out, out_norm, layer, B, S):
    T, D = x2d.shape
    n_heads = D // RET_QK_DIM
    cos, sin = _rot_tables(positions, RET_QK_DIM, RET_THETA)
    hn = rmsnorm_cast(x2d, gain)
    qkv, g = ret_in_proj(hn, w_in, layer, cos, sin, n_heads)
    og = retention_core(qkv, g, out_norm, B, S, n_heads)
    return matmul(og, w_out.astype(BF16), layer, res=x2d, tm=2048, tn=256, name="ret_out_proj")


def kernel(x, p, positions, attn_norm, ffn_norm, ple_norm, a_w_in, a_w_out, a_q_norm, a_k_norm, a_idx_k_norm,
           b_w_in, b_w_out, b_out_norm, moe_w_group, moe_b_group, moe_w_expert, moe_b_expert, moe_w_gate,
           moe_w_up, moe_w_down, ple_proj, ple_gate):
    B, S, D = x.shape
    depth = p.shape[0]
    T = B * S
    x2d = x.reshape(T, D)
    p3d = p.reshape(depth, T, p.shape[-1])
    for i in range(depth):
        j = i // 2
        if i % 2 == 0:
            x2d = dsa_layer(x2d, positions, attn_norm[i], a_w_in, a_w_out, a_q_norm[j], a_k_norm[j],
                            a_idx_k_norm[j], j, B, S)
        else:
            x2d = retention_layer(x2d, positions, attn_norm[i], b_w_in, b_w_out, b_out_norm[j], j, B, S)
        x2d, xn = hier_moe_residual(x2d, ffn_norm[i], moe_w_group[i], moe_b_group[i], moe_w_expert[i],
                                    moe_b_expert[i], moe_w_gate, moe_w_up, moe_w_down, i, ple_norm[i])
        x2d = ple_residual(x2d, xn, ple_gate, p3d, ple_proj, i)
    return x2d.reshape(B, S, D)
```

```python
import functools

import jax
import jax.numpy as jnp
from jax import lax
from jax.experimental import pallas as pl
from jax.experimental.pallas import tpu as pltpu

F32 = jnp.float32
BF16 = jnp.bfloat16
I32 = jnp.int32

RMS_EPS = 1e-6
HEAD_DIM = 128
IDX_HEADS = 64
IDX_TOPK_MAX = 256
ROPE_THETA = 500000.0
RET_QK_DIM = 256
RET_THETA = 10000.0
RET_CHUNK = 128
N_GROUPS = 8
EXPERTS_PER_GROUP = 8
LANES = 128
VMEM_LIMIT_BYTES = 56 * 1024 * 1024
NEG_BIG = -1e30
INT_MIN = -2147483648


def _cparams(sem):
    return pltpu.CompilerParams(dimension_semantics=sem, vmem_limit_bytes=VMEM_LIMIT_BYTES)


def _pick(n, cands):
    for c in cands:
        if n % c == 0:
            return c
    raise ValueError(f"no tile for {n} in {cands}")


def _rmsnorm_kernel(x_ref, g_ref, o_ref):
    x = x_ref[...]
    ms = jnp.mean(x * x, axis=-1, keepdims=True)
    o_ref[...] = (x * lax.rsqrt(ms + RMS_EPS) * g_ref[...]).astype(o_ref.dtype)


def rmsnorm_cast(x2d, gain):
    T, D = x2d.shape
    tm = _pick(T, (512, 256, 128, 8))
    return pl.pallas_call(
        _rmsnorm_kernel,
        grid=(T // tm,),
        in_specs=[pl.BlockSpec((tm, D), lambda i: (i, 0)), pl.BlockSpec((1, D), lambda i: (0, 0))],
        out_specs=pl.BlockSpec((tm, D), lambda i: (i, 0)),
        out_shape=jax.ShapeDtypeStruct((T, D), BF16),
        compiler_params=_cparams(("parallel",)),
        name="rmsnorm_cast",
    )(x2d, gain.reshape(1, D))


ROW_CHUNK = 512


def _mm_kernel(*refs, head_major, has_res, row_chunk):
    a_ref, w_ref = refs[0], refs[1]
    r_ref = refs[2] if has_res else None
    o_ref = refs[2 + has_res]
    w = w_ref[...].astype(BF16)
    for rc in range(o_ref.shape[0] // row_chunk):
        rows = pl.ds(rc * row_chunk, row_chunk)
        if head_major:
            a = jnp.concatenate([a_ref[h, rows, :] for h in range(a_ref.shape[0])], axis=1)
        else:
            a = a_ref[rows, :]
        acc = jnp.dot(a, w, preferred_element_type=F32)
        if has_res:
            acc = acc + r_ref[rows, :]
        o_ref[rows, :] = acc.astype(o_ref.dtype)


def matmul(a, w, layer, *, col0=0, res=None, out_dtype=F32, tm=1024, tn=512, head_major=False, name="matmul"):
    if head_major:
        nh, M, hd = a.shape
        K = nh * hd
    else:
        M, K = a.shape
    n_out = w.shape[2] - col0
    tm = min(tm, M)
    assert col0 % tn == 0 and M % tm == 0 and n_out % tn == 0
    once = dict(pipeline_mode=pl.Buffered(1))
    if head_major:
        a_spec = pl.BlockSpec((nh, tm, hd), lambda i, j: (0, i, 0), **once)
    else:
        a_spec = pl.BlockSpec((tm, K), lambda i, j: (i, 0), **once)
    in_specs = [a_spec, pl.BlockSpec((None, K, tn), lambda i, j: (layer, 0, j + col0 // tn))]
    args = [a, w]
    if res is not None:
        in_specs.append(pl.BlockSpec((tm, tn), lambda i, j: (i, j)))
        args.append(res)
    return pl.pallas_call(
        functools.partial(_mm_kernel, head_major=head_major, has_res=res is not None,
                          row_chunk=min(ROW_CHUNK, tm)),
        grid=(M // tm, n_out // tn),
        in_specs=in_specs,
        out_specs=pl.BlockSpec((tm, tn), lambda i, j: (i, j)),
        out_shape=jax.ShapeDtypeStruct((M, n_out), out_dtype),
        compiler_params=_cparams(("parallel", "parallel")),
        name=name,
    )(*args)


def _rot_head(x, c, s1, s2):
    half = HEAD_DIM // 8
    return x * c + pltpu.roll(x, HEAD_DIM - half, 1) * s1 + pltpu.roll(x, half, 1) * s2


def _flag(pred, shape):
    return jnp.broadcast_to(pred.astype(I32), shape) == 1


def _dsa_proj_kernel(a_ref, w_ref, c_ref, s1_ref, s2_ref, gains_ref, heads_ref, wi_ref, *, n_valid, bounds,
                     wi_lane, wscale, row_chunk):
    j = pl.program_id(1)
    tm = a_ref.shape[0]
    tn = w_ref.shape[1]
    b_q, b_k, b_v, b_qi = bounds
    col = j * tn + lax.broadcasted_iota(I32, (1, tn), 1)
    w = jnp.where(col < n_valid, w_ref[...], 0.0).astype(BF16)
    is_q = j < b_q
    is_k = jnp.logical_and(j >= b_q, j < b_k)
    is_v = jnp.logical_and(j >= b_k, j < b_v)
    is_ki = j == b_qi
    lane = (1, HEAD_DIM)
    gain = jnp.where(_flag(is_q, lane), gains_ref[0:1, :],
                     jnp.where(_flag(is_k, lane), gains_ref[1:2, :],
                               jnp.where(_flag(is_ki, lane), gains_ref[2:3, :], 1.0)))
    use_norm = _flag(jnp.logical_or(jnp.logical_or(is_q, is_k), is_ki), (row_chunk, 1))
    no_rot = _flag(is_v, (row_chunk, HEAD_DIM))
    for rc in range(tm // row_chunk):
        rows = pl.ds(rc * row_chunk, row_chunk)
        acc = jnp.dot(a_ref[rows, :], w, preferred_element_type=F32)
        c = jnp.where(no_rot, 1.0, c_ref[rows, :])
        s1 = jnp.where(no_rot, 0.0, s1_ref[rows, :])
        s2 = jnp.where(no_rot, 0.0, s2_ref[rows, :])
        for hh in range(tn // HEAD_DIM):
            x = acc[:, hh * HEAD_DIM:(hh + 1) * HEAD_DIM]
            ms = jnp.mean(x * x, axis=-1, keepdims=True)
            x = x * jnp.where(use_norm, lax.rsqrt(ms + RMS_EPS), 1.0) * gain
            heads_ref[hh, rows, :] = _rot_head(x, c, s1, s2).astype(heads_ref.dtype)
        wi_ref[rows, :] = acc[:, wi_lane:wi_lane + HEAD_DIM] * wscale


def dsa_in_proj(hn, w_in, layer, tabs, q_gain, k_gain, ki_gain, n_heads, n_kv):
    T, K = hn.shape
    n_cols = w_in.shape[2]
    seg_ends = [n_heads * HEAD_DIM]
    for width in (n_kv * HEAD_DIM, n_kv * HEAD_DIM, IDX_HEADS * HEAD_DIM):
        seg_ends.append(seg_ends[-1] + width)
    tn = next(t for t in (512, 256, 128) if all(e % t == 0 for e in seg_ends))
    hb = tn // HEAD_DIM
    bounds = tuple(e // tn for e in seg_ends)
    b_v, b_qi = bounds[2], bounds[3]
    n_out = -(-n_cols // tn) * tn
    wi_lane = (seg_ends[-1] + HEAD_DIM) % tn
    assert n_cols == seg_ends[-1] + HEAD_DIM + IDX_HEADS and (n_out - tn) <= seg_ends[-1] + HEAD_DIM
    tm = min(2048, T)
    row_chunk = min(512, tm)
    c, s1, s2 = tabs
    gains = jnp.zeros((8, HEAD_DIM), F32).at[0].set(q_gain * (HEAD_DIM ** -0.5 * LOG2_E)).at[1].set(k_gain)
    gains = gains.at[2].set(ki_gain)
    once = dict(pipeline_mode=pl.Buffered(1))
    row = lambda i, j: (i, 0)

    def head_block(i, j):
        pos = jnp.where(j < b_v, j + (b_qi - b_v), jnp.where(j < b_qi, j - b_v, j))
        return (pos, i, 0)

    return pl.pallas_call(
        functools.partial(_dsa_proj_kernel, n_valid=n_cols, bounds=bounds, wi_lane=wi_lane,
                          wscale=IDX_HEADS ** -0.5 * HEAD_DIM ** -0.5, row_chunk=row_chunk),
        grid=(T // tm, n_out // tn),
        in_specs=[pl.BlockSpec((tm, K), row, **once),
                  pl.BlockSpec((None, K, tn), lambda i, j: (layer, 0, j)),
                  pl.BlockSpec((tm, HEAD_DIM), row, **once), pl.BlockSpec((tm, HEAD_DIM), row, **once),
                  pl.BlockSpec((tm, HEAD_DIM), row, **once), pl.BlockSpec((8, HEAD_DIM), lambda i, j: (0, 0))],
        out_specs=[pl.BlockSpec((hb, tm, HEAD_DIM), head_block), pl.BlockSpec((tm, HEAD_DIM), row)],
        out_shape=[jax.ShapeDtypeStruct((n_out // HEAD_DIM, T, HEAD_DIM), BF16),
                   jax.ShapeDtypeStruct((T, HEAD_DIM), F32)],
        compiler_params=_cparams(("parallel", "arbitrary")),
        name="dsa_in_proj",
    )(hn, w_in, c, s1, s2, gains)


_NT = (((1,), (1,)), ((), ()))


IDX_UNROLL = 4
LOG2_E = 1.4426950408889634
SOFTMAX_SHIFT_MAX = 40.0
NORM_SLACK = 1.0 + 2.0 ** -6


def _dsa_tile(bound_ref, qi_ref, ki_ref, wi_ref, q_ref, k_ref, v_ref, o_ref, sc_ref, u_ref, bias_ref,
              *, top_k, sub, steps_per_kv, nk, q0):
    tq = sc_ref.shape[1]
    keys = pl.ds(0, nk)

    def idx_term(h):
        s = lax.dot_general(ki_ref[keys, :], qi_ref[h], _NT, preferred_element_type=F32)
        return jnp.maximum(s, 0.0) * wi_ref[pl.ds(h, 1), :]

    def idx_terms(h0):
        acc = idx_term(h0)
        for d in range(1, IDX_UNROLL):
            acc = acc + idx_term(h0 + d)
        return acc

    sc_ref[keys, :] = idx_terms(0)

    def idx_body(g, carry):
        sc_ref[keys, :] += idx_terms(g * IDX_UNROLL)
        return carry

    lax.fori_loop(1, IDX_HEADS // IDX_UNROLL, idx_body, 0)

    key_pos = lax.broadcasted_iota(I32, (nk, tq), 0)
    q_pos = q0 + lax.broadcasted_iota(I32, (nk, tq), 1)
    causal = key_pos <= q_pos
    bits = pltpu.bitcast(sc_ref[keys, :], I32)
    u = bits ^ ((bits >> 31) & 0x7FFFFFFF)
    u_ref[keys, :] = jnp.where(causal, u, INT_MIN)

    def count_ge(cand):
        return jnp.sum(jnp.where(u_ref[keys, :] >= cand, 1.0, 0.0), axis=0, keepdims=True)

    zero = jnp.zeros((1, tq), I32)
    v0 = jnp.where(count_ge(zero) >= top_k, zero, jnp.full((1, tq), INT_MIN, I32))

    def bis_body(it, v):
        cand = v + lax.shift_left(jnp.int32(1), 30 - it)
        return jnp.where(count_ge(cand) >= top_k, cand, v)

    thr = lax.fori_loop(0, 31, bis_body, v0)
    keep = jnp.logical_and(u_ref[keys, :] >= thr, causal)
    bias_ref[:, keys] = jnp.transpose(jnp.where(keep, 0.0, NEG_BIG))

    bound = bound_ref[0]
    bound_ok = bound <= SOFTMAX_SHIFT_MAX

    def att_body(shift_by_bound, st, carry):
        n = st // steps_per_kv
        q = q_ref[pl.ds(st * sub, sub)].reshape(sub * tq, HEAD_DIM)
        lg = lax.dot_general(q, k_ref[n, keys, :], _NT, preferred_element_type=F32)
        ps, ls = [], []
        for g in range(sub):
            x = lg[g * tq:(g + 1) * tq] + bias_ref[:, keys]
            shift = bound if shift_by_bound else jnp.max(x, axis=-1, keepdims=True)
            p = jnp.exp2(x - shift)
            ls.append(jnp.sum(p, axis=-1, keepdims=True))
            ps.append(p.astype(BF16))
        o = jnp.dot(jnp.concatenate(ps, axis=0), v_ref[n, keys, :], preferred_element_type=F32)
        for g in range(sub):
            o_ref[st * sub + g] = (o[g * tq:(g + 1) * tq] / ls[g]).astype(o_ref.dtype)
        return carry

    n_steps = q_ref.shape[0] // sub

    @pl.when(bound_ok)
    def _():
        lax.fori_loop(0, n_steps, functools.partial(att_body, True), 0)

    @pl.when(jnp.logical_not(bound_ok))
    def _():
        lax.fori_loop(0, n_steps, functools.partial(att_body, False), 0)


def _dsa_kernel(*refs, **static):
    S, tq = refs[8].shape
    i = pl.program_id(1)
    for c in range(S // tq):
        @pl.when(i == c)
        def _(c=c):
            _dsa_tile(*refs, nk=(c + 1) * tq, q0=c * tq, **static)


def dsa_attention_core(heads, wi, logit_bound, n_heads, n_kv, B, S, top_k):
    T = heads.shape[1]
    tq = _pick(S, (256, 128))
    nq = S // tq
    group = n_heads // n_kv
    sub = 2 if group % 2 == 0 else 1
    off_q, off_k = IDX_HEADS, IDX_HEADS + n_heads
    off_v, off_ki = off_k + n_kv, off_k + 2 * n_kv
    assert off_q % n_heads == 0 and off_k % n_kv == 0 and off_v % n_kv == 0
    once = dict(pipeline_mode=pl.Buffered(1))
    return pl.pallas_call(
        functools.partial(_dsa_kernel, top_k=top_k, sub=sub, steps_per_kv=group // sub),
        grid=(B, nq),
        in_specs=[pl.BlockSpec(memory_space=pltpu.SMEM),
                  pl.BlockSpec((IDX_HEADS, tq, HEAD_DIM), lambda b, i: (0, b * nq + i, 0)),
                  pl.BlockSpec((None, S, HEAD_DIM), lambda b, i: (off_ki, b, 0), **once),
                  pl.BlockSpec((IDX_HEADS, tq), lambda b, i: (0, b * nq + i)),
                  pl.BlockSpec((n_heads, tq, HEAD_DIM), lambda b, i: (off_q // n_heads, b * nq + i, 0)),
                  pl.BlockSpec((n_kv, S, HEAD_DIM), lambda b, i: (off_k // n_kv, b, 0), **once),
                  pl.BlockSpec((n_kv, S, HEAD_DIM), lambda b, i: (off_v // n_kv, b, 0), **once)],
        out_specs=pl.BlockSpec((n_heads, tq, HEAD_DIM), lambda b, i: (0, b * nq + i, 0)),
        out_shape=jax.ShapeDtypeStruct((n_heads, T, HEAD_DIM), BF16),
        scratch_shapes=[pltpu.VMEM((S, tq), F32), pltpu.VMEM((S, tq), I32), pltpu.VMEM((tq, S), F32)],
        compiler_params=_cparams(("parallel", "arbitrary")),
        name="dsa_core",
    )(logit_bound, heads, heads, wi, heads, heads, heads)


def _pack_bf16_pair(t):
    n = t.shape[1] // 2
    hi = pltpu.bitcast(t[:, :n].astype(BF16).astype(F32), jnp.uint32)
    lo = pltpu.bitcast(t[:, n:].astype(BF16).astype(F32), jnp.uint32)
    return hi | lax.shift_right_logical(lo, jnp.uint32(16))


def _route_kernel(x_ref, g_ref, w_ref, b_ref, o_ref, tp_ref):
    x = x_ref[...]
    ms = jnp.mean(x * x, axis=-1, keepdims=True)
    t = x * lax.rsqrt(ms + RMS_EPS) * g_ref[...]
    tp_ref[...] = _pack_bf16_pair(t)
    logits = jnp.dot(t, w_ref[...], precision=lax.Precision.HIGHEST, preferred_element_type=F32) + b_ref[...]
    lane = lax.broadcasted_iota(I32, logits.shape, 1)
    lane_f = lane.astype(F32)
    big = float(LANES)
    gl = jnp.where(lane < N_GROUPS, logits, -jnp.inf)
    gmax = jnp.max(gl, axis=-1, keepdims=True)
    g_sel = jnp.min(jnp.where(gl == gmax, lane_f, big), axis=-1, keepdims=True)
    g_w = 1.0 / jnp.sum(jnp.exp(gl - gmax), axis=-1, keepdims=True)
    lo = N_GROUPS + g_sel * EXPERTS_PER_GROUP
    in_grp = jnp.logical_and(lane_f >= lo, lane_f < lo + EXPERTS_PER_GROUP)
    el = jnp.where(in_grp, logits, -jnp.inf)
    m1 = jnp.max(el, axis=-1, keepdims=True)
    i1 = jnp.min(jnp.where(el == m1, lane_f, big), axis=-1, keepdims=True)
    el2 = jnp.where(lane_f == i1, -jnp.inf, el)
    m2 = jnp.max(el2, axis=-1, keepdims=True)
    i2 = jnp.min(jnp.where(el2 == m2, lane_f, big), axis=-1, keepdims=True)
    z = jnp.sum(jnp.exp(el - m1), axis=-1, keepdims=True)
    p1 = 1.0 / z
    p2 = jnp.exp(m2 - m1) / z
    w1 = p1 / (p1 + p2) * g_w
    w2 = p2 / (p1 + p2) * g_w
    out = jnp.where(lane == 0, i1 - N_GROUPS, jnp.where(lane == 1, i2 - N_GROUPS,
          jnp.where(lane == 2, w1, jnp.where(lane == 3, w2, 0.0))))
    o_ref[...] = out


def moe_route(x2d, gain, w_group, b_group, w_expert, b_expert):
    T, D = x2d.shape
    n_log = w_group.shape[1] + w_expert.shape[1]
    w = jnp.concatenate([w_group, w_expert, jnp.zeros((D, LANES - n_log), F32)], axis=1)
    b = jnp.concatenate([b_group, b_expert, jnp.zeros((LANES - n_log,), F32)]).reshape(1, LANES)
    tm = _pick(T, (256, 128, 8))
    return pl.pallas_call(
        _route_kernel,
        grid=(T // tm,),
        in_specs=[pl.BlockSpec((tm, D), lambda i: (i, 0)), pl.BlockSpec((1, D), lambda i: (0, 0)),
                  pl.BlockSpec((D, LANES), lambda i: (0, 0)), pl.BlockSpec((1, LANES), lambda i: (0, 0))],
        out_specs=[pl.BlockSpec((tm, LANES), lambda i: (i, 0)), pl.BlockSpec((tm, D // 2), lambda i: (i, 0))],
        out_shape=[jax.ShapeDtypeStruct((T, LANES), F32), jax.ShapeDtypeStruct((T, D // 2), jnp.uint32)],
        compiler_params=_cparams(("parallel",)),
        name="moe_route",
    )(x2d, gain.reshape(1, D), w, b)


def _row_copy(src_hbm, dst_ref, sem, src_row, dst_row):
    return pltpu.make_async_copy(src_hbm.at[pl.ds(src_row, 1), :], dst_ref.at[pl.ds(dst_row, 1), :], sem)


DMA_PRIORITIES = 2


def _start_row_gather(idx_ref, src_hbm, dst_ref, sem, base, n_rows, priorities=(0, 1)):
    def start(g, c):
        for d, pr in enumerate(priorities):
            r = g * len(priorities) + d
            _row_copy(src_hbm, dst_ref, sem, idx_ref[base + r], r).start(priority=pr)
        return c

    lax.fori_loop(0, n_rows // len(priorities), start, 0)


def _wait_row_gather(src_hbm, dst_ref, sem, n_rows):
    pltpu.make_async_copy(src_hbm.at[pl.ds(0, n_rows), :], dst_ref, sem).wait()


FFN_SLOTS = 3
FFN_GATHER_PRIORITIES = (1, 1)


def _unpack_bf16_pair(words):
    hi = pltpu.bitcast(words & jnp.uint32(0xFFFF0000), F32).astype(BF16)
    lo = pltpu.bitcast(lax.shift_left(words, jnp.uint32(16)), F32).astype(BF16)
    return jnp.concatenate([hi, lo], axis=1)


def _ffn_kernel(src_ref, te_ref, nu_ref, t_hbm, wg_ref, wu_ref, wd_ref, o_ref, xbuf, sem, *, priorities):
    j = pl.program_id(0)
    tm = o_ref.shape[0]
    n_used = nu_ref[0]
    ahead = FFN_SLOTS - 1

    def fetch(tile, slot):
        _start_row_gather(src_ref, t_hbm, xbuf.at[slot], sem.at[slot], tile * tm, tm, priorities)

    for t in range(ahead):
        @pl.when(jnp.logical_and(j == 0, t < n_used))
        def _(t=t):
            fetch(t, t)

    @pl.when(j + ahead < n_used)
    def _():
        fetch(j + ahead, lax.rem(j + ahead, FFN_SLOTS))

    @pl.when(j < n_used)
    def _():
        slot = lax.rem(j, FFN_SLOTS)
        _wait_row_gather(t_hbm, xbuf.at[slot], sem.at[slot], tm)
        t = _unpack_bf16_pair(xbuf[slot])
        a = jnp.dot(t, wg_ref[0].astype(BF16), preferred_element_type=F32)
        u = jnp.dot(t, wu_ref[0].astype(BF16), preferred_element_type=F32)
        hid = a * jax.nn.sigmoid(a) * u
        o_ref[...] = jnp.dot(hid.astype(BF16), wd_ref[0].astype(BF16), preferred_element_type=F32)

    @pl.when(j >= n_used)
    def _():
        o_ref[...] = jnp.zeros_like(o_ref)


def moe_ffn(t_packed, src_tok, w_gate, w_up, w_down, tile_expert, n_used, tm, priorities):
    D = 2 * t_packed.shape[1]
    P = src_tok.shape[0]
    F = w_gate.shape[-1]
    assert P % tm == 0 and tm % DMA_PRIORITIES == 0
    wmap = lambda j, src, te, nu: (te[j], 0, 0)
    return pl.pallas_call(
        functools.partial(_ffn_kernel, priorities=priorities),
        grid_spec=pltpu.PrefetchScalarGridSpec(
            num_scalar_prefetch=3, grid=(P // tm,),
            in_specs=[pl.BlockSpec(memory_space=pl.ANY),
                      pl.BlockSpec((1, D, F), wmap), pl.BlockSpec((1, D, F), wmap),
                      pl.BlockSpec((1, F, D), wmap)],
            out_specs=pl.BlockSpec((tm, D), lambda j, src, te, nu: (j, 0)),
            scratch_shapes=[pltpu.VMEM((FFN_SLOTS, tm, D // 2), jnp.uint32),
                            pltpu.SemaphoreType.DMA((FFN_SLOTS,))]),
        out_shape=jax.ShapeDtypeStruct((P, D), F32),
        compiler_params=_cparams(("arbitrary",)),
        name="moe_ffn",
    )(src_tok, tile_expert, n_used, t_packed, w_gate, w_up, w_down)


ROUTE_W_LANE = 2


def _combine_kernel(p1_ref, p2_ref, ys_hbm, x_ref, route_ref, gain_ref, o_ref, on_ref, buf, sem):
    i = pl.program_id(0)
    R = x_ref.shape[0]

    def fetch(tile, slot):
        _start_row_gather(p1_ref, ys_hbm, buf.at[2 * slot], sem.at[2 * slot], tile * R, R)
        _start_row_gather(p2_ref, ys_hbm, buf.at[2 * slot + 1], sem.at[2 * slot + 1], tile * R, R)

    @pl.when(i == 0)
    def _():
        fetch(0, 0)

    @pl.when(i + 1 < pl.num_programs(0))
    def _():
        fetch(i + 1, lax.rem(i + 1, 2))

    slot = lax.rem(i, 2)
    _wait_row_gather(ys_hbm, buf.at[2 * slot], sem.at[2 * slot], R)
    _wait_row_gather(ys_hbm, buf.at[2 * slot + 1], sem.at[2 * slot + 1], R)
    w1 = route_ref[:, ROUTE_W_LANE:ROUTE_W_LANE + 1]
    w2 = route_ref[:, ROUTE_W_LANE + 1:ROUTE_W_LANE + 2]
    y = x_ref[...] + (buf[2 * slot] * w1 + buf[2 * slot + 1] * w2)
    o_ref[...] = y
    ms = jnp.mean(y * y, axis=-1, keepdims=True)
    on_ref[...] = (y * lax.rsqrt(ms + RMS_EPS) * gain_ref[...]).astype(on_ref.dtype)


def moe_combine(x2d, ys, pos1, pos2, route, next_gain, rows_per_step):
    T, D = x2d.shape
    R = rows_per_step
    tile = lambda i, a, b: (i, 0)
    return pl.pallas_call(
        _combine_kernel,
        grid_spec=pltpu.PrefetchScalarGridSpec(
            num_scalar_prefetch=2, grid=(T // R,),
            in_specs=[pl.BlockSpec(memory_space=pl.ANY), pl.BlockSpec((R, D), tile),
                      pl.BlockSpec((R, LANES), tile), pl.BlockSpec((1, D), lambda i, a, b: (0, 0))],
            out_specs=[pl.BlockSpec((R, D), tile), pl.BlockSpec((R, D), tile)],
            scratch_shapes=[pltpu.VMEM((4, R, D), F32), pltpu.SemaphoreType.DMA((4,))]),
        out_shape=[jax.ShapeDtypeStruct((T, D), F32), jax.ShapeDtypeStruct((T, D), BF16)],
        compiler_params=_cparams(("arbitrary",)),
        name="moe_combine",
    )(pos1, pos2, ys, x2d, route, next_gain.reshape(1, D))


def hier_moe_residual(x2d, gain, w_group, b_group, w_expert, b_expert, w_gate, w_up, w_down, layer, next_gain):
    T, D = x2d.shape
    depth, G, E = w_gate.shape[0], w_gate.shape[1], w_gate.shape[2]
    n_exp = G * E
    F = w_gate.shape[-1]
    route, t_packed = moe_route(x2d, gain, w_group, b_group, w_expert, b_expert)
    eid = route[:, 0:2].astype(I32).reshape(2 * T)
    tm = 256 if 2 * T >= 4096 else 128
    nt = (2 * T) // tm + n_exp
    onehot = (eid[None, :] == jnp.arange(n_exp, dtype=I32)[:, None]).astype(I32)
    csum = jnp.cumsum(onehot, axis=1)
    counts = csum[:, -1]
    rank = jnp.sum(onehot * csum, axis=0) - 1
    tiles = (counts + tm - 1) // tm
    tile_end = jnp.cumsum(tiles)
    tile_start = tile_end - tiles
    n_used = tile_end[-1:]
    j = jnp.arange(nt, dtype=I32)
    tile_expert = jnp.minimum(jnp.sum(tile_end[None, :] <= j[:, None], axis=1, dtype=I32), n_exp - 1)
    last_expert = jnp.sum(tile_end < n_used[0], dtype=I32)
    tile_expert = jnp.where(j < n_used[0], tile_expert, last_expert)
    pos = (tile_start[eid] * tm + rank).astype(I32)
    src_tok = jnp.zeros((nt * tm,), I32).at[pos].set(jnp.arange(2 * T, dtype=I32) // 2)
    pos = pos.reshape(T, 2)

    ys = moe_ffn(t_packed, src_tok, w_gate.reshape(depth * n_exp, D, F), w_up.reshape(depth * n_exp, D, F),
                 w_down.reshape(depth * n_exp, F, D), tile_expert + layer * n_exp, n_used, tm,
                 FFN_GATHER_PRIORITIES)
    return moe_combine(x2d, ys, pos[:, 0], pos[:, 1], route, next_gain, min(256, T))


def _ple_kernel(a_ref, w_ref, p_ref, pw_ref, x_ref, o_ref, *, row_chunk):
    w = w_ref[...].astype(BF16)
    pw = pw_ref[...].astype(BF16)
    for rc in range(o_ref.shape[0] // row_chunk):
        rows = pl.ds(rc * row_chunk, row_chunk)
        z = jnp.dot(a_ref[rows, :], w, preferred_element_type=F32)
        e = jnp.dot(p_ref[rows, :].astype(BF16), pw, preferred_element_type=F32)
        o_ref[rows, :] = x_ref[rows, :] + jax.nn.sigmoid(z) * e


def ple_residual(x2d, a, w_gate, p3d, w_proj, layer):
    T, D = x2d.shape
    P = p3d.shape[2]
    tm, tn = min(2048, T), 256
    return pl.pallas_call(
        functools.partial(_ple_kernel, row_chunk=min(ROW_CHUNK, tm)),
        grid=(T // tm, D // tn),
        in_specs=[pl.BlockSpec((tm, D), lambda i, j: (i, 0), pipeline_mode=pl.Buffered(1)),
                  pl.BlockSpec((None, D, tn), lambda i, j: (layer, 0, j)),
                  pl.BlockSpec((None, tm, P), lambda i, j: (layer, i, 0)),
                  pl.BlockSpec((None, P, tn), lambda i, j: (layer, 0, j)),
                  pl.BlockSpec((tm, tn), lambda i, j: (i, j))],
        out_specs=pl.BlockSpec((tm, tn), lambda i, j: (i, j)),
        out_shape=jax.ShapeDtypeStruct((T, D), F32),
        compiler_params=_cparams(("parallel", "parallel")),
        name="ple",
    )(a, w_gate, p3d, w_proj, x2d)


def _ret_proj_kernel(a_ref, w_ref, cos_ref, sin_ref, qkv_ref, *, n_qk, row_chunk):
    j = pl.program_id(1)
    tm = a_ref.shape[0]
    tn = w_ref.shape[1]
    half = RET_QK_DIM // 2
    w = w_ref[...].astype(BF16)
    shape = (row_chunk, half)
    is_rot = _flag(j < n_qk, shape)
    kscale = jnp.where(_flag(jnp.logical_and(j >= n_qk // 2, j < n_qk), shape), RET_QK_DIM ** -0.5, 1.0)
    for rc in range(tm // row_chunk):
        rows = pl.ds(rc * row_chunk, row_chunk)
        acc = jnp.dot(a_ref[rows, :], w, preferred_element_type=F32)
        cos = jnp.where(is_rot, cos_ref[rows, :], 1.0) * kscale
        sin = jnp.where(is_rot, sin_ref[rows, :], 0.0) * kscale
        for hh in range(tn // RET_QK_DIM):
            x1 = acc[:, hh * RET_QK_DIM:hh * RET_QK_DIM + half]
            x2 = acc[:, hh * RET_QK_DIM + half:(hh + 1) * RET_QK_DIM]
            qkv_ref[rows, hh * RET_QK_DIM:hh * RET_QK_DIM + half] = (x1 * cos - x2 * sin).astype(BF16)
            qkv_ref[rows, hh * RET_QK_DIM + half:(hh + 1) * RET_QK_DIM] = (x1 * sin + x2 * cos).astype(BF16)


def ret_in_proj(hn, w_in, layer, cos, sin, n_heads):
    T, K = hn.shape
    dk = RET_QK_DIM
    qkv_cols = 4 * n_heads * dk
    tn = 512 if (n_heads * dk) % 512 == 0 else dk
    tm = min(2048, T)
    once = dict(pipeline_mode=pl.Buffered(1))
    row = lambda i, j: (i, 0)
    qkv = pl.pallas_call(
        functools.partial(_ret_proj_kernel, n_qk=2 * n_heads * dk // tn, row_chunk=min(512, tm)),
        grid=(T // tm, qkv_cols // tn),
        in_specs=[pl.BlockSpec((tm, K), row, **once),
                  pl.BlockSpec((None, K, tn), lambda i, j: (layer, 0, j)),
                  pl.BlockSpec((tm, dk // 2), row, **once), pl.BlockSpec((tm, dk // 2), row, **once)],
        out_specs=pl.BlockSpec((tm, tn), lambda i, j: (i, j)),
        out_shape=jax.ShapeDtypeStruct((T, qkv_cols), BF16),
        compiler_params=_cparams(("parallel", "parallel")),
        name="ret_qkv_proj",
    )(hn, w_in, cos, sin)
    g = matmul(hn, w_in, layer, col0=qkv_cols, tm=tm, tn=tn, name="ret_g_proj")
    return qkv, g


RET_HEADS_PER_STEP = 2


def _retention_kernel(q_ref, k_ref, v_ref, g_ref, intra_ref, qd_ref, kd_ref, cd_ref, gn_ref, o_ref, state_ref):
    S = q_ref.shape[0]
    C = intra_ref.shape[1]
    dk = RET_QK_DIM
    dv = 2 * dk
    state_ref[...] = jnp.zeros_like(state_ref)

    def chunk(c, carry):
        rows = pl.ds(pl.multiple_of(c * C, C), C)
        for hh in range(RET_HEADS_PER_STEP):
            qc = q_ref[rows, hh * dk:(hh + 1) * dk]
            kc = k_ref[rows, hh * dk:(hh + 1) * dk]
            vc = v_ref[rows, hh * dv:(hh + 1) * dv]
            att = lax.dot_general(qc, kc, _NT, preferred_element_type=F32) * intra_ref[hh]
            inner = jnp.dot(att.astype(BF16), vc, preferred_element_type=F32)
            state = state_ref[hh]
            cross = jnp.dot((qc.astype(F32) * qd_ref[hh]).astype(BF16), state.astype(BF16),
                            preferred_element_type=F32)
            kt = jnp.transpose(kc.astype(F32) * kd_ref[hh]).astype(BF16)
            state_ref[hh] = state * cd_ref[hh] + jnp.dot(kt, vc, preferred_element_type=F32)
            o = inner + cross
            o = o * lax.rsqrt(jnp.mean(o * o, axis=-1, keepdims=True) + RMS_EPS) * gn_ref[hh]
            g = g_ref[rows, hh * dv:(hh + 1) * dv]
            o_ref[rows, hh * dv:(hh + 1) * dv] = (g * jax.nn.sigmoid(g) * o).astype(o_ref.dtype)
        return carry

    lax.fori_loop(0, S // C, chunk, 0)


def retention_core(qkv, g, out_norm, B, S, n_heads):
    T = qkv.shape[0]
    dk = RET_QK_DIM
    dv = 2 * dk
    C = RET_CHUNK
    hs = RET_HEADS_PER_STEP
    assert n_heads % hs == 0
    log_gamma = jnp.log1p(-jnp.exp2(-5.0 - jnp.arange(n_heads, dtype=F32)))
    i = jnp.arange(C, dtype=F32)
    dist = i[:, None] - i[None, :]
    intra = jnp.where(dist[None] >= 0, jnp.exp(log_gamma[:, None, None] * jnp.maximum(dist, 0.0)[None]), 0.0)
    q_decay = jnp.broadcast_to(jnp.exp(log_gamma[:, None] * (i[None, :] + 1.0))[:, :, None], (n_heads, C, dk))
    k_decay = jnp.broadcast_to(jnp.exp(log_gamma[:, None] * (C - 1.0 - i)[None, :])[:, :, None], (n_heads, C, dk))
    chunk_decay = jnp.broadcast_to(jnp.exp(log_gamma * C)[:, None, None], (n_heads, 1, dv))
    nq = n_heads // hs
    per_head = lambda b, h: (h, 0, 0)
    return pl.pallas_call(
        _retention_kernel,
        grid=(B, nq),
        in_specs=[pl.BlockSpec((S, hs * dk), lambda b, h: (b, h)),
                  pl.BlockSpec((S, hs * dk), lambda b, h: (b, nq + h)),
                  pl.BlockSpec((S, hs * dv), lambda b, h: (b, nq + h)),
                  pl.BlockSpec((S, hs * dv), lambda b, h: (b, h)),
                  pl.BlockSpec((hs, C, C), per_head), pl.BlockSpec((hs, C, dk), per_head),
                  pl.BlockSpec((hs, C, dk), per_head), pl.BlockSpec((hs, 1, dv), per_head),
                  pl.BlockSpec((hs, 1, dv), per_head)],
        out_specs=pl.BlockSpec((S, hs * dv), lambda b, h: (b, h)),
        out_shape=jax.ShapeDtypeStruct((T, n_heads * dv), BF16),
        scratch_shapes=[pltpu.VMEM((hs, dk, dv), F32)],
        compiler_params=_cparams(("parallel", "arbitrary")),
        name="retention_core",
    )(qkv, qkv, qkv, g, intra, q_decay, k_decay, chunk_decay, out_norm.reshape(n_heads, 1, dv))


def _rot_tables(positions, dim, theta):
    inv = theta ** (-jnp.arange(0, dim, 2, dtype=F32) / dim)
    ang = positions.astype(F32).reshape(-1, 1) * inv[None, :]
    return jnp.cos(ang), jnp.sin(ang)


def dsa_layer(x2d, positions, gain, w_in, w_out, q_norm, k_norm, idx_k_norm, layer, B, S):
    T, D = x2d.shape
    n_heads = D // HEAD_DIM
    n_kv = n_heads // 4
    rot_dim = HEAD_DIM // 4
    cos, sin = _rot_tables(positions, rot_dim, ROPE_THETA)
    ones = jnp.ones((T, HEAD_DIM - rot_dim), F32)
    zeros_r = jnp.zeros((T, HEAD_DIM - rot_dim // 2), F32)
    tabs = (jnp.concatenate([cos, cos, ones], axis=1),
            jnp.concatenate([-sin, zeros_r], axis=1),
            jnp.concatenate([jnp.zeros_like(sin), sin, jnp.zeros((T, HEAD_DIM - rot_dim), F32)], axis=1))
    hn = rmsnorm_cast(x2d, gain)
    heads, wi = dsa_in_proj(hn, w_in, layer, tabs, q_norm, k_norm, idx_k_norm, n_heads, n_kv)
    top_k = min(IDX_TOPK_MAX, S // 4)
    q_cap = HEAD_DIM ** 0.5 * jnp.max(jnp.abs(q_norm)) * (HEAD_DIM ** -0.5 * LOG2_E) * NORM_SLACK
    k_cap = HEAD_DIM ** 0.5 * jnp.max(jnp.abs(k_norm)) * NORM_SLACK
    o = dsa_attention_core(heads, jnp.transpose(wi[:, :IDX_HEADS]), (q_cap * k_cap).reshape(1), n_heads, n_kv,
                           B, S, top_k)
    return matmul(o, w_out, layer, res=x2d, head_major=True, tm=2048, tn=512, name="dsa_out_proj")


def retention_layer(x2d, positions, gain, w_in, w_out, out_norm, layer, B, S):
    T, D = x2d.shape
    n_heads = D // RET_QK_DIM
    cos, sin = _rot_tables(positions, RET_QK_DIM, RET_THETA)
    hn = rmsnorm_cast(x2d, gain)
    qkv, g = ret_in_proj(hn, w_in, layer, cos, sin, n_heads)
    og = retention_core(qkv, g, out_norm, B, S, n_heads)
    return matmul(og, w_out.astype(BF16), layer, res=x2d, tm=2048, tn=256, name="ret_out_proj")


def kernel(x, p, positions, attn_norm, ffn_norm, ple_norm, a_w_in, a_w_out, a_q_norm, a_k_norm, a_idx_k_norm,
           b_w_in, b_w_out, b_out_norm, moe_w_group, moe_b_group, moe_w_expert, moe_b_expert, moe_w_gate,
           moe_w_up, moe_w_down, ple_proj, ple_gate):
    B, S, D = x.shape
    depth = p.shape[0]
    T = B * S
    x2d = x.reshape(T, D)
    p3d = p.reshape(depth, T, p.shape[-1])
    for i in range(depth):
        j = i // 2
        if i % 2 == 0:
            x2d = dsa_layer(x2d, positions, attn_norm[i], a_w_in, a_w_out, a_q_norm[j], a_k_norm[j],
                            a_idx_k_norm[j], j, B, S)
        else:
            x2d = retention_layer(x2d, positions, attn_norm[i], b_w_in, b_w_out, b_out_norm[j], j, B, S)
        x2d, xn = hier_moe_residual(x2d, ffn_norm[i], moe_w_group[i], moe_b_group[i], moe_w_expert[i],
                                    moe_b_expert[i], moe_w_gate, moe_w_up, moe_w_down, i, ple_norm[i])
        x2d = ple_residual(x2d, xn, ple_gate, p3d, ple_proj, i)
    return x2d.reshape(B, S, D)
```

```python
import functools

import jax
import jax.numpy as jnp
from jax import lax
from jax.experimental import pallas as pl
from jax.experimental.pallas import tpu as pltpu

F32 = jnp.float32
BF16 = jnp.bfloat16
I32 = jnp.int32

RMS_EPS = 1e-6
HEAD_DIM = 128
IDX_HEADS = 64
IDX_TOPK_MAX = 256
ROPE_THETA = 500000.0
RET_QK_DIM = 256
RET_THETA = 10000.0
RET_CHUNK = 128
N_GROUPS = 8
EXPERTS_PER_GROUP = 8
LANES = 128
VMEM_LIMIT_BYTES = 56 * 1024 * 1024
NEG_BIG = -1e30
INT_MIN = -2147483648


def _cparams(sem):
    return pltpu.CompilerParams(dimension_semantics=sem, vmem_limit_bytes=VMEM_LIMIT_BYTES)


def _pick(n, cands):
    for c in cands:
        if n % c == 0:
            return c
    raise ValueError(f"no tile for {n} in {cands}")


def _rmsnorm_kernel(x_ref, g_ref, o_ref):
    x = x_ref[...]
    ms = jnp.mean(x * x, axis=-1, keepdims=True)
    o_ref[...] = (x * lax.rsqrt(ms + RMS_EPS) * g_ref[...]).astype(o_ref.dtype)


def rmsnorm_cast(x2d, gain):
    T, D = x2d.shape
    tm = _pick(T, (512, 256, 128, 8))
    return pl.pallas_call(
        _rmsnorm_kernel,
        grid=(T // tm,),
        in_specs=[pl.BlockSpec((tm, D), lambda i: (i, 0)), pl.BlockSpec((1, D), lambda i: (0, 0))],
        out_specs=pl.BlockSpec((tm, D), lambda i: (i, 0)),
        out_shape=jax.ShapeDtypeStruct((T, D), BF16),
        compiler_params=_cparams(("parallel",)),
        name="rmsnorm_cast",
    )(x2d, gain.reshape(1, D))


ROW_CHUNK = 512


def _mm_kernel(*refs, head_major, has_res, row_chunk):
    a_ref, w_ref = refs[0], refs[1]
    r_ref = refs[2] if has_res else None
    o_ref = refs[2 + has_res]
    w = w_ref[...].astype(BF16)
    for rc in range(o_ref.shape[0] // row_chunk):
        rows = pl.ds(rc * row_chunk, row_chunk)
        if head_major:
            a = jnp.concatenate([a_ref[h, rows, :] for h in range(a_ref.shape[0])], axis=1)
        else:
            a = a_ref[rows, :]
        acc = jnp.dot(a, w, preferred_element_type=F32)
        if has_res:
            acc = acc + r_ref[rows, :]
        o_ref[rows, :] = acc.astype(o_ref.dtype)


def matmul(a, w, layer, *, col0=0, res=None, out_dtype=F32, tm=1024, tn=512, head_major=False, name="matmul"):
    if head_major:
        nh, M, hd = a.shape
        K = nh * hd
    else:
        M, K = a.shape
    n_out = w.shape[2] - col0
    tm = min(tm, M)
    assert col0 % tn == 0 and M % tm == 0 and n_out % tn == 0
    once = dict(pipeline_mode=pl.Buffered(1))
    if head_major:
        a_spec = pl.BlockSpec((nh, tm, hd), lambda i, j: (0, i, 0), **once)
    else:
        a_spec = pl.BlockSpec((tm, K), lambda i, j: (i, 0), **once)
    in_specs = [a_spec, pl.BlockSpec((None, K, tn), lambda i, j: (layer, 0, j + col0 // tn))]
    args = [a, w]
    if res is not None:
        in_specs.append(pl.BlockSpec((tm, tn), lambda i, j: (i, j)))
        args.append(res)
    return pl.pallas_call(
        functools.partial(_mm_kernel, head_major=head_major, has_res=res is not None,
                          row_chunk=min(ROW_CHUNK, tm)),
        grid=(M // tm, n_out // tn),
        in_specs=in_specs,
        out_specs=pl.BlockSpec((tm, tn), lambda i, j: (i, j)),
        out_shape=jax.ShapeDtypeStruct((M, n_out), out_dtype),
        compiler_params=_cparams(("parallel", "parallel")),
        name=name,
    )(*args)


def _rot_head(x, c, s1, s2):
    half = HEAD_DIM // 8
    return x * c + pltpu.roll(x, HEAD_DIM - half, 1) * s1 + pltpu.roll(x, half, 1) * s2


def _flag(pred, shape):
    return jnp.broadcast_to(pred.astype(I32), shape) == 1


def _dsa_proj_kernel(a_ref, w_ref, c_ref, s1_ref, s2_ref, gains_ref, heads_ref, wi_ref, *, n_valid, bounds,
                     wi_lane, wscale, row_chunk):
    j = pl.program_id(1)
    tm = a_ref.shape[0]
    tn = w_ref.shape[1]
    b_q, b_k, b_v, b_qi = bounds
    col = j * tn + lax.broadcasted_iota(I32, (1, tn), 1)
    w = jnp.where(col < n_valid, w_ref[...], 0.0).astype(BF16)
    is_q = j < b_q
    is_k = jnp.logical_and(j >= b_q, j < b_k)
    is_v = jnp.logical_and(j >= b_k, j < b_v)
    is_ki = j == b_qi
    lane = (1, HEAD_DIM)
    gain = jnp.where(_flag(is_q, lane), gains_ref[0:1, :],
                     jnp.where(_flag(is_k, lane), gains_ref[1:2, :],
                               jnp.where(_flag(is_ki, lane), gains_ref[2:3, :], 1.0)))
    use_norm = _flag(jnp.logical_or(jnp.logical_or(is_q, is_k), is_ki), (row_chunk, 1))
    no_rot = _flag(is_v, (row_chunk, HEAD_DIM))
    for rc in range(tm // row_chunk):
        rows = pl.ds(rc * row_chunk, row_chunk)
        acc = jnp.dot(a_ref[rows, :], w, preferred_element_type=F32)
        c = jnp.where(no_rot, 1.0, c_ref[rows, :])
        s1 = jnp.where(no_rot, 0.0, s1_ref[rows, :])
        s2 = jnp.where(no_rot, 0.0, s2_ref[rows, :])
        for hh in range(tn // HEAD_DIM):
            x = acc[:, hh * HEAD_DIM:(hh + 1) * HEAD_DIM]
            ms = jnp.mean(x * x, axis=-1, keepdims=True)
            x = x * jnp.where(use_norm, lax.rsqrt(ms + RMS_EPS), 1.0) * gain
            heads_ref[hh, rows, :] = _rot_head(x, c, s1, s2).astype(heads_ref.dtype)
        wi_ref[rows, :] = acc[:, wi_lane:wi_lane + HEAD_DIM] * wscale


def dsa_in_proj(hn, w_in, layer, tabs, q_gain, k_gain, ki_gain, n_heads, n_kv):
    T, K = hn.shape
    n_cols = w_in.shape[2]
    seg_ends = [n_heads * HEAD_DIM]
    for width in (n_kv * HEAD_DIM, n_kv * HEAD_DIM, IDX_HEADS * HEAD_DIM):
        seg_ends.append(seg_ends[-1] + width)
    tn = next(t for t in (512, 256, 128) if all(e % t == 0 for e in seg_ends))
    hb = tn // HEAD_DIM
    bounds = tuple(e // tn for e in seg_ends)
    b_v, b_qi = bounds[2], bounds[3]
    n_out = -(-n_cols // tn) * tn
    wi_lane = (seg_ends[-1] + HEAD_DIM) % tn
    assert n_cols == seg_ends[-1] + HEAD_DIM + IDX_HEADS and (n_out - tn) <= seg_ends[-1] + HEAD_DIM
    tm = min(2048, T)
    row_chunk = min(512, tm)
    c, s1, s2 = tabs
    gains = jnp.zeros((8, HEAD_DIM), F32).at[0].set(q_gain * (HEAD_DIM ** -0.5 * LOG2_E)).at[1].set(k_gain)
    gains = gains.at[2].set(ki_gain)
    once = dict(pipeline_mode=pl.Buffered(1))
    row = lambda i, j: (i, 0)

    def head_block(i, j):
        pos = jnp.where(j < b_v, j + (b_qi - b_v), jnp.where(j < b_qi, j - b_v, j))
        return (pos, i, 0)

    return pl.pallas_call(
        functools.partial(_dsa_proj_kernel, n_valid=n_cols, bounds=bounds, wi_lane=wi_lane,
                          wscale=IDX_HEADS ** -0.5 * HEAD_DIM ** -0.5, row_chunk=row_chunk),
        grid=(T // tm, n_out // tn),
        in_specs=[pl.BlockSpec((tm, K), row, **once),
                  pl.BlockSpec((None, K, tn), lambda i, j: (layer, 0, j)),
                  pl.BlockSpec((tm, HEAD_DIM), row, **once), pl.BlockSpec((tm, HEAD_DIM), row, **once),
                  pl.BlockSpec((tm, HEAD_DIM), row, **once), pl.BlockSpec((8, HEAD_DIM), lambda i, j: (0, 0))],
        out_specs=[pl.BlockSpec((hb, tm, HEAD_DIM), head_block), pl.BlockSpec((tm, HEAD_DIM), row)],
        out_shape=[jax.ShapeDtypeStruct((n_out // HEAD_DIM, T, HEAD_DIM), BF16),
                   jax.ShapeDtypeStruct((T, HEAD_DIM), F32)],
        compiler_params=_cparams(("parallel", "arbitrary")),
        name="dsa_in_proj",
    )(hn, w_in, c, s1, s2, gains)


_NT = (((1,), (1,)), ((), ()))


IDX_UNROLL = 4
LOG2_E = 1.4426950408889634
SOFTMAX_SHIFT_MAX = 40.0
NORM_SLACK = 1.0 + 2.0 ** -6


def _dsa_tile(bound_ref, qi_ref, ki_ref, wi_ref, q_ref, k_ref, v_ref, o_ref, sc_ref, u_ref, bias_ref,
              *, top_k, sub, steps_per_kv, nk, q0):
    tq = sc_ref.shape[1]
    keys = pl.ds(0, nk)

    def idx_term(h):
        s = lax.dot_general(ki_ref[keys, :], qi_ref[h], _NT, preferred_element_type=F32)
        return jnp.maximum(s, 0.0) * wi_ref[pl.ds(h, 1), :]

    def idx_terms(h0):
        acc = idx_term(h0)
        for d in range(1, IDX_UNROLL):
            acc = acc + idx_term(h0 + d)
        return acc

    sc_ref[keys, :] = idx_terms(0)

    def idx_body(g, carry):
        sc_ref[keys, :] += idx_terms(g * IDX_UNROLL)
        return carry

    lax.fori_loop(1, IDX_HEADS // IDX_UNROLL, idx_body, 0)

    key_pos = lax.broadcasted_iota(I32, (nk, tq), 0)
    q_pos = q0 + lax.broadcasted_iota(I32, (nk, tq), 1)
    causal = key_pos <= q_pos
    bits = pltpu.bitcast(sc_ref[keys, :], I32)
    u = bits ^ ((bits >> 31) & 0x7FFFFFFF)
    u_ref[keys, :] = jnp.where(causal, u, INT_MIN)

    def count_ge(cand):
        return jnp.sum(jnp.where(u_ref[keys, :] >= cand, 1.0, 0.0), axis=0, keepdims=True)

    zero = jnp.zeros((1, tq), I32)
    v0 = jnp.where(count_ge(zero) >= top_k, zero, jnp.full((1, tq), INT_MIN, I32))

    def bis_body(it, v):
        cand = v + lax.shift_left(jnp.int32(1), 30 - it)
        return jnp.where(count_ge(cand) >= top_k, cand, v)

    thr = lax.fori_loop(0, 31, bis_body, v0)
    keep = jnp.logical_and(u_ref[keys, :] >= thr, causal)
    bias_ref[:, keys] = jnp.transpose(jnp.where(keep, 0.0, NEG_BIG))

    bound = bound_ref[0]
    bound_ok = bound <= SOFTMAX_SHIFT_MAX

    def att_body(shift_by_bound, st, carry):
        n = st // steps_per_kv
        q = q_ref[pl.ds(st * sub, sub)].reshape(sub * tq, HEAD_DIM)
        lg = lax.dot_general(q, k_ref[n, keys, :], _NT, preferred_element_type=F32)
        ps, ls = [], []
        for g in range(sub):
            x = lg[g * tq:(g + 1) * tq] + bias_ref[:, keys]
            shift = bound if shift_by_bound else jnp.max(x, axis=-1, keepdims=True)
            p = jnp.exp2(x - shift)
            ls.append(jnp.sum(p, axis=-1, keepdims=True))
            ps.append(p.astype(BF16))
        o = jnp.dot(jnp.concatenate(ps, axis=0), v_ref[n, keys, :], preferred_element_type=F32)
        for g in range(sub):
            o_ref[st * sub + g] = (o[g * tq:(g + 1) * tq] / ls[g]).astype(o_ref.dtype)
        return carry

    n_steps = q_ref.shape[0] // sub

    @pl.when(bound_ok)
    def _():
        lax.fori_loop(0, n_steps, functools.partial(att_body, True), 0)

    @pl.when(jnp.logical_not(bound_ok))
    def _():
        lax.fori_loop(0, n_steps, functools.partial(att_body, False), 0)


def _dsa_kernel(*refs, **static):
    S, tq = refs[8].shape
    i = pl.program_id(1)
    for c in range(S // tq):
        @pl.when(i == c)
        def _(c=c):
            _dsa_tile(*refs, nk=(c + 1) * tq, q0=c * tq, **static)


def dsa_attention_core(heads, wi, logit_bound, n_heads, n_kv, B, S, top_k):
    T = heads.shape[1]
    tq = _pick(S, (256, 128))
    nq = S // tq
    group = n_heads // n_kv
    sub = 2 if group % 2 == 0 else 1
    off_q, off_k = IDX_HEADS, IDX_HEADS + n_heads
    off_v, off_ki = off_k + n_kv, off_k + 2 * n_kv
    assert off_q % n_heads == 0 and off_k % n_kv == 0 and off_v % n_kv == 0
    once = dict(pipeline_mode=pl.Buffered(1))
    return pl.pallas_call(
        functools.partial(_dsa_kernel, top_k=top_k, sub=sub, steps_per_kv=group // sub),
        grid=(B, nq),
        in_specs=[pl.BlockSpec(memory_space=pltpu.SMEM),
                  pl.BlockSpec((IDX_HEADS, tq, HEAD_DIM), lambda b, i: (0, b * nq + i, 0)),
                  pl.BlockSpec((None, S, HEAD_DIM), lambda b, i: (off_ki, b, 0), **once),
                  pl.BlockSpec((IDX_HEADS, tq), lambda b, i: (0, b * nq + i)),
                  pl.BlockSpec((n_heads, tq, HEAD_DIM), lambda b, i: (off_q // n_heads, b * nq + i, 0)),
                  pl.BlockSpec((n_kv, S, HEAD_DIM), lambda b, i: (off_k // n_kv, b, 0), **once),
                  pl.BlockSpec((n_kv, S, HEAD_DIM), lambda b, i: (off_v // n_kv, b, 0), **once)],
        out_specs=pl.BlockSpec((n_heads, tq, HEAD_DIM), lambda b, i: (0, b * nq + i, 0)),
        out_shape=jax.ShapeDtypeStruct((n_heads, T, HEAD_DIM), BF16),
        scratch_shapes=[pltpu.VMEM((S, tq), F32), pltpu.VMEM((S, tq), I32), pltpu.VMEM((tq, S), F32)],
        compiler_params=_cparams(("parallel", "arbitrary")),
        name="dsa_core",
    )(logit_bound, heads, heads, wi, heads, heads, heads)


def _pack_bf16_pair(t):
    n = t.shape[1] // 2
    hi = pltpu.bitcast(t[:, :n].astype(BF16).astype(F32), jnp.uint32)
    lo = pltpu.bitcast(t[:, n:].astype(BF16).astype(F32), jnp.uint32)
    return hi | lax.shift_right_logical(lo, jnp.uint32(16))


def _route_kernel(x_ref, g_ref, w_ref, b_ref, o_ref, tp_ref):
    x = x_ref[...]
    ms = jnp.mean(x * x, axis=-1, keepdims=True)
    t = x * lax.rsqrt(ms + RMS_EPS) * g_ref[...]
    tp_ref[...] = _pack_bf16_pair(t)
    t_hi = t.astype(BF16)
    t_lo = (t - t_hi.astype(F32)).astype(BF16)
    w_hi, w_lo = w_ref[0], w_ref[1]
    logits = (jnp.dot(t_hi, w_hi, preferred_element_type=F32) + jnp.dot(t_hi, w_lo, preferred_element_type=F32)
              + jnp.dot(t_lo, w_hi, preferred_element_type=F32)) + b_ref[...]
    lane = lax.broadcasted_iota(I32, logits.shape, 1)
    lane_f = lane.astype(F32)
    big = float(LANES)
    gl = jnp.where(lane < N_GROUPS, logits, -jnp.inf)
    gmax = jnp.max(gl, axis=-1, keepdims=True)
    g_sel = jnp.min(jnp.where(gl == gmax, lane_f, big), axis=-1, keepdims=True)
    g_w = 1.0 / jnp.sum(jnp.exp(gl - gmax), axis=-1, keepdims=True)
    lo = N_GROUPS + g_sel * EXPERTS_PER_GROUP
    in_grp = jnp.logical_and(lane_f >= lo, lane_f < lo + EXPERTS_PER_GROUP)
    el = jnp.where(in_grp, logits, -jnp.inf)
    m1 = jnp.max(el, axis=-1, keepdims=True)
    i1 = jnp.min(jnp.where(el == m1, lane_f, big), axis=-1, keepdims=True)
    el2 = jnp.where(lane_f == i1, -jnp.inf, el)
    m2 = jnp.max(el2, axis=-1, keepdims=True)
    i2 = jnp.min(jnp.where(el2 == m2, lane_f, big), axis=-1, keepdims=True)
    z = jnp.sum(jnp.exp(el - m1), axis=-1, keepdims=True)
    p1 = 1.0 / z
    p2 = jnp.exp(m2 - m1) / z
    w1 = p1 / (p1 + p2) * g_w
    w2 = p2 / (p1 + p2) * g_w
    out = jnp.where(lane == 0, i1 - N_GROUPS, jnp.where(lane == 1, i2 - N_GROUPS,
          jnp.where(lane == 2, w1, jnp.where(lane == 3, w2, 0.0))))
    o_ref[...] = out


def moe_route(x2d, gain, w_group, b_group, w_expert, b_expert):
    T, D = x2d.shape
    n_log = w_group.shape[1] + w_expert.shape[1]
    w = jnp.concatenate([w_group, w_expert, jnp.zeros((D, LANES - n_log), F32)], axis=1)
    b = jnp.concatenate([b_group, b_expert, jnp.zeros((LANES - n_log,), F32)]).reshape(1, LANES)
    w_hi = w.astype(BF16)
    w = jnp.stack([w_hi, (w - w_hi.astype(F32)).astype(BF16)])
    tm = _pick(T, (256, 128, 8))
    return pl.pallas_call(
        _route_kernel,
        grid=(T // tm,),
        in_specs=[pl.BlockSpec((tm, D), lambda i: (i, 0)), pl.BlockSpec((1, D), lambda i: (0, 0)),
                  pl.BlockSpec((2, D, LANES), lambda i: (0, 0, 0)), pl.BlockSpec((1, LANES), lambda i: (0, 0))],
        out_specs=[pl.BlockSpec((tm, LANES), lambda i: (i, 0)), pl.BlockSpec((tm, D // 2), lambda i: (i, 0))],
        out_shape=[jax.ShapeDtypeStruct((T, LANES), F32), jax.ShapeDtypeStruct((T, D // 2), jnp.uint32)],
        compiler_params=_cparams(("parallel",)),
        name="moe_route",
    )(x2d, gain.reshape(1, D), w, b)


def _row_copy(src_hbm, dst_ref, sem, src_row, dst_row):
    return pltpu.make_async_copy(src_hbm.at[pl.ds(src_row, 1), :], dst_ref.at[pl.ds(dst_row, 1), :], sem)


DMA_PRIORITIES = 2


def _start_row_gather(idx_ref, src_hbm, dst_ref, sem, base, n_rows, priorities=(0, 1)):
    def start(g, c):
        for d, pr in enumerate(priorities):
            r = g * len(priorities) + d
            _row_copy(src_hbm, dst_ref, sem, idx_ref[base + r], r).start(priority=pr)
        return c

    lax.fori_loop(0, n_rows // len(priorities), start, 0)


def _wait_row_gather(src_hbm, dst_ref, sem, n_rows):
    pltpu.make_async_copy(src_hbm.at[pl.ds(0, n_rows), :], dst_ref, sem).wait()


FFN_SLOTS = 3
FFN_GATHER_PRIORITIES = (1, 1)


def _unpack_bf16_pair(words):
    hi = pltpu.bitcast(words & jnp.uint32(0xFFFF0000), F32).astype(BF16)
    lo = pltpu.bitcast(lax.shift_left(words, jnp.uint32(16)), F32).astype(BF16)
    return jnp.concatenate([hi, lo], axis=1)


def _ffn_kernel(src_ref, te_ref, nu_ref, t_hbm, wg_ref, wu_ref, wd_ref, o_ref, xbuf, sem, *, priorities):
    j = pl.program_id(0)
    tm = o_ref.shape[0]
    n_used = nu_ref[0]
    ahead = FFN_SLOTS - 1

    def fetch(tile, slot):
        _start_row_gather(src_ref, t_hbm, xbuf.at[slot], sem.at[slot], tile * tm, tm, priorities)

    for t in range(ahead):
        @pl.when(jnp.logical_and(j == 0, t < n_used))
        def _(t=t):
            fetch(t, t)

    @pl.when(j + ahead < n_used)
    def _():
        fetch(j + ahead, lax.rem(j + ahead, FFN_SLOTS))

    @pl.when(j < n_used)
    def _():
        slot = lax.rem(j, FFN_SLOTS)
        _wait_row_gather(t_hbm, xbuf.at[slot], sem.at[slot], tm)
        t = _unpack_bf16_pair(xbuf[slot])
        a = jnp.dot(t, wg_ref[0].astype(BF16), preferred_element_type=F32)
        u = jnp.dot(t, wu_ref[0].astype(BF16), preferred_element_type=F32)
        hid = a * jax.nn.sigmoid(a) * u
        o_ref[...] = jnp.dot(hid.astype(BF16), wd_ref[0].astype(BF16), preferred_element_type=F32)

    @pl.when(j >= n_used)
    def _():
        o_ref[...] = jnp.zeros_like(o_ref)


def moe_ffn(t_packed, src_tok, w_gate, w_up, w_down, tile_expert, n_used, tm, priorities):
    D = 2 * t_packed.shape[1]
    P = src_tok.shape[0]
    F = w_gate.shape[-1]
    assert P % tm == 0 and tm % DMA_PRIORITIES == 0
    wmap = lambda j, src, te, nu: (te[j], 0, 0)
    return pl.pallas_call(
        functools.partial(_ffn_kernel, priorities=priorities),
        grid_spec=pltpu.PrefetchScalarGridSpec(
            num_scalar_prefetch=3, grid=(P // tm,),
            in_specs=[pl.BlockSpec(memory_space=pl.ANY),
                      pl.BlockSpec((1, D, F), wmap), pl.BlockSpec((1, D, F), wmap),
                      pl.BlockSpec((1, F, D), wmap)],
            out_specs=pl.BlockSpec((tm, D), lambda j, src, te, nu: (j, 0)),
            scratch_shapes=[pltpu.VMEM((FFN_SLOTS, tm, D // 2), jnp.uint32),
                            pltpu.SemaphoreType.DMA((FFN_SLOTS,))]),
        out_shape=jax.ShapeDtypeStruct((P, D), F32),
        compiler_params=_cparams(("arbitrary",)),
        name="moe_ffn",
    )(src_tok, tile_expert, n_used, t_packed, w_gate, w_up, w_down)


ROUTE_W_LANE = 2


def _combine_kernel(p1_ref, p2_ref, ys_hbm, x_ref, route_ref, gain_ref, o_ref, on_ref, buf, sem):
    i = pl.program_id(0)
    R = x_ref.shape[0]

    def fetch(tile, slot):
        _start_row_gather(p1_ref, ys_hbm, buf.at[2 * slot], sem.at[2 * slot], tile * R, R)
        _start_row_gather(p2_ref, ys_hbm, buf.at[2 * slot + 1], sem.at[2 * slot + 1], tile * R, R)

    @pl.when(i == 0)
    def _():
        fetch(0, 0)

    @pl.when(i + 1 < pl.num_programs(0))
    def _():
        fetch(i + 1, lax.rem(i + 1, 2))

    slot = lax.rem(i, 2)
    _wait_row_gather(ys_hbm, buf.at[2 * slot], sem.at[2 * slot], R)
    _wait_row_gather(ys_hbm, buf.at[2 * slot + 1], sem.at[2 * slot + 1], R)
    w1 = route_ref[:, ROUTE_W_LANE:ROUTE_W_LANE + 1]
    w2 = route_ref[:, ROUTE_W_LANE + 1:ROUTE_W_LANE + 2]
    y = x_ref[...] + (buf[2 * slot] * w1 + buf[2 * slot + 1] * w2)
    o_ref[...] = y
    ms = jnp.mean(y * y, axis=-1, keepdims=True)
    on_ref[...] = (y * lax.rsqrt(ms + RMS_EPS) * gain_ref[...]).astype(on_ref.dtype)


def moe_combine(x2d, ys, pos1, pos2, route, next_gain, rows_per_step):
    T, D = x2d.shape
    R = rows_per_step
    tile = lambda i, a, b: (i, 0)
    return pl.pallas_call(
        _combine_kernel,
        grid_spec=pltpu.PrefetchScalarGridSpec(
            num_scalar_prefetch=2, grid=(T // R,),
            in_specs=[pl.BlockSpec(memory_space=pl.ANY), pl.BlockSpec((R, D), tile),
                      pl.BlockSpec((R, LANES), tile), pl.BlockSpec((1, D), lambda i, a, b: (0, 0))],
            out_specs=[pl.BlockSpec((R, D), tile), pl.BlockSpec((R, D), tile)],
            scratch_shapes=[pltpu.VMEM((4, R, D), F32), pltpu.SemaphoreType.DMA((4,))]),
        out_shape=[jax.ShapeDtypeStruct((T, D), F32), jax.ShapeDtypeStruct((T, D), BF16)],
        compiler_params=_cparams(("arbitrary",)),
        name="moe_combine",
    )(pos1, pos2, ys, x2d, route, next_gain.reshape(1, D))


def hier_moe_residual(x2d, gain, w_group, b_group, w_expert, b_expert, w_gate, w_up, w_down, layer, next_gain):
    T, D = x2d.shape
    depth, G, E = w_gate.shape[0], w_gate.shape[1], w_gate.shape[2]
    n_exp = G * E
    F = w_gate.shape[-1]
    route, t_packed = moe_route(x2d, gain, w_group, b_group, w_expert, b_expert)
    eid = route[:, 0:2].astype(I32).reshape(2 * T)
    tm = 256 if 2 * T >= 4096 else 128
    nt = (2 * T) // tm + n_exp
    onehot = (eid[None, :] == jnp.arange(n_exp, dtype=I32)[:, None]).astype(I32)
    csum = jnp.cumsum(onehot, axis=1)
    counts = csum[:, -1]
    rank = jnp.sum(onehot * csum, axis=0) - 1
    tiles = (counts + tm - 1) // tm
    tile_end = jnp.cumsum(tiles)
    tile_start = tile_end - tiles
    n_used = tile_end[-1:]
    j = jnp.arange(nt, dtype=I32)
    tile_expert = jnp.minimum(jnp.sum(tile_end[None, :] <= j[:, None], axis=1, dtype=I32), n_exp - 1)
    last_expert = jnp.sum(tile_end < n_used[0], dtype=I32)
    tile_expert = jnp.where(j < n_used[0], tile_expert, last_expert)
    pos = (tile_start[eid] * tm + rank).astype(I32)
    src_tok = jnp.zeros((nt * tm,), I32).at[pos].set(jnp.arange(2 * T, dtype=I32) // 2)
    pos = pos.reshape(T, 2)

    ys = moe_ffn(t_packed, src_tok, w_gate.reshape(depth * n_exp, D, F), w_up.reshape(depth * n_exp, D, F),
                 w_down.reshape(depth * n_exp, F, D), tile_expert + layer * n_exp, n_used, tm,
                 FFN_GATHER_PRIORITIES)
    return moe_combine(x2d, ys, pos[:, 0], pos[:, 1], route, next_gain, min(256, T))


def _ple_kernel(a_ref, w_ref, p_ref, pw_ref, x_ref, o_ref, *, row_chunk):
    w = w_ref[...].astype(BF16)
    pw = pw_ref[...].astype(BF16)
    for rc in range(o_ref.shape[0] // row_chunk):
        rows = pl.ds(rc * row_chunk, row_chunk)
        z = jnp.dot(a_ref[rows, :], w, preferred_element_type=F32)
        e = jnp.dot(p_ref[rows, :].astype(BF16), pw, preferred_element_type=F32)
        o_ref[rows, :] = x_ref[rows, :] + jax.nn.sigmoid(z) * e


def ple_residual(x2d, a, w_gate, p3d, w_proj, layer):
    T, D = x2d.shape
    P = p3d.shape[2]
    tm, tn = min(2048, T), 256
    return pl.pallas_call(
        functools.partial(_ple_kernel, row_chunk=min(ROW_CHUNK, tm)),
        grid=(T // tm, D // tn),
        in_specs=[pl.BlockSpec((tm, D), lambda i, j: (i, 0), pipeline_mode=pl.Buffered(1)),
                  pl.BlockSpec((None, D, tn), lambda i, j: (layer, 0, j)),
                  pl.BlockSpec((None, tm, P), lambda i, j: (layer, i, 0)),
                  pl.BlockSpec((None, P, tn), lambda i, j: (layer, 0, j)),
                  pl.BlockSpec((tm, tn), lambda i, j: (i, j))],
        out_specs=pl.BlockSpec((tm, tn), lambda i, j: (i, j)),
        out_shape=jax.ShapeDtypeStruct((T, D), F32),
        compiler_params=_cparams(("parallel", "parallel")),
        name="ple",
    )(a, w_gate, p3d, w_proj, x2d)


def _ret_proj_kernel(a_ref, w_ref, cos_ref, sin_ref, qkv_ref, *, n_qk, row_chunk):
    j = pl.program_id(1)
    tm = a_ref.shape[0]
    tn = w_ref.shape[1]
    half = RET_QK_DIM // 2
    w = w_ref[...].astype(BF16)
    shape = (row_chunk, half)
    is_rot = _flag(j < n_qk, shape)
    kscale = jnp.where(_flag(jnp.logical_and(j >= n_qk // 2, j < n_qk), shape), RET_QK_DIM ** -0.5, 1.0)
    for rc in range(tm // row_chunk):
        rows = pl.ds(rc * row_chunk, row_chunk)
        acc = jnp.dot(a_ref[rows, :], w, preferred_element_type=F32)
        cos = jnp.where(is_rot, cos_ref[rows, :], 1.0) * kscale
        sin = jnp.where(is_rot, sin_ref[rows, :], 0.0) * kscale
        for hh in range(tn // RET_QK_DIM):
            x1 = acc[:, hh * RET_QK_DIM:hh * RET_QK_DIM + half]
            x2 = acc[:, hh * RET_QK_DIM + half:(hh + 1) * RET_QK_DIM]
            qkv_ref[rows, hh * RET_QK_DIM:hh * RET_QK_DIM + half] = (x1 * cos - x2 * sin).astype(BF16)
            qkv_ref[rows, hh * RET_QK_DIM + half:(hh + 1) * RET_QK_DIM] = (x1 * sin + x2 * cos).astype(BF16)


def ret_in_proj(hn, w_in, layer, cos, sin, n_heads):
    T, K = hn.shape
    dk = RET_QK_DIM
    qkv_cols = 4 * n_heads * dk
    tn = 512 if (n_heads * dk) % 512 == 0 else dk
    tm = min(2048, T)
    once = dict(pipeline_mode=pl.Buffered(1))
    row = lambda i, j: (i, 0)
    qkv = pl.pallas_call(
        functools.partial(_ret_proj_kernel, n_qk=2 * n_heads * dk // tn, row_chunk=min(512, tm)),
        grid=(T // tm, qkv_cols // tn),
        in_specs=[pl.BlockSpec((tm, K), row, **once),
                  pl.BlockSpec((None, K, tn), lambda i, j: (layer, 0, j)),
                  pl.BlockSpec((tm, dk // 2), row, **once), pl.BlockSpec((tm, dk // 2), row, **once)],
        out_specs=pl.BlockSpec((tm, tn), lambda i, j: (i, j)),
        out_shape=jax.ShapeDtypeStruct((T, qkv_cols), BF16),
        compiler_params=_cparams(("parallel", "parallel")),
        name="ret_qkv_proj",
    )(hn, w_in, cos, sin)
    g = matmul(hn, w_in, layer, col0=qkv_cols, tm=tm, tn=tn, name="ret_g_proj")
    return qkv, g


RET_HEADS_PER_STEP = 2


def _retention_kernel(q_ref, k_ref, v_ref, g_ref, intra_ref, qd_ref, kd_ref, cd_ref, gn_ref, o_ref, state_ref):
    S = q_ref.shape[0]
    C = intra_ref.shape[1]
    dk = RET_QK_DIM
    dv = 2 * dk
    state_ref[...] = jnp.zeros_like(state_ref)

    def chunk(c, carry):
        rows = pl.ds(pl.multiple_of(c * C, C), C)
        for hh in range(RET_HEADS_PER_STEP):
            qc = q_ref[rows, hh * dk:(hh + 1) * dk]
            kc = k_ref[rows, hh * dk:(hh + 1) * dk]
            vc = v_ref[rows, hh * dv:(hh + 1) * dv]
            att = lax.dot_general(qc, kc, _NT, preferred_element_type=F32) * intra_ref[hh]
            inner = jnp.dot(att.astype(BF16), vc, preferred_element_type=F32)
            state = state_ref[hh]
            cross = jnp.dot((qc.astype(F32) * qd_ref[hh]).astype(BF16), state.astype(BF16),
                            preferred_element_type=F32)
            kt = jnp.transpose(kc.astype(F32) * kd_ref[hh]).astype(BF16)
            state_ref[hh] = state * cd_ref[hh] + jnp.dot(kt, vc, preferred_element_type=F32)
            o = inner + cross
            o = o * lax.rsqrt(jnp.mean(o * o, axis=-1, keepdims=True) + RMS_EPS) * gn_ref[hh]
            g = g_ref[rows, hh * dv:(hh + 1) * dv]
            o_ref[rows, hh * dv:(hh + 1) * dv] = (g * jax.nn.sigmoid(g) * o).astype(o_ref.dtype)
        return carry

    lax.fori_loop(0, S // C, chunk, 0)


def retention_core(qkv, g, out_norm, B, S, n_heads):
    T = qkv.shape[0]
    dk = RET_QK_DIM
    dv = 2 * dk
    C = RET_CHUNK
    hs = RET_HEADS_PER_STEP
    assert n_heads % hs == 0
    log_gamma = jnp.log1p(-jnp.exp2(-5.0 - jnp.arange(n_heads, dtype=F32)))
    i = jnp.arange(C, dtype=F32)
    dist = i[:, None] - i[None, :]
    intra = jnp.where(dist[None] >= 0, jnp.exp(log_gamma[:, None, None] * jnp.maximum(dist, 0.0)[None]), 0.0)
    q_decay = jnp.broadcast_to(jnp.exp(log_gamma[:, None] * (i[None, :] + 1.0))[:, :, None], (n_heads, C, dk))
    k_decay = jnp.broadcast_to(jnp.exp(log_gamma[:, None] * (C - 1.0 - i)[None, :])[:, :, None], (n_heads, C, dk))
    chunk_decay = jnp.broadcast_to(jnp.exp(log_gamma * C)[:, None, None], (n_heads, 1, dv))
    nq = n_heads // hs
    per_head = lambda b, h: (h, 0, 0)
    return pl.pallas_call(
        _retention_kernel,
        grid=(B, nq),
        in_specs=[pl.BlockSpec((S, hs * dk), lambda b, h: (b, h)),
                  pl.BlockSpec((S, hs * dk), lambda b, h: (b, nq + h)),
                  pl.BlockSpec((S, hs * dv), lambda b, h: (b, nq + h)),
                  pl.BlockSpec((S, hs * dv), lambda b, h: (b, h)),
                  pl.BlockSpec((hs, C, C), per_head), pl.BlockSpec((hs, C, dk), per_head),
                  pl.BlockSpec((hs, C, dk), per_head), pl.BlockSpec((hs, 1, dv), per_head),
                  pl.BlockSpec((hs, 1, dv), per_head)],
        out_specs=pl.BlockSpec((S, hs * dv), lambda b, h: (b, h)),
        out_shape=jax.ShapeDtypeStruct((T, n_heads * dv), BF16),
        scratch_shapes=[pltpu.VMEM((hs, dk, dv), F32)],
        compiler_params=_cparams(("parallel", "arbitrary")),
        name="retention_core",
    )(qkv, qkv, qkv, g, intra, q_decay, k_decay, chunk_decay, out_norm.reshape(n_heads, 1, dv))


def _rot_tables(positions, dim, theta):
    inv = theta ** (-jnp.arange(0, dim, 2, dtype=F32) / dim)
    ang = positions.astype(F32).reshape(-1, 1) * inv[None, :]
    return jnp.cos(ang), jnp.sin(ang)


def dsa_layer(x2d, positions, gain, w_in, w_out, q_norm, k_norm, idx_k_norm, layer, B, S):
    T, D = x2d.shape
    n_heads = D // HEAD_DIM
    n_kv = n_heads // 4
    rot_dim = HEAD_DIM // 4
    cos, sin = _rot_tables(positions, rot_dim, ROPE_THETA)
    ones = jnp.ones((T, HEAD_DIM - rot_dim), F32)
    zeros_r = jnp.zeros((T, HEAD_DIM - rot_dim // 2), F32)
    tabs = (jnp.concatenate([cos, cos, ones], axis=1),
            jnp.concatenate([-sin, zeros_r], axis=1),
            jnp.concatenate([jnp.zeros_like(sin), sin, jnp.zeros((T, HEAD_DIM - rot_dim), F32)], axis=1))
    hn = rmsnorm_cast(x2d, gain)
    heads, wi = dsa_in_proj(hn, w_in, layer, tabs, q_norm, k_norm, idx_k_norm, n_heads, n_kv)
    top_k = min(IDX_TOPK_MAX, S // 4)
    q_cap = HEAD_DIM ** 0.5 * jnp.max(jnp.abs(q_norm)) * (HEAD_DIM ** -0.5 * LOG2_E) * NORM_SLACK
    k_cap = HEAD_DIM ** 0.5 * jnp.max(jnp.abs(k_norm)) * NORM_SLACK
    o = dsa_attention_core(heads, jnp.transpose(wi[:, :IDX_HEADS]), (q_cap * k_cap).reshape(1), n_heads, n_kv,
                           B, S, top_k)
    return matmul(o, w_out, layer, res=x2d, head_major=True, tm=2048, tn=512, name="dsa_out_proj")


def retention_layer(x2d, positions, gain, w_in, w_out, out_norm, layer, B, S):
    T, D = x2d.shape
    n_heads = D // RET_QK_DIM
    cos, sin = _rot_tables(positions, RET_QK_DIM, RET_THETA)
    hn = rmsnorm_cast(x2d, gain)
    qkv, g = ret_in_proj(hn, w_in, layer, cos, sin, n_heads)
    og = retention_core(qkv, g, out_norm, B, S, n_heads)
    return matmul(og, w_out.astype(BF16), layer, res=x2d, tm=2048, tn=256, name="ret_out_proj")


def kernel(x, p, positions, attn_norm, ffn_norm, ple_norm, a_w_in, a_w_out, a_q_norm, a_k_norm, a_idx_k_norm,
           b_w_in, b_w_out, b_out_norm, moe_w_group, moe_b_group, moe_w_expert, moe_b_expert, moe_w_gate,
           moe_w_up, moe_w_down, ple_proj, ple_gate):
    B, S, D = x.shape
    depth = p.shape[0]
    T = B * S
    x2d = x.reshape(T, D)
    p3d = p.reshape(depth, T, p.shape[-1])
    for i in range(depth):
        j = i // 2
        if i % 2 == 0:
            x2d = dsa_layer(x2d, positions, attn_norm[i], a_w_in, a_w_out, a_q_norm[j], a_k_norm[j],
                            a_idx_k_norm[j], j, B, S)
        else:
            x2d = retention_layer(x2d, positions, attn_norm[i], b_w_in, b_w_out, b_out_norm[j], j, B, S)
        x2d, xn = hier_moe_residual(x2d, ffn_norm[i], moe_w_group[i], moe_b_group[i], moe_w_expert[i],
                                    moe_b_expert[i], moe_w_gate, moe_w_up, moe_w_down, i, ple_norm[i])
        x2d = ple_residual(x2d, xn, ple_gate, p3d, ple_proj, i)
    return x2d.reshape(B, S, D)
```

```python
import functools

import jax
import jax.numpy as jnp
from jax import lax
from jax.experimental import pallas as pl
from jax.experimental.pallas import tpu as pltpu

F32 = jnp.float32
BF16 = jnp.bfloat16
I32 = jnp.int32

RMS_EPS = 1e-6
HEAD_DIM = 128
IDX_HEADS = 64
IDX_TOPK_MAX = 256
ROPE_THETA = 500000.0
RET_QK_DIM = 256
RET_THETA = 10000.0
RET_CHUNK = 128
N_GROUPS = 8
EXPERTS_PER_GROUP = 8
LANES = 128
VMEM_LIMIT_BYTES = 56 * 1024 * 1024
NEG_BIG = -1e30
INT_MIN = -2147483648


def _cparams(sem):
    return pltpu.CompilerParams(dimension_semantics=sem, vmem_limit_bytes=VMEM_LIMIT_BYTES)


def _pick(n, cands):
    for c in cands:
        if n % c == 0:
            return c
    raise ValueError(f"no tile for {n} in {cands}")


def _rmsnorm_kernel(x_ref, g_ref, o_ref):
    x = x_ref[...]
    ms = jnp.mean(x * x, axis=-1, keepdims=True)
    o_ref[...] = (x * lax.rsqrt(ms + RMS_EPS) * g_ref[...]).astype(o_ref.dtype)


def rmsnorm_cast(x2d, gain):
    T, D = x2d.shape
    tm = _pick(T, (512, 256, 128, 8))
    return pl.pallas_call(
        _rmsnorm_kernel,
        grid=(T // tm,),
        in_specs=[pl.BlockSpec((tm, D), lambda i: (i, 0)), pl.BlockSpec((1, D), lambda i: (0, 0))],
        out_specs=pl.BlockSpec((tm, D), lambda i: (i, 0)),
        out_shape=jax.ShapeDtypeStruct((T, D), BF16),
        compiler_params=_cparams(("parallel",)),
        name="rmsnorm_cast",
    )(x2d, gain.reshape(1, D))


ROW_CHUNK = 512


def _mm_kernel(*refs, head_major, has_res, row_chunk):
    a_ref, w_ref = refs[0], refs[1]
    r_ref = refs[2] if has_res else None
    o_ref = refs[2 + has_res]
    w = w_ref[...].astype(BF16)
    for rc in range(o_ref.shape[0] // row_chunk):
        rows = pl.ds(rc * row_chunk, row_chunk)
        if head_major:
            a = jnp.concatenate([a_ref[h, rows, :] for h in range(a_ref.shape[0])], axis=1)
        else:
            a = a_ref[rows, :]
        acc = jnp.dot(a, w, preferred_element_type=F32)
        if has_res:
            acc = acc + r_ref[rows, :]
        o_ref[rows, :] = acc.astype(o_ref.dtype)


def matmul(a, w, layer, *, col0=0, res=None, out_dtype=F32, tm=1024, tn=512, head_major=False, name="matmul"):
    if head_major:
        nh, M, hd = a.shape
        K = nh * hd
    else:
        M, K = a.shape
    n_out = w.shape[2] - col0
    tm = min(tm, M)
    assert col0 % tn == 0 and M % tm == 0 and n_out % tn == 0
    once = dict(pipeline_mode=pl.Buffered(1))
    if head_major:
        a_spec = pl.BlockSpec((nh, tm, hd), lambda i, j: (0, i, 0), **once)
    else:
        a_spec = pl.BlockSpec((tm, K), lambda i, j: (i, 0), **once)
    in_specs = [a_spec, pl.BlockSpec((None, K, tn), lambda i, j: (layer, 0, j + col0 // tn))]
    args = [a, w]
    if res is not None:
        in_specs.append(pl.BlockSpec((tm, tn), lambda i, j: (i, j)))
        args.append(res)
    return pl.pallas_call(
        functools.partial(_mm_kernel, head_major=head_major, has_res=res is not None,
                          row_chunk=min(ROW_CHUNK, tm)),
        grid=(M // tm, n_out // tn),
        in_specs=in_specs,
        out_specs=pl.BlockSpec((tm, tn), lambda i, j: (i, j)),
        out_shape=jax.ShapeDtypeStruct((M, n_out), out_dtype),
        compiler_params=_cparams(("parallel", "parallel")),
        name=name,
    )(*args)


def _rot_head(x, c, s1, s2):
    half = HEAD_DIM // 8
    return x * c + pltpu.roll(x, HEAD_DIM - half, 1) * s1 + pltpu.roll(x, half, 1) * s2


def _flag(pred, shape):
    return jnp.broadcast_to(pred.astype(I32), shape) == 1


def _dsa_proj_kernel(a_ref, w_ref, c_ref, s1_ref, s2_ref, gains_ref, heads_ref, wi_ref, *, n_valid, bounds,
                     wi_lane, wscale, row_chunk):
    j = pl.program_id(1)
    tm = a_ref.shape[0]
    tn = w_ref.shape[1]
    b_q, b_k, b_v, b_qi = bounds
    col = j * tn + lax.broadcasted_iota(I32, (1, tn), 1)
    w = jnp.where(col < n_valid, w_ref[...], 0.0).astype(BF16)
    is_q = j < b_q
    is_k = jnp.logical_and(j >= b_q, j < b_k)
    is_v = jnp.logical_and(j >= b_k, j < b_v)
    is_ki = j == b_qi
    lane = (1, HEAD_DIM)
    gain = jnp.where(_flag(is_q, lane), gains_ref[0:1, :],
                     jnp.where(_flag(is_k, lane), gains_ref[1:2, :],
                               jnp.where(_flag(is_ki, lane), gains_ref[2:3, :], 1.0)))
    use_norm = _flag(jnp.logical_or(jnp.logical_or(is_q, is_k), is_ki), (row_chunk, 1))
    no_rot = _flag(is_v, (row_chunk, HEAD_DIM))
    for rc in range(tm // row_chunk):
        rows = pl.ds(rc * row_chunk, row_chunk)
        acc = jnp.dot(a_ref[rows, :], w, preferred_element_type=F32)
        c = jnp.where(no_rot, 1.0, c_ref[rows, :])
        s1 = jnp.where(no_rot, 0.0, s1_ref[rows, :])
        s2 = jnp.where(no_rot, 0.0, s2_ref[rows, :])
        for hh in range(tn // HEAD_DIM):
            x = acc[:, hh * HEAD_DIM:(hh + 1) * HEAD_DIM]
            ms = jnp.mean(x * x, axis=-1, keepdims=True)
            x = x * jnp.where(use_norm, lax.rsqrt(ms + RMS_EPS), 1.0) * gain
            heads_ref[hh, rows, :] = _rot_head(x, c, s1, s2).astype(heads_ref.dtype)
        wi_ref[rows, :] = acc[:, wi_lane:wi_lane + HEAD_DIM] * wscale


def dsa_in_proj(hn, w_in, layer, tabs, q_gain, k_gain, ki_gain, n_heads, n_kv):
    T, K = hn.shape
    n_cols = w_in.shape[2]
    seg_ends = [n_heads * HEAD_DIM]
    for width in (n_kv * HEAD_DIM, n_kv * HEAD_DIM, IDX_HEADS * HEAD_DIM):
        seg_ends.append(seg_ends[-1] + width)
    tn = next(t for t in (512, 256, 128) if all(e % t == 0 for e in seg_ends))
    hb = tn // HEAD_DIM
    bounds = tuple(e // tn for e in seg_ends)
    b_v, b_qi = bounds[2], bounds[3]
    n_out = -(-n_cols // tn) * tn
    wi_lane = (seg_ends[-1] + HEAD_DIM) % tn
    assert n_cols == seg_ends[-1] + HEAD_DIM + IDX_HEADS and (n_out - tn) <= seg_ends[-1] + HEAD_DIM
    tm = min(2048, T)
    row_chunk = min(512, tm)
    c, s1, s2 = tabs
    gains = jnp.zeros((8, HEAD_DIM), F32).at[0].set(q_gain * (HEAD_DIM ** -0.5 * LOG2_E)).at[1].set(k_gain)
    gains = gains.at[2].set(ki_gain)
    once = dict(pipeline_mode=pl.Buffered(1))
    row = lambda i, j: (i, 0)

    def head_block(i, j):
        pos = jnp.where(j < b_v, j + (b_qi - b_v), jnp.where(j < b_qi, j - b_v, j))
        return (pos, i, 0)

    return pl.pallas_call(
        functools.partial(_dsa_proj_kernel, n_valid=n_cols, bounds=bounds, wi_lane=wi_lane,
                          wscale=IDX_HEADS ** -0.5 * HEAD_DIM ** -0.5, row_chunk=row_chunk),
        grid=(T // tm, n_out // tn),
        in_specs=[pl.BlockSpec((tm, K), row, **once),
                  pl.BlockSpec((None, K, tn), lambda i, j: (layer, 0, j)),
                  pl.BlockSpec((tm, HEAD_DIM), row, **once), pl.BlockSpec((tm, HEAD_DIM), row, **once),
                  pl.BlockSpec((tm, HEAD_DIM), row, **once), pl.BlockSpec((8, HEAD_DIM), lambda i, j: (0, 0))],
        out_specs=[pl.BlockSpec((hb, tm, HEAD_DIM), head_block), pl.BlockSpec((tm, HEAD_DIM), row)],
        out_shape=[jax.ShapeDtypeStruct((n_out // HEAD_DIM, T, HEAD_DIM), BF16),
                   jax.ShapeDtypeStruct((T, HEAD_DIM), F32)],
        compiler_params=_cparams(("parallel", "arbitrary")),
        name="dsa_in_proj",
    )(hn, w_in, c, s1, s2, gains)


_NT = (((1,), (1,)), ((), ()))


IDX_UNROLL = 4
LOG2_E = 1.4426950408889634
SOFTMAX_SHIFT_MAX = 40.0
NORM_SLACK = 1.0 + 2.0 ** -6


def _dsa_tile(bound_ref, qi_ref, ki_ref, wi_ref, q_ref, k_ref, v_ref, o_ref, sc_ref, u_ref, bias_ref,
              *, top_k, sub, steps_per_kv, nk, q0):
    tq = sc_ref.shape[1]
    keys = pl.ds(0, nk)

    def idx_term(h):
        s = lax.dot_general(ki_ref[keys, :], qi_ref[h], _NT, preferred_element_type=F32)
        return jnp.maximum(s, 0.0) * wi_ref[pl.ds(h, 1), :]

    def idx_terms(h0):
        acc = idx_term(h0)
        for d in range(1, IDX_UNROLL):
            acc = acc + idx_term(h0 + d)
        return acc

    sc_ref[keys, :] = idx_terms(0)

    def idx_body(g, carry):
        sc_ref[keys, :] += idx_terms(g * IDX_UNROLL)
        return carry

    lax.fori_loop(1, IDX_HEADS // IDX_UNROLL, idx_body, 0)

    key_pos = lax.broadcasted_iota(I32, (nk, tq), 0)
    q_pos = q0 + lax.broadcasted_iota(I32, (nk, tq), 1)
    causal = key_pos <= q_pos
    bits = pltpu.bitcast(sc_ref[keys, :], I32)
    u = bits ^ ((bits >> 31) & 0x7FFFFFFF)
    u_ref[keys, :] = jnp.where(causal, u, INT_MIN)

    def count_ge(cand):
        return jnp.sum(jnp.where(u_ref[keys, :] >= cand, 1.0, 0.0), axis=0, keepdims=True)

    zero = jnp.zeros((1, tq), I32)
    v0 = jnp.where(count_ge(zero) >= top_k, zero, jnp.full((1, tq), INT_MIN, I32))

    def bis_body(it, v):
        cand = v + lax.shift_left(jnp.int32(1), 30 - it)
        return jnp.where(count_ge(cand) >= top_k, cand, v)

    thr = lax.fori_loop(0, 31, bis_body, v0)
    keep = jnp.logical_and(u_ref[keys, :] >= thr, causal)
    bias_ref[:, keys] = jnp.transpose(jnp.where(keep, 0.0, NEG_BIG))

    bound = bound_ref[0]
    bound_ok = bound <= SOFTMAX_SHIFT_MAX

    def att_body(shift_by_bound, st, carry):
        n = st // steps_per_kv
        q = q_ref[pl.ds(st * sub, sub)].reshape(sub * tq, HEAD_DIM)
        lg = lax.dot_general(q, k_ref[n, keys, :], _NT, preferred_element_type=F32)
        ps, ls = [], []
        for g in range(sub):
            x = lg[g * tq:(g + 1) * tq] + bias_ref[:, keys]
            shift = bound if shift_by_bound else jnp.max(x, axis=-1, keepdims=True)
            p = jnp.exp2(x - shift)
            ls.append(jnp.sum(p, axis=-1, keepdims=True))
            ps.append(p.astype(BF16))
        o = jnp.dot(jnp.concatenate(ps, axis=0), v_ref[n, keys, :], preferred_element_type=F32)
        for g in range(sub):
            o_ref[st * sub + g] = (o[g * tq:(g + 1) * tq] / ls[g]).astype(o_ref.dtype)
        return carry

    n_steps = q_ref.shape[0] // sub

    @pl.when(bound_ok)
    def _():
        lax.fori_loop(0, n_steps, functools.partial(att_body, True), 0)

    @pl.when(jnp.logical_not(bound_ok))
    def _():
        lax.fori_loop(0, n_steps, functools.partial(att_body, False), 0)


def _dsa_kernel(*refs, **static):
    S, tq = refs[8].shape
    i = pl.program_id(1)
    for c in range(S // tq):
        @pl.when(i == c)
        def _(c=c):
            _dsa_tile(*refs, nk=(c + 1) * tq, q0=c * tq, **static)


def dsa_attention_core(heads, wi, logit_bound, n_heads, n_kv, B, S, top_k):
    T = heads.shape[1]
    tq = _pick(S, (256, 128))
    nq = S // tq
    group = n_heads // n_kv
    sub = 2 if group % 2 == 0 else 1
    off_q, off_k = IDX_HEADS, IDX_HEADS + n_heads
    off_v, off_ki = off_k + n_kv, off_k + 2 * n_kv
    assert off_q % n_heads == 0 and off_k % n_kv == 0 and off_v % n_kv == 0
    once = dict(pipeline_mode=pl.Buffered(1))
    return pl.pallas_call(
        functools.partial(_dsa_kernel, top_k=top_k, sub=sub, steps_per_kv=group // sub),
        grid=(B, nq),
        in_specs=[pl.BlockSpec(memory_space=pltpu.SMEM),
                  pl.BlockSpec((IDX_HEADS, tq, HEAD_DIM), lambda b, i: (0, b * nq + i, 0)),
                  pl.BlockSpec((None, S, HEAD_DIM), lambda b, i: (off_ki, b, 0), **once),
                  pl.BlockSpec((IDX_HEADS, tq), lambda b, i: (0, b * nq + i)),
                  pl.BlockSpec((n_heads, tq, HEAD_DIM), lambda b, i: (off_q // n_heads, b * nq + i, 0)),
                  pl.BlockSpec((n_kv, S, HEAD_DIM), lambda b, i: (off_k // n_kv, b, 0), **once),
                  pl.BlockSpec((n_kv, S, HEAD_DIM), lambda b, i: (off_v // n_kv, b, 0), **once)],
        out_specs=pl.BlockSpec((n_heads, tq, HEAD_DIM), lambda b, i: (0, b * nq + i, 0)),
        out_shape=jax.ShapeDtypeStruct((n_heads, T, HEAD_DIM), BF16),
        scratch_shapes=[pltpu.VMEM((S, tq), F32), pltpu.VMEM((S, tq), I32), pltpu.VMEM((tq, S), F32)],
        compiler_params=_cparams(("parallel", "arbitrary")),
        name="dsa_core",
    )(logit_bound, heads, heads, wi, heads, heads, heads)


def _pack_bf16_pair(t):
    n = t.shape[1] // 2
    hi = pltpu.bitcast(t[:, :n].astype(BF16).astype(F32), jnp.uint32)
    lo = pltpu.bitcast(t[:, n:].astype(BF16).astype(F32), jnp.uint32)
    return hi | lax.shift_right_logical(lo, jnp.uint32(16))


def _route_kernel(x_ref, g_ref, w_ref, b_ref, o_ref, tp_ref):
    x = x_ref[...]
    ms = jnp.mean(x * x, axis=-1, keepdims=True)
    t = x * lax.rsqrt(ms + RMS_EPS) * g_ref[...]
    tp_ref[...] = _pack_bf16_pair(t)
    t_hi = t.astype(BF16)
    t_lo = (t - t_hi.astype(F32)).astype(BF16)
    w_hi, w_lo = w_ref[0], w_ref[1]
    logits = (jnp.dot(t_hi, w_hi, preferred_element_type=F32) + jnp.dot(t_hi, w_lo, preferred_element_type=F32)
              + jnp.dot(t_lo, w_hi, preferred_element_type=F32)) + b_ref[...]
    lane = lax.broadcasted_iota(I32, logits.shape, 1)
    lane_f = lane.astype(F32)
    big = float(LANES)
    gl = jnp.where(lane < N_GROUPS, logits, -jnp.inf)
    gmax = jnp.max(gl, axis=-1, keepdims=True)
    g_sel = jnp.min(jnp.where(gl == gmax, lane_f, big), axis=-1, keepdims=True)
    g_w = 1.0 / jnp.sum(jnp.exp(gl - gmax), axis=-1, keepdims=True)
    lo = N_GROUPS + g_sel * EXPERTS_PER_GROUP
    in_grp = jnp.logical_and(lane_f >= lo, lane_f < lo + EXPERTS_PER_GROUP)
    el = jnp.where(in_grp, logits, -jnp.inf)
    m1 = jnp.max(el, axis=-1, keepdims=True)
    i1 = jnp.min(jnp.where(el == m1, lane_f, big), axis=-1, keepdims=True)
    el2 = jnp.where(lane_f == i1, -jnp.inf, el)
    m2 = jnp.max(el2, axis=-1, keepdims=True)
    i2 = jnp.min(jnp.where(el2 == m2, lane_f, big), axis=-1, keepdims=True)
    z = jnp.sum(jnp.exp(el - m1), axis=-1, keepdims=True)
    p1 = 1.0 / z
    p2 = jnp.exp(m2 - m1) / z
    w1 = p1 / (p1 + p2) * g_w
    w2 = p2 / (p1 + p2) * g_w
    out = jnp.where(lane == 0, i1 - N_GROUPS, jnp.where(lane == 1, i2 - N_GROUPS,
          jnp.where(lane == 2, w1, jnp.where(lane == 3, w2, 0.0))))
    o_ref[...] = out


def moe_route(x2d, gain, w_group, b_group, w_expert, b_expert):
    T, D = x2d.shape
    n_log = w_group.shape[1] + w_expert.shape[1]
    w = jnp.concatenate([w_group, w_expert, jnp.zeros((D, LANES - n_log), F32)], axis=1)
    b = jnp.concatenate([b_group, b_expert, jnp.zeros((LANES - n_log,), F32)]).reshape(1, LANES)
    w_hi = w.astype(BF16)
    w = jnp.stack([w_hi, (w - w_hi.astype(F32)).astype(BF16)])
    tm = _pick(T, (256, 128, 8))
    return pl.pallas_call(
        _route_kernel,
        grid=(T // tm,),
        in_specs=[pl.BlockSpec((tm, D), lambda i: (i, 0)), pl.BlockSpec((1, D), lambda i: (0, 0)),
                  pl.BlockSpec((2, D, LANES), lambda i: (0, 0, 0)), pl.BlockSpec((1, LANES), lambda i: (0, 0))],
        out_specs=[pl.BlockSpec((tm, LANES), lambda i: (i, 0)), pl.BlockSpec((tm, D // 2), lambda i: (i, 0))],
        out_shape=[jax.ShapeDtypeStruct((T, LANES), F32), jax.ShapeDtypeStruct((T, D // 2), jnp.uint32)],
        compiler_params=_cparams(("parallel",)),
        name="moe_route",
    )(x2d, gain.reshape(1, D), w, b)


def _row_copy(src_hbm, dst_ref, sem, src_row, dst_row):
    return pltpu.make_async_copy(src_hbm.at[pl.ds(src_row, 1), :], dst_ref.at[pl.ds(dst_row, 1), :], sem)


def _start_row_gather(idx_ref, src_hbm, dst_ref, sem, base, n_rows, priorities=(0, 1)):
    def start(g, c):
        for d, pr in enumerate(priorities):
            r = g * len(priorities) + d
            _row_copy(src_hbm, dst_ref, sem, idx_ref[base + r], r).start(priority=pr)
        return c

    lax.fori_loop(0, n_rows // len(priorities), start, 0)


def _wait_row_gather(src_hbm, dst_ref, sem, n_rows):
    pltpu.make_async_copy(src_hbm.at[pl.ds(0, n_rows), :], dst_ref, sem).wait()


FFN_SLOTS = 3
FFN_GATHER_PRIORITIES = (0, 1)


def _unpack_bf16_pair(words):
    hi = pltpu.bitcast(words & jnp.uint32(0xFFFF0000), F32).astype(BF16)
    lo = pltpu.bitcast(lax.shift_left(words, jnp.uint32(16)), F32).astype(BF16)
    return jnp.concatenate([hi, lo], axis=1)


def _ffn_kernel(src_ref, te_ref, nu_ref, t_hbm, wg_ref, wu_ref, wd_ref, o_ref, xbuf, sem, *, priorities):
    j = pl.program_id(0)
    tm = o_ref.shape[0]
    n_used = nu_ref[0]
    ahead = FFN_SLOTS - 1

    def fetch(tile, slot):
        _start_row_gather(src_ref, t_hbm, xbuf.at[slot], sem.at[slot], tile * tm, tm, priorities)

    for t in range(ahead):
        @pl.when(jnp.logical_and(j == 0, t < n_used))
        def _(t=t):
            fetch(t, t)

    @pl.when(j + ahead < n_used)
    def _():
        fetch(j + ahead, lax.rem(j + ahead, FFN_SLOTS))

    @pl.when(j < n_used)
    def _():
        slot = lax.rem(j, FFN_SLOTS)
        _wait_row_gather(t_hbm, xbuf.at[slot], sem.at[slot], tm)
        t = _unpack_bf16_pair(xbuf[slot])
        a = jnp.dot(t, wg_ref[0].astype(BF16), preferred_element_type=F32)
        u = jnp.dot(t, wu_ref[0].astype(BF16), preferred_element_type=F32)
        hid = a * jax.nn.sigmoid(a) * u
        o_ref[...] = jnp.dot(hid.astype(BF16), wd_ref[0].astype(BF16), preferred_element_type=F32)

    @pl.when(j >= n_used)
    def _():
        o_ref[...] = jnp.zeros_like(o_ref)


def moe_ffn(t_packed, src_tok, w_gate, w_up, w_down, tile_expert, n_used, tm, priorities):
    D = 2 * t_packed.shape[1]
    P = src_tok.shape[0]
    F = w_gate.shape[-1]
    assert P % tm == 0 and tm % len(priorities) == 0
    wmap = lambda j, src, te, nu: (te[j], 0, 0)
    return pl.pallas_call(
        functools.partial(_ffn_kernel, priorities=priorities),
        grid_spec=pltpu.PrefetchScalarGridSpec(
            num_scalar_prefetch=3, grid=(P // tm,),
            in_specs=[pl.BlockSpec(memory_space=pl.ANY),
                      pl.BlockSpec((1, D, F), wmap), pl.BlockSpec((1, D, F), wmap),
                      pl.BlockSpec((1, F, D), wmap)],
            out_specs=pl.BlockSpec((tm, D), lambda j, src, te, nu: (j, 0)),
            scratch_shapes=[pltpu.VMEM((FFN_SLOTS, tm, D // 2), jnp.uint32),
                            pltpu.SemaphoreType.DMA((FFN_SLOTS,))]),
        out_shape=jax.ShapeDtypeStruct((P, D), F32),
        compiler_params=_cparams(("arbitrary",)),
        name="moe_ffn",
    )(src_tok, tile_expert, n_used, t_packed, w_gate, w_up, w_down)


ROUTE_W_LANE = 2


def _combine_kernel(p1_ref, p2_ref, ys_hbm, x_ref, route_ref, gain_ref, o_ref, on_ref, buf, sem):
    i = pl.program_id(0)
    R = x_ref.shape[0]

    def fetch(tile, slot):
        _start_row_gather(p1_ref, ys_hbm, buf.at[2 * slot], sem.at[2 * slot], tile * R, R)
        _start_row_gather(p2_ref, ys_hbm, buf.at[2 * slot + 1], sem.at[2 * slot + 1], tile * R, R)

    @pl.when(i == 0)
    def _():
        fetch(0, 0)

    @pl.when(i + 1 < pl.num_programs(0))
    def _():
        fetch(i + 1, lax.rem(i + 1, 2))

    slot = lax.rem(i, 2)
    _wait_row_gather(ys_hbm, buf.at[2 * slot], sem.at[2 * slot], R)
    _wait_row_gather(ys_hbm, buf.at[2 * slot + 1], sem.at[2 * slot + 1], R)
    w1 = route_ref[:, ROUTE_W_LANE:ROUTE_W_LANE + 1]
    w2 = route_ref[:, ROUTE_W_LANE + 1:ROUTE_W_LANE + 2]
    y = x_ref[...] + (buf[2 * slot] * w1 + buf[2 * slot + 1] * w2)
    o_ref[...] = y
    ms = jnp.mean(y * y, axis=-1, keepdims=True)
    on_ref[...] = (y * lax.rsqrt(ms + RMS_EPS) * gain_ref[...]).astype(on_ref.dtype)


def moe_combine(x2d, ys, pos1, pos2, route, next_gain, rows_per_step):
    T, D = x2d.shape
    R = rows_per_step
    tile = lambda i, a, b: (i, 0)
    return pl.pallas_call(
        _combine_kernel,
        grid_spec=pltpu.PrefetchScalarGridSpec(
            num_scalar_prefetch=2, grid=(T // R,),
            in_specs=[pl.BlockSpec(memory_space=pl.ANY), pl.BlockSpec((R, D), tile),
                      pl.BlockSpec((R, LANES), tile), pl.BlockSpec((1, D), lambda i, a, b: (0, 0))],
            out_specs=[pl.BlockSpec((R, D), tile), pl.BlockSpec((R, D), tile)],
            scratch_shapes=[pltpu.VMEM((4, R, D), F32), pltpu.SemaphoreType.DMA((4,))]),
        out_shape=[jax.ShapeDtypeStruct((T, D), F32), jax.ShapeDtypeStruct((T, D), BF16)],
        compiler_params=_cparams(("arbitrary",)),
        name="moe_combine",
    )(pos1, pos2, ys, x2d, route, next_gain.reshape(1, D))


def hier_moe_residual(x2d, gain, w_group, b_group, w_expert, b_expert, w_gate, w_up, w_down, layer, next_gain):
    T, D = x2d.shape
    depth, G, E = w_gate.shape[0], w_gate.shape[1], w_gate.shape[2]
    n_exp = G * E
    F = w_gate.shape[-1]
    route, t_packed = moe_route(x2d, gain, w_group, b_group, w_expert, b_expert)
    eid = route[:, 0:2].astype(I32).reshape(2 * T)
    tm = 256 if 2 * T >= 4096 else 128
    nt = (2 * T) // tm + n_exp
    onehot = (eid[None, :] == jnp.arange(n_exp, dtype=I32)[:, None]).astype(I32)
    csum = jnp.cumsum(onehot, axis=1)
    counts = csum[:, -1]
    rank = jnp.sum(onehot * csum, axis=0) - 1
    tiles = (counts + tm - 1) // tm
    tile_end = jnp.cumsum(tiles)
    tile_start = tile_end - tiles
    n_used = tile_end[-1:]
    j = jnp.arange(nt, dtype=I32)
    tile_expert = jnp.minimum(jnp.sum(tile_end[None, :] <= j[:, None], axis=1, dtype=I32), n_exp - 1)
    last_expert = jnp.sum(tile_end < n_used[0], dtype=I32)
    tile_expert = jnp.where(j < n_used[0], tile_expert, last_expert)
    pos = (tile_start[eid] * tm + rank).astype(I32)
    src_tok = jnp.zeros((nt * tm,), I32).at[pos].set(jnp.arange(2 * T, dtype=I32) // 2)
    pos = pos.reshape(T, 2)

    ys = moe_ffn(t_packed, src_tok, w_gate.reshape(depth * n_exp, D, F), w_up.reshape(depth * n_exp, D, F),
                 w_down.reshape(depth * n_exp, F, D), tile_expert + layer * n_exp, n_used, tm,
                 FFN_GATHER_PRIORITIES)
    return moe_combine(x2d, ys, pos[:, 0], pos[:, 1], route, next_gain, min(256, T))


def _ple_kernel(a_ref, w_ref, p_ref, pw_ref, x_ref, o_ref, *, row_chunk):
    w = w_ref[...].astype(BF16)
    pw = pw_ref[...].astype(BF16)
    for rc in range(o_ref.shape[0] // row_chunk):
        rows = pl.ds(rc * row_chunk, row_chunk)
        z = jnp.dot(a_ref[rows, :], w, preferred_element_type=F32)
        e = jnp.dot(p_ref[rows, :].astype(BF16), pw, preferred_element_type=F32)
        o_ref[rows, :] = x_ref[rows, :] + jax.nn.sigmoid(z) * e


def ple_residual(x2d, a, w_gate, p3d, w_proj, layer):
    T, D = x2d.shape
    P = p3d.shape[2]
    tm, tn = min(2048, T), 256
    return pl.pallas_call(
        functools.partial(_ple_kernel, row_chunk=min(ROW_CHUNK, tm)),
        grid=(T // tm, D // tn),
        in_specs=[pl.BlockSpec((tm, D), lambda i, j: (i, 0), pipeline_mode=pl.Buffered(1)),
                  pl.BlockSpec((None, D, tn), lambda i, j: (layer, 0, j)),
                  pl.BlockSpec((None, tm, P), lambda i, j: (layer, i, 0)),
                  pl.BlockSpec((None, P, tn), lambda i, j: (layer, 0, j)),
                  pl.BlockSpec((tm, tn), lambda i, j: (i, j))],
        out_specs=pl.BlockSpec((tm, tn), lambda i, j: (i, j)),
        out_shape=jax.ShapeDtypeStruct((T, D), F32),
        compiler_params=_cparams(("parallel", "parallel")),
        name="ple",
    )(a, w_gate, p3d, w_proj, x2d)


def _ret_proj_kernel(a_ref, w_ref, cos_ref, sin_ref, qkv_ref, *, n_qk, row_chunk):
    j = pl.program_id(1)
    tm = a_ref.shape[0]
    tn = w_ref.shape[1]
    half = RET_QK_DIM // 2
    w = w_ref[...].astype(BF16)
    shape = (row_chunk, half)
    is_rot = _flag(j < n_qk, shape)
    kscale = jnp.where(_flag(jnp.logical_and(j >= n_qk // 2, j < n_qk), shape), RET_QK_DIM ** -0.5, 1.0)
    for rc in range(tm // row_chunk):
        rows = pl.ds(rc * row_chunk, row_chunk)
        acc = jnp.dot(a_ref[rows, :], w, preferred_element_type=F32)
        cos = jnp.where(is_rot, cos_ref[rows, :], 1.0) * kscale
        sin = jnp.where(is_rot, sin_ref[rows, :], 0.0) * kscale
        for hh in range(tn // RET_QK_DIM):
            x1 = acc[:, hh * RET_QK_DIM:hh * RET_QK_DIM + half]
            x2 = acc[:, hh * RET_QK_DIM + half:(hh + 1) * RET_QK_DIM]
            qkv_ref[rows, hh * RET_QK_DIM:hh * RET_QK_DIM + half] = (x1 * cos - x2 * sin).astype(BF16)
            qkv_ref[rows, hh * RET_QK_DIM + half:(hh + 1) * RET_QK_DIM] = (x1 * sin + x2 * cos).astype(BF16)


def ret_in_proj(hn, w_in, layer, cos, sin, n_heads):
    T, K = hn.shape
    dk = RET_QK_DIM
    qkv_cols = 4 * n_heads * dk
    tn = 512 if (n_heads * dk) % 512 == 0 else dk
    tm = min(2048, T)
    once = dict(pipeline_mode=pl.Buffered(1))
    row = lambda i, j: (i, 0)
    qkv = pl.pallas_call(
        functools.partial(_ret_proj_kernel, n_qk=2 * n_heads * dk // tn, row_chunk=min(512, tm)),
        grid=(T // tm, qkv_cols // tn),
        in_specs=[pl.BlockSpec((tm, K), row, **once),
                  pl.BlockSpec((None, K, tn), lambda i, j: (layer, 0, j)),
                  pl.BlockSpec((tm, dk // 2), row, **once), pl.BlockSpec((tm, dk // 2), row, **once)],
        out_specs=pl.BlockSpec((tm, tn), lambda i, j: (i, j)),
        out_shape=jax.ShapeDtypeStruct((T, qkv_cols), BF16),
        compiler_params=_cparams(("parallel", "parallel")),
        name="ret_qkv_proj",
    )(hn, w_in, cos, sin)
    g = matmul(hn, w_in, layer, col0=qkv_cols, tm=tm, tn=tn, name="ret_g_proj")
    return qkv, g


RET_HEADS_PER_STEP = 2


def _retention_kernel(q_ref, k_ref, v_ref, g_ref, intra_ref, qd_ref, kd_ref, cd_ref, gn_ref, o_ref, state_ref):
    S = q_ref.shape[0]
    C = intra_ref.shape[1]
    dk = RET_QK_DIM
    dv = 2 * dk
    state_ref[...] = jnp.zeros_like(state_ref)

    def chunk(c, carry):
        rows = pl.ds(pl.multiple_of(c * C, C), C)
        for hh in range(RET_HEADS_PER_STEP):
            qc = q_ref[rows, hh * dk:(hh + 1) * dk]
            kc = k_ref[rows, hh * dk:(hh + 1) * dk]
            vc = v_ref[rows, hh * dv:(hh + 1) * dv]
            att = lax.dot_general(qc, kc, _NT, preferred_element_type=F32) * intra_ref[hh]
            inner = jnp.dot(att.astype(BF16), vc, preferred_element_type=F32)
            state = state_ref[hh]
            cross = jnp.dot((qc.astype(F32) * qd_ref[hh]).astype(BF16), state.astype(BF16),
                            preferred_element_type=F32)
            kt = jnp.transpose(kc.astype(F32) * kd_ref[hh]).astype(BF16)
            state_ref[hh] = state * cd_ref[hh] + jnp.dot(kt, vc, preferred_element_type=F32)
            o = inner + cross
            o = o * lax.rsqrt(jnp.mean(o * o, axis=-1, keepdims=True) + RMS_EPS) * gn_ref[hh]
            g = g_ref[rows, hh * dv:(hh + 1) * dv]
            o_ref[rows, hh * dv:(hh + 1) * dv] = (g * jax.nn.sigmoid(g) * o).astype(o_ref.dtype)
        return carry

    lax.fori_loop(0, S // C, chunk, 0)


def retention_core(qkv, g, out_norm, B, S, n_heads):
    T = qkv.shape[0]
    dk = RET_QK_DIM
    dv = 2 * dk
    C = RET_CHUNK
    hs = RET_HEADS_PER_STEP
    assert n_heads % hs == 0
    log_gamma = jnp.log1p(-jnp.exp2(-5.0 - jnp.arange(n_heads, dtype=F32)))
    i = jnp.arange(C, dtype=F32)
    dist = i[:, None] - i[None, :]
    intra = jnp.where(dist[None] >= 0, jnp.exp(log_gamma[:, None, None] * jnp.maximum(dist, 0.0)[None]), 0.0)
    q_decay = jnp.broadcast_to(jnp.exp(log_gamma[:, None] * (i[None, :] + 1.0))[:, :, None], (n_heads, C, dk))
    k_decay = jnp.broadcast_to(jnp.exp(log_gamma[:, None] * (C - 1.0 - i)[None, :])[:, :, None], (n_heads, C, dk))
    chunk_decay = jnp.broadcast_to(jnp.exp(log_gamma * C)[:, None, None], (n_heads, 1, dv))
    nq = n_heads // hs
    per_head = lambda b, h: (h, 0, 0)
    return pl.pallas_call(
        _retention_kernel,
        grid=(B, nq),
        in_specs=[pl.BlockSpec((S, hs * dk), lambda b, h: (b, h)),
                  pl.BlockSpec((S, hs * dk), lambda b, h: (b, nq + h)),
                  pl.BlockSpec((S, hs * dv), lambda b, h: (b, nq + h)),
                  pl.BlockSpec((S, hs * dv), lambda b, h: (b, h)),
                  pl.BlockSpec((hs, C, C), per_head), pl.BlockSpec((hs, C, dk), per_head),
                  pl.BlockSpec((hs, C, dk), per_head), pl.BlockSpec((hs, 1, dv), per_head),
                  pl.BlockSpec((hs, 1, dv), per_head)],
        out_specs=pl.BlockSpec((S, hs * dv), lambda b, h: (b, h)),
        out_shape=jax.ShapeDtypeStruct((T, n_heads * dv), BF16),
        scratch_shapes=[pltpu.VMEM((hs, dk, dv), F32)],
        compiler_params=_cparams(("parallel", "arbitrary")),
        name="retention_core",
    )(qkv, qkv, qkv, g, intra, q_decay, k_decay, chunk_decay, out_norm.reshape(n_heads, 1, dv))


def _rot_tables(positions, dim, theta):
    inv = theta ** (-jnp.arange(0, dim, 2, dtype=F32) / dim)
    ang = positions.astype(F32).reshape(-1, 1) * inv[None, :]
    return jnp.cos(ang), jnp.sin(ang)


def dsa_layer(x2d, positions, gain, w_in, w_out, q_norm, k_norm, idx_k_norm, layer, B, S):
    T, D = x2d.shape
    n_heads = D // HEAD_DIM
    n_kv = n_heads // 4
    rot_dim = HEAD_DIM // 4
    cos, sin = _rot_tables(positions, rot_dim, ROPE_THETA)
    ones = jnp.ones((T, HEAD_DIM - rot_dim), F32)
    zeros_r = jnp.zeros((T, HEAD_DIM - rot_dim // 2), F32)
    tabs = (jnp.concatenate([cos, cos, ones], axis=1),
            jnp.concatenate([-sin, zeros_r], axis=1),
            jnp.concatenate([jnp.zeros_like(sin), sin, jnp.zeros((T, HEAD_DIM - rot_dim), F32)], axis=1))
    hn = rmsnorm_cast(x2d, gain)
    heads, wi = dsa_in_proj(hn, w_in, layer, tabs, q_norm, k_norm, idx_k_norm, n_heads, n_kv)
    top_k = min(IDX_TOPK_MAX, S // 4)
    q_cap = HEAD_DIM ** 0.5 * jnp.max(jnp.abs(q_norm)) * (HEAD_DIM ** -0.5 * LOG2_E) * NORM_SLACK
    k_cap = HEAD_DIM ** 0.5 * jnp.max(jnp.abs(k_norm)) * NORM_SLACK
    o = dsa_attention_core(heads, jnp.transpose(wi[:, :IDX_HEADS]), (q_cap * k_cap).reshape(1), n_heads, n_kv,
                           B, S, top_k)
    return matmul(o, w_out, layer, res=x2d, head_major=True, tm=2048, tn=512, name="dsa_out_proj")


def retention_layer(x2d, positions, gain, w_in, w_out, out_norm, layer, B, S):
    T, D = x2d.shape
    n_heads = D // RET_QK_DIM
    cos, sin = _rot_tables(positions, RET_QK_DIM, RET_THETA)
    hn = rmsnorm_cast(x2d, gain)
    qkv, g = ret_in_proj(hn, w_in, layer, cos, sin, n_heads)
    og = retention_core(qkv, g, out_norm, B, S, n_heads)
    return matmul(og, w_out.astype(BF16), layer, res=x2d, tm=2048, tn=256, name="ret_out_proj")


def kernel(x, p, positions, attn_norm, ffn_norm, ple_norm, a_w_in, a_w_out, a_q_norm, a_k_norm, a_idx_k_norm,
           b_w_in, b_w_out, b_out_norm, moe_w_group, moe_b_group, moe_w_expert, moe_b_expert, moe_w_gate,
           moe_w_up, moe_w_down, ple_proj, ple_gate):
    B, S, D = x.shape
    depth = p.shape[0]
    T = B * S
    x2d = x.reshape(T, D)
    p3d = p.reshape(depth, T, p.shape[-1])
    for i in range(depth):
        j = i // 2
        if i % 2 == 0:
            x2d = dsa_layer(x2d, positions, attn_norm[i], a_w_in, a_w_out, a_q_norm[j], a_k_norm[j],
                            a_idx_k_norm[j], j, B, S)
        else:
            x2d = retention_layer(x2d, positions, attn_norm[i], b_w_in, b_w_out, b_out_norm[j], j, B, S)
        x2d, xn = hier_moe_residual(x2d, ffn_norm[i], moe_w_group[i], moe_b_group[i], moe_w_expert[i],
                                    moe_b_expert[i], moe_w_gate, moe_w_up, moe_w_down, i, ple_norm[i])
        x2d = ple_residual(x2d, xn, ple_gate, p3d, ple_proj, i)
    return x2d.reshape(B, S, D)
```
